```python
import jax, jax.numpy as jnp
from jax import lax
import numpy as np

D_MODEL = 2048
BATCH = 4
SEQ = 8192
DEPTH = 2
DEC_BATCH = 1
DEC_SEQ = 16384
PAST_LEN = 128

N_MIXERS = 2
N_A_LAYERS = (DEPTH + N_MIXERS - 1) // N_MIXERS
N_B_LAYERS = DEPTH // N_MIXERS
RMS_EPS = 1e-6
L2_EPS = 1e-6

GDN_QK_HEADS = 16
GDN_V_HEADS = 32
GDN_DK = 128
GDN_DV = 128
GDN_CONV = 5
GDN_CHUNK = 64
GDN_Q_DIM = GDN_QK_HEADS * GDN_DK
GDN_V_DIM = GDN_V_HEADS * GDN_DV
GDN_CONV_DIM = 2 * GDN_Q_DIM + GDN_V_DIM
GDN_PROJ_DIM = GDN_CONV_DIM + GDN_V_DIM + 4 * GDN_V_HEADS

MLA_HEADS = 16
MLA_Q_RANK = 768
MLA_KV_RANK = 512
MLA_D_NOPE = 128
MLA_D_ROPE = 64
MLA_D_V = 128
MLA_D_QK = MLA_D_NOPE + MLA_D_ROPE
MLA_A_DIM = MLA_Q_RANK + MLA_KV_RANK + MLA_D_ROPE
ROPE_THETA = 10000.0
Q_BLOCK = 128

D_FF = 4 * D_MODEL

kernel_name = 'hybrid_gdn_mla_bidir_encoder'


def rms_norm(x, w):
    xf = x.astype(jnp.float32)
    y = xf * lax.rsqrt(jnp.mean(xf * xf, axis=-1, keepdims=True) + RMS_EPS)
    return (y * w.astype(jnp.float32)).astype(x.dtype)


def l2_normalize(x):
    xf = x.astype(jnp.float32)
    return xf * lax.rsqrt(jnp.sum(xf * xf, axis=-1, keepdims=True) + L2_EPS)


def centred_depthwise_conv(x, w):
    width = w.shape[0]
    r = width // 2
    L = x.shape[1]
    xp = jnp.pad(x, ((0, 0), (r, r), (0, 0)))
    out = xp[:, 0:L] * w[0]
    for j in range(1, width):
        out = out + xp[:, j:j + L] * w[j]
    return out


def chunk_gated_delta_rule(q, k, v, g, beta):
    B, H, L, DK = q.shape
    DV = v.shape[-1]
    C = GDN_CHUNK
    N = L // C
    q = q.reshape(B, H, N, C, DK)
    k = k.reshape(B, H, N, C, DK)
    v = v.reshape(B, H, N, C, DV)
    g = jnp.cumsum(g.reshape(B, H, N, C), axis=-1)
    beta = beta.reshape(B, H, N, C)
    tril = jnp.tril(jnp.ones((C, C), dtype=bool))
    strict = jnp.tril(jnp.ones((C, C), dtype=bool), -1)
    diff = g[..., :, None] - g[..., None, :]
    decay = jnp.where(tril, jnp.exp(jnp.where(tril, diff, 0.0)), 0.0)
    k_beta = k * beta[..., None]
    v_beta = v * beta[..., None]
    a = jnp.where(strict, jnp.einsum('bhncd,bhnsd->bhncs', k_beta, k) * decay, 0.0)
    rhs = jnp.concatenate([v_beta, k_beta * jnp.exp(g)[..., None]], axis=-1)
    sol = lax.linalg.triangular_solve(a, rhs, left_side=True, lower=True, unit_diagonal=True)
    u = sol[..., :DV]
    w = sol[..., DV:]
    qk = jnp.where(tril, jnp.einsum('bhncd,bhnsd->bhncs', q, k) * decay, 0.0)
    g_last = g[..., -1]
    q_dec = q * jnp.exp(g)[..., None]
    k_dec = k * jnp.exp(g_last[..., None] - g)[..., None]
    xs = (jnp.moveaxis(qk, 2, 0), jnp.moveaxis(q_dec, 2, 0), jnp.moveaxis(k_dec, 2, 0),
          jnp.moveaxis(u, 2, 0), jnp.moveaxis(w, 2, 0), jnp.moveaxis(g_last, 2, 0))

    def step(S, chunk):
        qk_c, q_c, k_c, u_c, w_c, gl = chunk
        v_new = u_c - jnp.einsum('bhcd,bhde->bhce', w_c, S)
        o = jnp.einsum('bhcd,bhde->bhce', q_c, S) + jnp.einsum('bhcs,bhse->bhce', qk_c, v_new)
        S = S * jnp.exp(gl)[..., None, None] + jnp.einsum('bhcd,bhce->bhde', k_c, v_new)
        return S, o

    S0 = jnp.zeros((B, H, DK, DV), jnp.float32)
    _, o = lax.scan(step, S0, xs)
    return jnp.moveaxis(o, 0, 2).reshape(B, H, L, DV)


def gated_deltanet_mixer(h, w_in, conv_w, a_log, dt_bias, norm_w, w_out):
    B, L, _ = h.shape
    f32 = jnp.float32
    proj = h @ w_in
    o_z = GDN_CONV_DIM
    o_b = o_z + GDN_V_DIM
    o_a = o_b + 2 * GDN_V_HEADS
    qkv = jax.nn.silu(centred_depthwise_conv(proj[..., :GDN_CONV_DIM], conv_w))
    z = proj[..., o_z:o_b].reshape(B, L, GDN_V_HEADS, GDN_DV)
    b_logit = proj[..., o_b:o_a].reshape(B, L, 2, GDN_V_HEADS)
    a_in = proj[..., o_a:].reshape(B, L, 2, GDN_V_HEADS)
    rep = GDN_V_HEADS // GDN_QK_HEADS
    q = qkv[..., :GDN_Q_DIM].reshape(B, L, GDN_QK_HEADS, GDN_DK)
    k = qkv[..., GDN_Q_DIM:2 * GDN_Q_DIM].reshape(B, L, GDN_QK_HEADS, GDN_DK)
    v = qkv[..., 2 * GDN_Q_DIM:].reshape(B, L, GDN_V_HEADS, GDN_DV).astype(f32)
    q = jnp.repeat(l2_normalize(q), rep, axis=2) * (GDN_DK ** -0.5)
    k = jnp.repeat(l2_normalize(k), rep, axis=2)
    beta = jax.nn.sigmoid(b_logit.astype(f32))
    g = -jnp.exp(a_log.astype(f32)) * jax.nn.softplus(a_in.astype(f32) + dt_bias.astype(f32))
    q = q.transpose(0, 2, 1, 3)
    k = k.transpose(0, 2, 1, 3)
    v = v.transpose(0, 2, 1, 3)
    g_f, g_b = g[:, :, 0].transpose(0, 2, 1), g[:, :, 1].transpose(0, 2, 1)
    beta_f, beta_b = beta[:, :, 0].transpose(0, 2, 1), beta[:, :, 1].transpose(0, 2, 1)
    flip = lambda t: jnp.flip(t, axis=2)
    o_fwd = chunk_gated_delta_rule(q, k, v, g_f, beta_f)
    o_bwd = flip(chunk_gated_delta_rule(flip(q), flip(k), flip(v), flip(g_b), flip(beta_b)))
    o = (o_fwd + o_bwd).transpose(0, 2, 1, 3)
    o = rms_norm(o, norm_w) * jax.nn.silu(z.astype(f32))
    return o.reshape(B, L, GDN_V_DIM).astype(h.dtype) @ w_out


def rope_tables(L):
    half = MLA_D_ROPE // 2
    inv_freq = ROPE_THETA ** (-jnp.arange(half, dtype=jnp.float32) / half)
    ang = jnp.arange(L, dtype=jnp.float32)[:, None] * inv_freq[None, :]
    return jnp.cos(ang), jnp.sin(ang)


def apply_rope(x, cos, sin):
    half = MLA_D_ROPE // 2
    xf = x.astype(jnp.float32)
    x1, x2 = xf[..., :half], xf[..., half:]
    c, s = cos[:, None, :], sin[:, None, :]
    return jnp.concatenate([x1 * c - x2 * s, x2 * c + x1 * s], axis=-1).astype(x.dtype)


def block_attention(q, k, v):
    B, L, H, D = q.shape
    nb = L // Q_BLOCK
    scale = D ** -0.5
    qb = jnp.moveaxis(q.reshape(B, nb, Q_BLOCK, H, D), 1, 0)

    def one_block(q_blk):
        s = jnp.einsum('bqhd,bkhd->bhqk', q_blk, k).astype(jnp.float32) * scale
        p = jax.nn.softmax(s, axis=-1)
        return jnp.einsum('bhqk,bkhe->bqhe', p.astype(v.dtype), v)

    o = lax.map(one_block, qb)
    return jnp.moveaxis(o, 0, 1).reshape(B, L, H, v.shape[-1])


def mla_mixer(h, w_a, q_a_norm, w_q_b, kv_a_norm, w_kv_b, w_o):
    B, L, _ = h.shape
    a = h @ w_a
    c_q = rms_norm(a[..., :MLA_Q_RANK], q_a_norm)
    c_kv = rms_norm(a[..., MLA_Q_RANK:MLA_Q_RANK + MLA_KV_RANK], kv_a_norm)
    k_rope = a[..., MLA_Q_RANK + MLA_KV_RANK:][:, :, None, :]
    q = (c_q @ w_q_b).reshape(B, L, MLA_HEADS, MLA_D_QK)
    kv = (c_kv @ w_kv_b).reshape(B, L, MLA_HEADS, MLA_D_NOPE + MLA_D_V)
    cos, sin = rope_tables(L)
    q = jnp.concatenate([q[..., :MLA_D_NOPE], apply_rope(q[..., MLA_D_NOPE:], cos, sin)], axis=-1)
    k_rope = jnp.broadcast_to(apply_rope(k_rope, cos, sin), (B, L, MLA_HEADS, MLA_D_ROPE))
    k = jnp.concatenate([kv[..., :MLA_D_NOPE], k_rope], axis=-1)
    v = kv[..., MLA_D_NOPE:]
    o = block_attention(q, k, v)
    return o.reshape(B, L, MLA_HEADS * MLA_D_V) @ w_o


def squared_relu_mlp(h, w_in, w_out):
    return jnp.square(jax.nn.relu(h @ w_in)) @ w_out


def trunk(x, norm_mix_pre, norm_mix_post, norm_ffn_pre, norm_ffn_post,
          gdn_w_in, gdn_conv_w, gdn_a_log, gdn_dt_bias, gdn_norm_w, gdn_w_out,
          mla_w_a, mla_q_a_norm, mla_w_q_b, mla_kv_a_norm, mla_w_kv_b, mla_w_o,
          ffn_w_in, ffn_w_out):
    for i in range(DEPTH):
        j = i // N_MIXERS
        h = rms_norm(x, norm_mix_pre[i])
        if i % N_MIXERS == 0:
            m = gated_deltanet_mixer(h, gdn_w_in[j], gdn_conv_w[j], gdn_a_log[j],
                                     gdn_dt_bias[j], gdn_norm_w[j], gdn_w_out[j])
        else:
            m = mla_mixer(h, mla_w_a[j], mla_q_a_norm[j], mla_w_q_b[j],
                          mla_kv_a_norm[j], mla_w_kv_b[j], mla_w_o[j])
        x = x + rms_norm(m, norm_mix_post[i])
        h = rms_norm(x, norm_ffn_pre[i])
        x = x + rms_norm(squared_relu_mlp(h, ffn_w_in[i], ffn_w_out[i]), norm_ffn_post[i])
    return x


def setup_inputs(seed: int = 0) -> dict:
    key = jax.random.key(seed)
    ks = jax.random.split(key, 24)
    f32 = jnp.float32

    def dense(k, shape, fan_in):
        return jax.random.normal(k, shape, f32) * (fan_in ** -0.5)

    def gain(k, shape):
        return 1.0 + 0.05 * jax.random.normal(k, shape, f32)

    dt = jnp.exp(jax.random.uniform(ks[8], (N_A_LAYERS, 2, GDN_V_HEADS), f32,
                                    np.log(1e-3), np.log(1e-1)))
    return {
        'x_prompt': jax.random.normal(ks[0], (BATCH, SEQ, D_MODEL), f32),
        'x_sample': jax.random.normal(ks[1], (DEC_BATCH, DEC_SEQ, D_MODEL), f32),
        'norm_mix_pre': gain(ks[2], (DEPTH, D_MODEL)),
        'norm_mix_post': gain(ks[3], (DEPTH, D_MODEL)),
        'norm_ffn_pre': gain(ks[4], (DEPTH, D_MODEL)),
        'norm_ffn_post': gain(ks[5], (DEPTH, D_MODEL)),
        'gdn_w_in': dense(ks[6], (N_A_LAYERS, D_MODEL, GDN_PROJ_DIM), D_MODEL),
        'gdn_conv_w': dense(ks[7], (N_A_LAYERS, GDN_CONV, GDN_CONV_DIM), GDN_CONV),
        'gdn_a_log': jnp.log(jax.random.uniform(ks[9], (N_A_LAYERS, 2, GDN_V_HEADS), f32, 1.0, 16.0)),
        'gdn_dt_bias': dt + jnp.log(-jnp.expm1(-dt)),
        'gdn_norm_w': gain(ks[10], (N_A_LAYERS, GDN_DV)),
        'gdn_w_out': dense(ks[11], (N_A_LAYERS, GDN_V_DIM, D_MODEL), GDN_V_DIM),
        'mla_w_a': dense(ks[12], (N_B_LAYERS, D_MODEL, MLA_A_DIM), D_MODEL),
        'mla_q_a_norm': gain(ks[13], (N_B_LAYERS, MLA_Q_RANK)),
        'mla_w_q_b': dense(ks[14], (N_B_LAYERS, MLA_Q_RANK, MLA_HEADS * MLA_D_QK), MLA_Q_RANK),
        'mla_kv_a_norm': gain(ks[15], (N_B_LAYERS, MLA_KV_RANK)),
        'mla_w_kv_b': dense(ks[16], (N_B_LAYERS, MLA_KV_RANK, MLA_HEADS * (MLA_D_NOPE + MLA_D_V)), MLA_KV_RANK),
        'mla_w_o': dense(ks[17], (N_B_LAYERS, MLA_HEADS * MLA_D_V, D_MODEL), MLA_HEADS * MLA_D_V),
        'ffn_w_in': dense(ks[18], (DEPTH, D_MODEL, D_FF), D_MODEL),
        'ffn_w_out': dense(ks[19], (DEPTH, D_FF, D_MODEL), D_FF),
    }


def reference(x_prompt, x_sample, norm_mix_pre, norm_mix_post, norm_ffn_pre, norm_ffn_post,
              gdn_w_in, gdn_conv_w, gdn_a_log, gdn_dt_bias, gdn_norm_w, gdn_w_out,
              mla_w_a, mla_q_a_norm, mla_w_q_b, mla_kv_a_norm, mla_w_kv_b, mla_w_o,
              ffn_w_in, ffn_w_out):
    y_prompt = trunk(x_prompt, norm_mix_pre, norm_mix_post, norm_ffn_pre, norm_ffn_post,
                     gdn_w_in, gdn_conv_w, gdn_a_log, gdn_dt_bias, gdn_norm_w, gdn_w_out,
                     mla_w_a, mla_q_a_norm, mla_w_q_b, mla_kv_a_norm, mla_w_kv_b, mla_w_o,
                     ffn_w_in, ffn_w_out)
    y_sample = trunk(x_sample, norm_mix_pre, norm_mix_post, norm_ffn_pre, norm_ffn_post,
                     gdn_w_in, gdn_conv_w, gdn_a_log, gdn_dt_bias, gdn_norm_w, gdn_w_out,
                     mla_w_a, mla_q_a_norm, mla_w_q_b, mla_kv_a_norm, mla_w_kv_b, mla_w_o,
                     ffn_w_in, ffn_w_out)
    return (y_prompt, y_sample)
```

```python
import functools

import jax
import jax.numpy as jnp
from jax import lax
from jax.experimental import pallas as pl
from jax.experimental.pallas import tpu as pltpu

F32 = jnp.float32
BF16 = jnp.bfloat16

RMS_EPS = 1e-6
L2_EPS = 1e-6
LANES = 128

GDN_QK_HEADS = 16
GDN_V_HEADS = 32
GDN_DK = 128
GDN_CONV = 5
GDN_CHUNK = 64
GDN_Q_DIM = GDN_QK_HEADS * GDN_DK
GDN_V_DIM = GDN_V_HEADS * GDN_DK
GDN_CONV_DIM = 2 * GDN_Q_DIM + GDN_V_DIM
GDN_MAIN_DIM = GDN_CONV_DIM + GDN_V_DIM
GDN_GATE_LANES = 4 * GDN_V_HEADS

MLA_HEADS = 16
MLA_Q_RANK = 768
MLA_KV_RANK = 512
MLA_D_NOPE = 128
MLA_D_ROPE = 64
MLA_D_V = 128
MLA_D_QK = MLA_D_NOPE + MLA_D_ROPE
MLA_D_PAD = 2 * LANES
MLA_A_PAD = MLA_Q_RANK + MLA_KV_RANK + LANES
ROPE_THETA = 10000.0

VMEM_LIMIT = 56 * 1024 * 1024


def _params(sem):
    return pltpu.CompilerParams(dimension_semantics=sem, vmem_limit_bytes=VMEM_LIMIT)


def _resident(shape, index_map):
    return pl.BlockSpec(shape, index_map, pipeline_mode=pl.Buffered(1))


def _rms(x, w):
    return x * lax.rsqrt(jnp.mean(x * x, axis=-1, keepdims=True) + RMS_EPS) * w


def _dot(a, b):
    return jnp.dot(a, b, preferred_element_type=F32)


def _dot_nt(a, b):
    return lax.dot_general(a, b, (((1,), (1,)), ((), ())), preferred_element_type=F32)


def _dot_tn(a, b):
    return lax.dot_general(a, b, (((0,), (0,)), ((), ())), preferred_element_type=F32)


def _split3(x):
    hi = x.astype(BF16)
    r1 = x - hi.astype(F32)
    mid = r1.astype(BF16)
    lo = (r1 - mid.astype(F32)).astype(BF16)
    return hi, mid, lo


def _gdn_proj_kernel(x_ref, nw_ref, w_ref, wg_ref, wgt_ref, o_ref, g_ref, gt_ref, xn_ref):
    @pl.when(pl.program_id(1) == 0)
    def _():
        xn = _rms(x_ref[...], nw_ref[...]).astype(BF16)
        xn_ref[...] = xn
        g_ref[...] = _dot(xn, wg_ref[...])
        gt_ref[...] = _dot_nt(wgt_ref[...], xn)

    o_ref[...] = _dot(xn_ref[...], w_ref[...]).astype(BF16)


def _gdn_proj(x, nw, w, wg, wgt, tm, tn):
    T, D = x.shape
    N = w.shape[1]
    return pl.pallas_call(
        _gdn_proj_kernel,
        grid=(T // tm, N // tn),
        in_specs=[
            pl.BlockSpec((tm, D), lambda i, j: (i, 0)),
            pl.BlockSpec((1, D), lambda i, j: (0, 0)),
            pl.BlockSpec((D, tn), lambda i, j: (0, j)),
            pl.BlockSpec((D, LANES), lambda i, j: (0, 0)),
            pl.BlockSpec((LANES, D), lambda i, j: (0, 0)),
        ],
        out_specs=[
            pl.BlockSpec((tm, tn), lambda i, j: (i, j)),
            pl.BlockSpec((tm, LANES), lambda i, j: (i, 0)),
            pl.BlockSpec((LANES, tm), lambda i, j: (0, i)),
        ],
        out_shape=[
            jax.ShapeDtypeStruct((T, N), BF16),
            jax.ShapeDtypeStruct((T, LANES), F32),
            jax.ShapeDtypeStruct((LANES, T), F32),
        ],
        scratch_shapes=[pltpu.VMEM((tm, D), BF16)],
        compiler_params=_params(("parallel", "arbitrary")),
        name="gdn_proj",
    )(x, nw, w, wg, wgt)


def _softplus(y):
    return jnp.maximum(y, 0.0) + jnp.log1p(jnp.exp(-jnp.abs(y)))


def _gate_values(x, neg_a, dt):
    beta = 1.0 / (1.0 + jnp.exp(-x))
    g = neg_a * _softplus(x + dt)
    return beta, g


def _gdn_gates_kernel(x_ref, xt_ref, na_ref, dt_ref, nat_ref, dtt_ref, o_ref, ot_ref):
    R = x_ref.shape[0]
    ii = lax.broadcasted_iota(jnp.int32, (R, R), 0)
    jj = lax.broadcasted_iota(jnp.int32, (R, R), 1)
    same = (ii // GDN_CHUNK) == (jj // GDN_CHUNK)
    lower = jnp.where(same & (jj <= ii), 1.0, 0.0).astype(BF16)
    upper = jnp.where(same & (jj >= ii), 1.0, 0.0).astype(BF16)

    beta, g = _gate_values(x_ref[...], na_ref[...], dt_ref[...])
    parts = _split3(g)
    cf = sum(_dot(lower, p) for p in parts)
    cb = sum(_dot(upper, p) for p in parts)
    e = lax.broadcasted_iota(jnp.int32, (R, LANES), 1) % 8
    o_ref[...] = jnp.where(e < 4, beta, jnp.where(e < 6, cf, cb))

    beta_t, g_t = _gate_values(xt_ref[...], nat_ref[...], dtt_ref[...])
    parts_t = _split3(g_t)
    cf_t = sum(_dot(p, upper) for p in parts_t)
    cb_t = sum(_dot(p, lower) for p in parts_t)
    e_t = lax.broadcasted_iota(jnp.int32, (LANES, R), 0) % 8
    ot_ref[...] = jnp.where(e_t < 4, beta_t, jnp.where(e_t < 6, cf_t, cb_t))


def _gdn_gates(gates, gates_t, neg_a, dt, tr):
    T = gates.shape[0]
    row = lambda i: (i, 0)
    col = lambda i: (0, i)
    fixed = lambda i: (0, 0)
    return pl.pallas_call(
        _gdn_gates_kernel,
        grid=(T // tr,),
        in_specs=[
            pl.BlockSpec((tr, LANES), row),
            pl.BlockSpec((LANES, tr), col),
            pl.BlockSpec((1, LANES), fixed),
            pl.BlockSpec((1, LANES), fixed),
            pl.BlockSpec((LANES, 1), fixed),
            pl.BlockSpec((LANES, 1), fixed),
        ],
        out_specs=[pl.BlockSpec((tr, LANES), row), pl.BlockSpec((LANES, tr), col)],
        out_shape=[jax.ShapeDtypeStruct((T, LANES), F32), jax.ShapeDtypeStruct((LANES, T), F32)],
        compiler_params=_params(("parallel",)),
        name="gdn_gates",
    )(gates, gates_t, neg_a.reshape(1, LANES), dt.reshape(1, LANES),
      neg_a.reshape(LANES, 1), dt.reshape(LANES, 1))


CONV_HALO = 16
CONV_PAD = 8


def _gdn_conv_kernel(blocks_per_seq, n_q_blocks, n_qk_blocks,
                     prev_ref, main_ref, next_ref, w_ref, o_ref, xs_ref):
    i = pl.program_id(0)
    j = pl.program_id(1)
    tr, tc = main_ref.shape
    r = GDN_CONV // 2
    pos = i % blocks_per_seq
    keep_prev = jnp.where(pos == 0, 0.0, 1.0)
    keep_next = jnp.where(pos == blocks_per_seq - 1, 0.0, 1.0)
    xs_ref[CONV_PAD - r:CONV_PAD, :] = prev_ref[CONV_HALO - r:, :].astype(F32) * keep_prev
    xs_ref[CONV_PAD:CONV_PAD + tr, :] = main_ref[...].astype(F32)
    xs_ref[CONV_PAD + tr:CONV_PAD + tr + r, :] = next_ref[:r, :].astype(F32) * keep_next
    w = w_ref[...]
    acc = xs_ref[CONV_PAD - r:CONV_PAD - r + tr, :] * w[0:1, :]
    for t in range(1, GDN_CONV):
        acc = acc + xs_ref[CONV_PAD - r + t:CONV_PAD - r + t + tr, :] * w[t:t + 1, :]
    y = acc * (1.0 / (1.0 + jnp.exp(-acc)))

    @pl.when(j >= n_qk_blocks)
    def _():
        o_ref[...] = y.astype(BF16)

    @pl.when(j < n_qk_blocks)
    def _():
        q_scale = jnp.where(j < n_q_blocks, GDN_DK ** -0.5, 1.0)
        for h in range(tc // LANES):
            blk = y[:, h * LANES:(h + 1) * LANES]
            inv = lax.rsqrt(jnp.sum(blk * blk, axis=-1, keepdims=True) + L2_EPS) * q_scale
            o_ref[:, h * LANES:(h + 1) * LANES] = (blk * inv).astype(BF16)


def _gdn_conv(pre, conv_w, seq_len, tr, tc):
    T = pre.shape[0]
    C = conv_w.shape[1]
    hb = tr // CONV_HALO
    n_halo = T // CONV_HALO
    kern = functools.partial(_gdn_conv_kernel, seq_len // tr, GDN_Q_DIM // tc, 2 * GDN_Q_DIM // tc)
    return pl.pallas_call(
        kern,
        grid=(T // tr, C // tc),
        in_specs=[
            pl.BlockSpec((CONV_HALO, tc), lambda i, j: (jnp.maximum(i * hb - 1, 0), j)),
            pl.BlockSpec((tr, tc), lambda i, j: (i, j)),
            pl.BlockSpec((CONV_HALO, tc), lambda i, j: (jnp.minimum((i + 1) * hb, n_halo - 1), j)),
            pl.BlockSpec((GDN_CONV, tc), lambda i, j: (0, j)),
        ],
        out_specs=pl.BlockSpec((tr, tc), lambda i, j: (i, j)),
        out_shape=jax.ShapeDtypeStruct((T, C), BF16),
        scratch_shapes=[pltpu.VMEM((tr + 2 * CONV_PAD, tc), F32)],
        compiler_params=_params(("parallel", "parallel")),
        name="gdn_conv",
    )(pre, pre, pre, conv_w)


NEUMANN_DOUBLINGS = 5


def _delta_kernel(rev, chunks, *refs):
    if rev:
        q_ref, k_ref, v_ref, g_ref, gt_ref, of_ref, z_ref, nw_ref, o_ref, s_ref = refs
    else:
        q_ref, k_ref, v_ref, g_ref, gt_ref, o_ref, s_ref = refs
    C = GDN_CHUNK
    d = 1 if rev else 0

    @pl.when(pl.program_id(2) == 0)
    def _():
        s_ref[...] = jnp.zeros_like(s_ref)

    qh = pl.program_id(1)
    gsel = pltpu.roll(g_ref[...], (LANES - 8 * qh) % LANES, axis=1)

    ii = lax.broadcasted_iota(jnp.int32, (C, C), 0)
    jj = lax.broadcasted_iota(jnp.int32, (C, C), 1)
    incl = (ii <= jj) if rev else (ii >= jj)
    strict = (ii < jj) if rev else (ii > jj)
    eye = jnp.where(ii == jj, 1.0, 0.0)

    order = range(chunks - 1, -1, -1) if rev else range(chunks)
    for ci in order:
        r0 = ci * C
        kc = k_ref[r0:r0 + C, :]
        qc = q_ref[r0:r0 + C, :]
        kq = _dot_nt(jnp.concatenate([kc, qc], axis=0), kc)
        kk = kq[:C]
        qk = kq[C:]
        kcf = kc.astype(F32)
        qcf = qc.astype(F32)
        for hv in range(2):
            lb = 2 * d + hv
            lg = 4 + 2 * d + hv
            beta = gsel[r0:r0 + C, lb:lb + 1]
            gcc = gsel[r0:r0 + C, lg:lg + 1]
            gcr = gt_ref[lg:lg + 1, r0:r0 + C]
            decay = jnp.where(incl, jnp.exp(jnp.where(incl, gcc - gcr, 0.0)), 0.0)
            a = jnp.where(strict, beta * kk * decay, 0.0)
            t = eye - a
            x = a
            for _ in range(NEUMANN_DOUBLINGS):
                xb = x.astype(BF16)
                x = _dot(xb, xb)
                t = t + _dot(t.astype(BF16), x.astype(BF16))
            eg = jnp.exp(gcc)
            vb = v_ref[r0:r0 + C, hv * LANES:(hv + 1) * LANES].astype(F32) * beta
            kb = kcf * (beta * eg)
            uw = _dot(t.astype(BF16), jnp.concatenate([vb, kb], axis=1).astype(BF16))
            u = uw[:, :LANES]
            w = uw[:, LANES:]
            gl = gcr[:, 0:1] if rev else gcr[:, C - 1:C]
            qd = qcf * eg
            kd = kcf * jnp.exp(gl - gcc)
            qkm = (qk * decay).astype(BF16)
            s = s_ref[hv]
            wqs = _dot(jnp.concatenate([w, qd], axis=0).astype(BF16), s.astype(BF16))
            vn = (u - wqs[:C]).astype(BF16)
            o = wqs[C:] + _dot(qkm, vn)
            s_ref[hv] = s * jnp.exp(gl) + _dot_tn(kd.astype(BF16), vn)
            cols = slice(hv * LANES, (hv + 1) * LANES)
            if rev:
                tot = of_ref[r0:r0 + C, cols] + o
                zf = z_ref[r0:r0 + C, cols].astype(F32)
                gate = zf * (1.0 / (1.0 + jnp.exp(-zf)))
                o_ref[r0:r0 + C, cols] = (_rms(tot, nw_ref[...]) * gate).astype(BF16)
            else:
                o_ref[r0:r0 + C, cols] = o


def _gdn_delta(rev, qkv, gsum, gsum_t, batch, chunks, extra=None):
    T = qkv.shape[0]
    R = chunks * GDN_CHUNK
    nb = T // batch // R
    rows = (lambda b, h, n: b * nb + (nb - 1 - n)) if rev else (lambda b, h, n: b * nb + n)
    q_blocks = GDN_Q_DIM // LANES
    v_block0 = 2 * GDN_Q_DIM // (2 * LANES)
    in_specs = [
        pl.BlockSpec((R, LANES), lambda b, h, n: (rows(b, h, n), h)),
        pl.BlockSpec((R, LANES), lambda b, h, n: (rows(b, h, n), q_blocks + h)),
        pl.BlockSpec((R, 2 * LANES), lambda b, h, n: (rows(b, h, n), v_block0 + h)),
        pl.BlockSpec((R, LANES), lambda b, h, n: (rows(b, h, n), 0)),
        pl.BlockSpec((8, R), lambda b, h, n: (h, rows(b, h, n))),
    ]
    args = [qkv, qkv, qkv, gsum, gsum_t]
    if rev:
        o_fwd, pre, norm_w = extra
        z_block0 = GDN_CONV_DIM // (2 * LANES)
        in_specs += [
            pl.BlockSpec((R, 2 * LANES), lambda b, h, n: (rows(b, h, n), h)),
            pl.BlockSpec((R, 2 * LANES), lambda b, h, n: (rows(b, h, n), z_block0 + h)),
            pl.BlockSpec((1, LANES), lambda b, h, n: (0, 0)),
        ]
        args += [o_fwd, pre, norm_w]
    return pl.pallas_call(
        functools.partial(_delta_kernel, rev, chunks),
        grid=(batch, GDN_QK_HEADS, nb),
        in_specs=in_specs,
        out_specs=pl.BlockSpec((R, 2 * LANES), lambda b, h, n: (rows(b, h, n), h)),
        out_shape=jax.ShapeDtypeStruct((T, GDN_V_DIM), BF16 if rev else F32),
        scratch_shapes=[pltpu.VMEM((2, GDN_DK, GDN_DK), F32)],
        compiler_params=_params(("parallel", "parallel", "arbitrary")),
        name="gdn_delta_bwd" if rev else "gdn_delta_fwd",
    )(*args)


def _out_proj_kernel(a_ref, w_ref, nw_ref, x_ref, o_ref):
    m = _dot(a_ref[...], w_ref[...])
    o_ref[...] = x_ref[...] + _rms(m, nw_ref[...])


def _out_proj(a, w, nw, x, tm):
    T, K = a.shape
    D = w.shape[1]
    return pl.pallas_call(
        _out_proj_kernel,
        grid=(T // tm,),
        in_specs=[
            pl.BlockSpec((tm, K), lambda i: (i, 0)),
            _resident((K, D), lambda i: (0, 0)),
            pl.BlockSpec((1, D), lambda i: (0, 0)),
            pl.BlockSpec((tm, D), lambda i: (i, 0)),
        ],
        out_specs=pl.BlockSpec((tm, D), lambda i: (i, 0)),
        out_shape=jax.ShapeDtypeStruct((T, D), F32),
        compiler_params=_params(("parallel",)),
        name="out_proj",
    )(a, w, nw, x)


def _ffn_kernel(x_ref, nw1_ref, w1_ref, w2_ref, nw2_ref, o_ref, xn_ref, acc_ref):
    j = pl.program_id(1)

    @pl.when(j == 0)
    def _():
        xn_ref[...] = _rms(x_ref[...], nw1_ref[...]).astype(BF16)
        acc_ref[...] = jnp.zeros_like(acc_ref)

    h = jnp.maximum(_dot(xn_ref[...], w1_ref[...]), 0.0)
    acc_ref[...] += _dot((h * h).astype(BF16), w2_ref[...])

    @pl.when(j == pl.num_programs(1) - 1)
    def _():
        o_ref[...] = x_ref[...] + _rms(acc_ref[...], nw2_ref[...])


def _ffn(x, nw1, w1, w2, nw2, tm, tf):
    T, D = x.shape
    Fd = w1.shape[1]
    return pl.pallas_call(
        _ffn_kernel,
        grid=(T // tm, Fd // tf),
        in_specs=[
            pl.BlockSpec((tm, D), lambda i, j: (i, 0)),
            pl.BlockSpec((1, D), lambda i, j: (0, 0)),
            pl.BlockSpec((D, tf), lambda i, j: (0, j)),
            pl.BlockSpec((tf, D), lambda i, j: (j, 0)),
            pl.BlockSpec((1, D), lambda i, j: (0, 0)),
        ],
        out_specs=pl.BlockSpec((tm, D), lambda i, j: (i, 0)),
        out_shape=jax.ShapeDtypeStruct((T, D), F32),
        scratch_shapes=[pltpu.VMEM((tm, D), BF16), pltpu.VMEM((tm, D), F32)],
        compiler_params=_params(("parallel", "arbitrary")),
        name="ffn",
    )(x, nw1, w1, w2, nw2)


def _rope(x, cos, sin):
    return x * cos + pltpu.roll(x, LANES // 2, axis=1) * sin


def _mla_proj_kernel(x_ref, nw_ref, wa_ref, qn_ref, kvn_ref, wq_ref, wkn_ref, wv_ref,
                     cos_ref, sin_ref, q_ref, k_ref, v_ref):
    xn = _rms(x_ref[...], nw_ref[...]).astype(BF16)
    a = _dot(xn, wa_ref[...])
    cq = _rms(a[:, :MLA_Q_RANK], qn_ref[...]).astype(BF16)
    ckv = _rms(a[:, MLA_Q_RANK:MLA_Q_RANK + MLA_KV_RANK], kvn_ref[...]).astype(BF16)
    cos = cos_ref[...]
    sin = sin_ref[...]
    k_rope = _rope(a[:, MLA_Q_RANK + MLA_KV_RANK:], cos, sin).astype(BF16)
    scale = MLA_D_QK ** -0.5
    q = _dot(cq, wq_ref[...])
    kn = _dot(ckv, wkn_ref[...])
    for h in range(MLA_HEADS):
        c0 = h * MLA_D_PAD
        q_ref[:, c0:c0 + LANES] = (q[:, c0:c0 + LANES] * scale).astype(BF16)
        q_ref[:, c0 + LANES:c0 + 2 * LANES] = (_rope(q[:, c0 + LANES:c0 + 2 * LANES], cos, sin) * scale).astype(BF16)
        k_ref[:, c0:c0 + LANES] = kn[:, h * LANES:(h + 1) * LANES].astype(BF16)
        k_ref[:, c0 + LANES:c0 + 2 * LANES] = k_rope
    v_ref[...] = _dot(ckv, wv_ref[...]).astype(BF16)


def _mla_proj(x, nw, wa, qn, kvn, wq, wkn, wv, cos, sin, seq_len, tm):
    T, D = x.shape
    pos_blocks = seq_len // tm
    row = lambda i: (i, 0)
    fixed = lambda i: (0, 0)
    pos = lambda i: (i % pos_blocks, 0)
    return pl.pallas_call(
        _mla_proj_kernel,
        grid=(T // tm,),
        in_specs=[
            pl.BlockSpec((tm, D), row),
            pl.BlockSpec((1, D), fixed),
            _resident(wa.shape, fixed),
            pl.BlockSpec((1, MLA_Q_RANK), fixed),
            pl.BlockSpec((1, MLA_KV_RANK), fixed),
            _resident(wq.shape, fixed),
            _resident(wkn.shape, fixed),
            _resident(wv.shape, fixed),
            pl.BlockSpec((tm, LANES), pos),
            pl.BlockSpec((tm, LANES), pos),
        ],
        out_specs=[
            pl.BlockSpec((tm, MLA_HEADS * MLA_D_PAD), row),
            pl.BlockSpec((tm, MLA_HEADS * MLA_D_PAD), row),
            pl.BlockSpec((tm, MLA_HEADS * MLA_D_V), row),
        ],
        out_shape=[
            jax.ShapeDtypeStruct((T, MLA_HEADS * MLA_D_PAD), BF16),
            jax.ShapeDtypeStruct((T, MLA_HEADS * MLA_D_PAD), BF16),
            jax.ShapeDtypeStruct((T, MLA_HEADS * MLA_D_V), BF16),
        ],
        compiler_params=_params(("parallel",)),
        name="mla_proj",
    )(x, nw, wa, qn, kvn, wq, wkn, wv, cos, sin)


def _attn_kernel(tk, q_ref, k_ref, v_ref, o_ref):
    tq = q_ref.shape[0]
    L = k_ref.shape[0]
    q = q_ref[...]

    def body(t, carry):
        m, l, acc = carry
        start = pl.multiple_of(t * tk, tk)
        s = _dot_nt(q, k_ref[pl.ds(start, tk), :])
        m_new = jnp.maximum(m, jnp.max(s, axis=-1, keepdims=True))
        p = jnp.exp(s - m_new)
        alpha = jnp.exp(m - m_new)
        l = alpha * l + jnp.sum(p, axis=-1, keepdims=True)
        acc = alpha * acc + _dot(p.astype(BF16), v_ref[pl.ds(start, tk), :])
        return m_new, l, acc

    init = (jnp.full((tq, 1), -jnp.inf, F32), jnp.zeros((tq, 1), F32), jnp.zeros((tq, MLA_D_V), F32))
    _, l, acc = lax.fori_loop(0, L // tk, body, init)
    o_ref[...] = (acc / l).astype(BF16)


def _attention(q, k, v, batch, tq, tk):
    T = q.shape[0]
    L = T // batch
    nq = L // tq
    return pl.pallas_call(
        functools.partial(_attn_kernel, tk),
        grid=(batch, MLA_HEADS, nq),
        in_specs=[
            pl.BlockSpec((tq, MLA_D_PAD), lambda b, h, i: (b * nq + i, h)),
            pl.BlockSpec((L, MLA_D_PAD), lambda b, h, i: (b, h)),
            pl.BlockSpec((L, MLA_D_V), lambda b, h, i: (b, h)),
        ],
        out_specs=pl.BlockSpec((tq, MLA_D_V), lambda b, h, i: (b * nq + i, h)),
        out_shape=jax.ShapeDtypeStruct((T, MLA_HEADS * MLA_D_V), BF16),
        compiler_params=_params(("parallel", "parallel", "arbitrary")),
        name="attention",
    )(q, k, v)


def _gate_lane_perm():
    lanes = jnp.arange(GDN_GATE_LANES)
    q, e = lanes // 8, lanes % 8
    return GDN_V_HEADS * (e // 2) + 2 * q + e % 2


def _gate_lane_params(a_log, dt_bias):
    lanes = jnp.arange(GDN_GATE_LANES)
    q, e = lanes // 8, lanes % 8
    head = 2 * q + e % 2
    direction = jnp.maximum(e // 2 - 2, 0)
    is_decay = e >= 4
    neg_a = jnp.where(is_decay, -jnp.exp(a_log.astype(F32))[direction, head], 0.0)
    dt = jnp.where(is_decay, dt_bias.astype(F32)[direction, head], 0.0)
    return neg_a, dt


def _pad_rope_cols(w):
    half = MLA_D_ROPE // 2
    z = jnp.zeros(w.shape[:-1] + (half,), w.dtype)
    return jnp.concatenate([w[..., :half], z, w[..., half:], z], axis=-1)


def _rope_tables(L):
    half = MLA_D_ROPE // 2
    inv_freq = ROPE_THETA ** (-jnp.arange(half, dtype=F32) / half)
    ang = jnp.arange(L, dtype=F32)[:, None] * inv_freq[None, :]
    c, s = jnp.cos(ang), jnp.sin(ang)
    z = jnp.zeros_like(c)
    return jnp.concatenate([c, z, c, z], axis=-1), jnp.concatenate([-s, z, s, z], axis=-1)


def _prepare(p):
    w = {}
    g_in = p['gdn_w_in'][0]
    w['gdn_main'] = g_in[:, :GDN_MAIN_DIM].astype(BF16)
    wg = g_in[:, GDN_MAIN_DIM:][:, _gate_lane_perm()].astype(BF16)
    w['gdn_gate'] = wg
    w['gdn_gate_t'] = wg.T
    w['gdn_neg_a'], w['gdn_dt'] = _gate_lane_params(p['gdn_a_log'][0], p['gdn_dt_bias'][0])
    w['gdn_conv'] = p['gdn_conv_w'][0].astype(F32)
    w['gdn_norm'] = p['gdn_norm_w'][0].reshape(1, GDN_DK).astype(F32)
    w['gdn_out'] = p['gdn_w_out'][0].astype(BF16)

    wa = p['mla_w_a'][0]
    rank = MLA_Q_RANK + MLA_KV_RANK
    w['mla_a'] = jnp.concatenate([wa[:, :rank], _pad_rope_cols(wa[:, rank:])], axis=-1).astype(BF16)
    wq = p['mla_w_q_b'][0].reshape(MLA_Q_RANK, MLA_HEADS, MLA_D_QK)
    wq = jnp.concatenate([wq[..., :MLA_D_NOPE], _pad_rope_cols(wq[..., MLA_D_NOPE:])], axis=-1)
    w['mla_q'] = wq.reshape(MLA_Q_RANK, MLA_HEADS * MLA_D_PAD).astype(BF16)
    wkv = p['mla_w_kv_b'][0].reshape(MLA_KV_RANK, MLA_HEADS, MLA_D_NOPE + MLA_D_V)
    w['mla_kn'] = wkv[..., :MLA_D_NOPE].reshape(MLA_KV_RANK, MLA_HEADS * MLA_D_NOPE).astype(BF16)
    w['mla_v'] = wkv[..., MLA_D_NOPE:].reshape(MLA_KV_RANK, MLA_HEADS * MLA_D_V).astype(BF16)
    w['mla_qn'] = p['mla_q_a_norm'][0].reshape(1, MLA_Q_RANK).astype(F32)
    w['mla_kvn'] = p['mla_kv_a_norm'][0].reshape(1, MLA_KV_RANK).astype(F32)
    w['mla_o'] = p['mla_w_o'][0].astype(BF16)

    w['ffn_in'] = p['ffn_w_in'].astype(BF16)
    w['ffn_out'] = p['ffn_w_out'].astype(BF16)
    for name in ('norm_mix_pre', 'norm_mix_post', 'norm_ffn_pre', 'norm_ffn_post'):
        w[name] = p[name].astype(F32)[:, None, :]
    return w


TILES = dict(
    proj_tm=512, proj_tn=1024,
    gate_tr=512,
    conv_tr=512, conv_tc=1024,
    delta_chunks=4,
    out_tm=256,
    ffn_tm=512, ffn_tf=1024,
    mla_tm=256,
    attn_tq=256, attn_tk=512,
)


def _trunk(x3, w, tiles):
    B, L, D = x3.shape
    x = x3.reshape(B * L, D)

    pre, gates, gates_t = _gdn_proj(x, w['norm_mix_pre'][0], w['gdn_main'], w['gdn_gate'], w['gdn_gate_t'],
                                    tiles['proj_tm'], tiles['proj_tn'])
    gsum, gsum_t = _gdn_gates(gates, gates_t, w['gdn_neg_a'], w['gdn_dt'], tiles['gate_tr'])
    qkv = _gdn_conv(pre, w['gdn_conv'], L, tiles['conv_tr'], tiles['conv_tc'])
    o_fwd = _gdn_delta(False, qkv, gsum, gsum_t, B, tiles['delta_chunks'])
    o = _gdn_delta(True, qkv, gsum, gsum_t, B, tiles['delta_chunks'], (o_fwd, pre, w['gdn_norm']))
    x = _out_proj(o, w['gdn_out'], w['norm_mix_post'][0], x, tiles['out_tm'])
    x = _ffn(x, w['norm_ffn_pre'][0], w['ffn_in'][0], w['ffn_out'][0], w['norm_ffn_post'][0],
             tiles['ffn_tm'], tiles['ffn_tf'])

    cos, sin = _rope_tables(L)
    q, k, v = _mla_proj(x, w['norm_mix_pre'][1], w['mla_a'], w['mla_qn'], w['mla_kvn'], w['mla_q'],
                        w['mla_kn'], w['mla_v'], cos, sin, L, tiles['mla_tm'])
    o = _attention(q, k, v, B, tiles['attn_tq'], tiles['attn_tk'])
    x = _out_proj(o, w['mla_o'], w['norm_mix_post'][1], x, tiles['out_tm'])
    x = _ffn(x, w['norm_ffn_pre'][1], w['ffn_in'][1], w['ffn_out'][1], w['norm_ffn_post'][1],
             tiles['ffn_tm'], tiles['ffn_tf'])
    return x.reshape(B, L, D)


def kernel(x_prompt, x_sample, norm_mix_pre, norm_mix_post, norm_ffn_pre, norm_ffn_post, gdn_w_in, gdn_conv_w, gdn_a_log, gdn_dt_bias, gdn_norm_w, gdn_w_out, mla_w_a, mla_q_a_norm, mla_w_q_b, mla_kv_a_norm, mla_w_kv_b, mla_w_o, ffn_w_in, ffn_w_out):
    w = _prepare(dict(
        norm_mix_pre=norm_mix_pre, norm_mix_post=norm_mix_post, norm_ffn_pre=norm_ffn_pre,
        norm_ffn_post=norm_ffn_post, gdn_w_in=gdn_w_in, gdn_conv_w=gdn_conv_w, gdn_a_log=gdn_a_log,
        gdn_dt_bias=gdn_dt_bias, gdn_norm_w=gdn_norm_w, gdn_w_out=gdn_w_out, mla_w_a=mla_w_a,
        mla_q_a_norm=mla_q_a_norm, mla_w_q_b=mla_w_q_b, mla_kv_a_norm=mla_kv_a_norm,
        mla_w_kv_b=mla_w_kv_b, mla_w_o=mla_w_o, ffn_w_in=ffn_w_in, ffn_w_out=ffn_w_out))
    return _trunk(x_prompt, w, TILES), _trunk(x_sample, w, TILES)
```

```python
import functools
import math

import jax
import jax.numpy as jnp
from jax import lax
from jax.experimental import pallas as pl
from jax.experimental.pallas import tpu as pltpu

F32 = jnp.float32
BF16 = jnp.bfloat16

RMS_EPS = 1e-6
L2_EPS = 1e-6
LANES = 128
BF16_ROWS = 16

GDN_QK_HEADS = 16
GDN_V_HEADS = 32
GDN_DK = 128
GDN_CONV = 5
GDN_CHUNK = 64
GDN_Q_DIM = GDN_QK_HEADS * GDN_DK
GDN_V_DIM = GDN_V_HEADS * GDN_DK
GDN_CONV_DIM = 2 * GDN_Q_DIM + GDN_V_DIM
GDN_MAIN_DIM = GDN_CONV_DIM + GDN_V_DIM
GDN_GATE_LANES = 4 * GDN_V_HEADS

MLA_HEADS = 16
MLA_Q_RANK = 768
MLA_KV_RANK = 512
MLA_D_NOPE = 128
MLA_D_ROPE = 64
MLA_D_V = 128
MLA_D_QK = MLA_D_NOPE + MLA_D_ROPE
MLA_D_PAD = 2 * LANES
MLA_V_ROWS = MLA_D_V + BF16_ROWS
ROPE_THETA = 10000.0

VMEM_LIMIT = 56 * 1024 * 1024


def _params(sem):
    return pltpu.CompilerParams(dimension_semantics=sem, vmem_limit_bytes=VMEM_LIMIT)


def _resident(shape, index_map):
    return pl.BlockSpec(shape, index_map, pipeline_mode=pl.Buffered(1))


def _rms(x, w):
    return x * lax.rsqrt(jnp.mean(x * x, axis=-1, keepdims=True) + RMS_EPS) * w


def _dot(a, b):
    return jnp.dot(a, b, preferred_element_type=F32)


def _dot_nt(a, b):
    return lax.dot_general(a, b, (((1,), (1,)), ((), ())), preferred_element_type=F32)


def _dot_tn(a, b):
    return lax.dot_general(a, b, (((0,), (0,)), ((), ())), preferred_element_type=F32)


def _split3(x):
    hi = x.astype(BF16)
    r1 = x - hi.astype(F32)
    mid = r1.astype(BF16)
    lo = (r1 - mid.astype(F32)).astype(BF16)
    return hi, mid, lo


def _silu(x):
    return x * (1.0 / (1.0 + jnp.exp(-x)))


def _gdn_proj_kernel(x_ref, nw_ref, w_ref, wg_ref, wgt_ref, o_ref, g_ref, gt_ref, xn_ref):
    @pl.when(pl.program_id(1) == 0)
    def _():
        xn = _rms(x_ref[...], nw_ref[...]).astype(BF16)
        xn_ref[...] = xn
        g_ref[...] = _dot(xn, wg_ref[...])
        gt_ref[...] = _dot_nt(wgt_ref[...], xn)

    o_ref[...] = _dot(xn_ref[...], w_ref[...]).astype(BF16)


def _gdn_proj(x, nw, w, wg, wgt, tm, tn):
    T, D = x.shape
    N = w.shape[1]
    return pl.pallas_call(
        _gdn_proj_kernel,
        grid=(T // tm, N // tn),
        in_specs=[
            pl.BlockSpec((tm, D), lambda i, j: (i, 0)),
            pl.BlockSpec((1, D), lambda i, j: (0, 0)),
            pl.BlockSpec((D, tn), lambda i, j: (0, j)),
            pl.BlockSpec((D, LANES), lambda i, j: (0, 0)),
            pl.BlockSpec((LANES, D), lambda i, j: (0, 0)),
        ],
        out_specs=[
            pl.BlockSpec((tm, tn), lambda i, j: (i, j)),
            pl.BlockSpec((tm, LANES), lambda i, j: (i, 0)),
            pl.BlockSpec((LANES, tm), lambda i, j: (0, i)),
        ],
        out_shape=[
            jax.ShapeDtypeStruct((T, N), BF16),
            jax.ShapeDtypeStruct((T, LANES), F32),
            jax.ShapeDtypeStruct((LANES, T), F32),
        ],
        scratch_shapes=[pltpu.VMEM((tm, D), BF16)],
        compiler_params=_params(("parallel", "arbitrary")),
        name="gdn_proj",
    )(x, nw, w, wg, wgt)


def _softplus(y):
    return jnp.maximum(y, 0.0) + jnp.log1p(jnp.exp(-jnp.abs(y)))


def _gate_values(x, neg_a, dt):
    beta = 1.0 / (1.0 + jnp.exp(-x))
    g = neg_a * _softplus(x + dt)
    return beta, g


def _gdn_gates_kernel(x_ref, xt_ref, na_ref, dt_ref, nat_ref, dtt_ref, o_ref, ot_ref):
    R = x_ref.shape[0]
    ii = lax.broadcasted_iota(jnp.int32, (R, R), 0)
    jj = lax.broadcasted_iota(jnp.int32, (R, R), 1)
    same = (ii // GDN_CHUNK) == (jj // GDN_CHUNK)
    lower = jnp.where(same & (jj <= ii), 1.0, 0.0).astype(BF16)
    upper = jnp.where(same & (jj >= ii), 1.0, 0.0).astype(BF16)

    beta, g = _gate_values(x_ref[...], na_ref[...], dt_ref[...])
    parts = _split3(g)
    cf = sum(_dot(lower, p) for p in parts)
    cb = sum(_dot(upper, p) for p in parts)
    e = lax.broadcasted_iota(jnp.int32, (R, LANES), 1) % 8
    o_ref[...] = jnp.where(e < 4, beta, jnp.where(e < 6, cf, cb))

    beta_t, g_t = _gate_values(xt_ref[...], nat_ref[...], dtt_ref[...])
    parts_t = _split3(g_t)
    cf_t = sum(_dot(p, upper) for p in parts_t)
    cb_t = sum(_dot(p, lower) for p in parts_t)
    e_t = lax.broadcasted_iota(jnp.int32, (LANES, R), 0) % 8
    ot_ref[...] = jnp.where(e_t < 4, beta_t, jnp.where(e_t < 6, cf_t, cb_t))


def _gdn_gates(gates, gates_t, neg_a, dt, tr):
    T = gates.shape[0]
    row = lambda i: (i, 0)
    col = lambda i: (0, i)
    fixed = lambda i: (0, 0)
    return pl.pallas_call(
        _gdn_gates_kernel,
        grid=(T // tr,),
        in_specs=[
            pl.BlockSpec((tr, LANES), row),
            pl.BlockSpec((LANES, tr), col),
            pl.BlockSpec((1, LANES), fixed),
            pl.BlockSpec((1, LANES), fixed),
            pl.BlockSpec((LANES, 1), fixed),
            pl.BlockSpec((LANES, 1), fixed),
        ],
        out_specs=[pl.BlockSpec((tr, LANES), row), pl.BlockSpec((LANES, tr), col)],
        out_shape=[jax.ShapeDtypeStruct((T, LANES), F32), jax.ShapeDtypeStruct((LANES, T), F32)],
        compiler_params=_params(("parallel",)),
        name="gdn_gates",
    )(gates, gates_t, neg_a.reshape(1, LANES), dt.reshape(1, LANES),
      neg_a.reshape(LANES, 1), dt.reshape(LANES, 1))


CONV_HALO = BF16_ROWS
CONV_PAD = 8


def _gdn_conv_kernel(blocks_per_seq, n_q_blocks, n_qk_blocks,
                     prev_ref, main_ref, next_ref, w_ref, o_ref, xs_ref):
    i = pl.program_id(0)
    j = pl.program_id(1)
    tr, tc = main_ref.shape
    r = GDN_CONV // 2
    pos = i % blocks_per_seq
    keep_prev = jnp.where(pos == 0, 0.0, 1.0)
    keep_next = jnp.where(pos == blocks_per_seq - 1, 0.0, 1.0)
    xs_ref[CONV_PAD - r:CONV_PAD, :] = prev_ref[CONV_HALO - r:, :].astype(F32) * keep_prev
    xs_ref[CONV_PAD:CONV_PAD + tr, :] = main_ref[...].astype(F32)
    xs_ref[CONV_PAD + tr:CONV_PAD + tr + r, :] = next_ref[:r, :].astype(F32) * keep_next
    w = w_ref[...]
    acc = xs_ref[CONV_PAD - r:CONV_PAD - r + tr, :] * w[0:1, :]
    for t in range(1, GDN_CONV):
        acc = acc + xs_ref[CONV_PAD - r + t:CONV_PAD - r + t + tr, :] * w[t:t + 1, :]
    y = _silu(acc)

    @pl.when(j >= n_qk_blocks)
    def _():
        o_ref[...] = y.astype(BF16)

    @pl.when(j < n_qk_blocks)
    def _():
        q_scale = jnp.where(j < n_q_blocks, GDN_DK ** -0.5, 1.0)
        for h in range(tc // LANES):
            blk = y[:, h * LANES:(h + 1) * LANES]
            inv = lax.rsqrt(jnp.sum(blk * blk, axis=-1, keepdims=True) + L2_EPS) * q_scale
            o_ref[:, h * LANES:(h + 1) * LANES] = (blk * inv).astype(BF16)


def _gdn_conv(pre, conv_w, seq_len, tr, tc):
    T = pre.shape[0]
    C = conv_w.shape[1]
    hb = tr // CONV_HALO
    n_halo = T // CONV_HALO
    kern = functools.partial(_gdn_conv_kernel, seq_len // tr, GDN_Q_DIM // tc, 2 * GDN_Q_DIM // tc)
    return pl.pallas_call(
        kern,
        grid=(T // tr, C // tc),
        in_specs=[
            pl.BlockSpec((CONV_HALO, tc), lambda i, j: (jnp.maximum(i * hb - 1, 0), j)),
            pl.BlockSpec((tr, tc), lambda i, j: (i, j)),
            pl.BlockSpec((CONV_HALO, tc), lambda i, j: (jnp.minimum((i + 1) * hb, n_halo - 1), j)),
            pl.BlockSpec((GDN_CONV, tc), lambda i, j: (0, j)),
        ],
        out_specs=pl.BlockSpec((tr, tc), lambda i, j: (i, j)),
        out_shape=jax.ShapeDtypeStruct((T, C), BF16),
        scratch_shapes=[pltpu.VMEM((tr + 2 * CONV_PAD, tc), F32)],
        compiler_params=_params(("parallel", "parallel")),
        name="gdn_conv",
    )(pre, pre, pre, conv_w)


NEUMANN_DOUBLINGS = 5


def _gate_lane(direction, v_head, decay):
    return (4 if decay else 0) + 2 * direction + v_head


def _delta_solve_kernel(chunks, q_ref, k_ref, v_ref, g_ref, gt_ref,
                        uf_ref, wf_ref, pf_ref, ub_ref, wb_ref, pb_ref):
    C = GDN_CHUNK
    qh = pl.program_id(1)
    gsel = pltpu.roll(g_ref[...], (LANES - 8 * qh) % LANES, axis=1)
    ii = lax.broadcasted_iota(jnp.int32, (C, C), 0)
    jj = lax.broadcasted_iota(jnp.int32, (C, C), 1)
    eye = jnp.where(ii == jj, 1.0, 0.0)
    masks = ((ii >= jj, ii > jj), (ii <= jj, ii < jj))
    outs = ((uf_ref, wf_ref, pf_ref), (ub_ref, wb_ref, pb_ref))

    kqs = []
    for ci in range(chunks):
        rows = slice(ci * C, (ci + 1) * C)
        kc = k_ref[rows, :]
        kqs.append(_dot_nt(jnp.concatenate([kc, q_ref[rows, :]], axis=0), kc))

    chains = [(ci, d, hv) for ci in range(chunks) for d in range(2) for hv in range(2)]
    ts, xs, rhs = [], [], []
    for ci, d, hv in chains:
        rows = slice(ci * C, (ci + 1) * C)
        incl, strict = masks[d]
        lb = _gate_lane(d, hv, False)
        lg = _gate_lane(d, hv, True)
        beta = gsel[rows, lb:lb + 1]
        gcc = gsel[rows, lg:lg + 1]
        gcr = gt_ref[lg:lg + 1, rows]
        decay = jnp.where(incl, jnp.exp(jnp.where(incl, gcc - gcr, 0.0)), 0.0)
        a = jnp.where(strict, beta * kqs[ci][:C] * decay, 0.0)
        ts.append(eye - a)
        xs.append(a)
        outs[d][2][rows, hv * C:(hv + 1) * C] = (kqs[ci][C:] * decay).astype(BF16)
        vb = v_ref[rows, hv * LANES:(hv + 1) * LANES].astype(F32) * beta
        kb = k_ref[rows, :].astype(F32) * (beta * jnp.exp(gcc))
        rhs.append(jnp.concatenate([vb, kb], axis=1).astype(BF16))

    for _ in range(NEUMANN_DOUBLINGS):
        xs = [_dot(x.astype(BF16), x.astype(BF16)) for x in xs]
        ts = [t + _dot(t.astype(BF16), x.astype(BF16)) for t, x in zip(ts, xs)]

    uws = [_dot(t.astype(BF16), r) for t, r in zip(ts, rhs)]
    for (ci, d, hv), uw in zip(chains, uws):
        rows = slice(ci * C, (ci + 1) * C)
        cols = slice(hv * LANES, (hv + 1) * LANES)
        outs[d][0][rows, cols] = uw[:, :LANES].astype(BF16)
        outs[d][1][rows, cols] = uw[:, LANES:].astype(BF16)


def _gdn_delta_solve(qkv, gsum, gsum_t, chunks):
    T = qkv.shape[0]
    R = chunks * GDN_CHUNK
    k_block0 = GDN_Q_DIM // LANES
    v_block0 = 2 * GDN_Q_DIM // (2 * LANES)
    wide = pl.BlockSpec((R, 2 * LANES), lambda i, h: (i, h))
    narrow = pl.BlockSpec((R, LANES), lambda i, h: (i, h))
    uw_shape = jax.ShapeDtypeStruct((T, GDN_V_DIM), BF16)
    p_shape = jax.ShapeDtypeStruct((T, GDN_V_HEADS * GDN_CHUNK), BF16)
    return pl.pallas_call(
        functools.partial(_delta_solve_kernel, chunks),
        grid=(T // R, GDN_QK_HEADS),
        in_specs=[
            narrow,
            pl.BlockSpec((R, LANES), lambda i, h: (i, k_block0 + h)),
            pl.BlockSpec((R, 2 * LANES), lambda i, h: (i, v_block0 + h)),
            pl.BlockSpec((R, LANES), lambda i, h: (i, 0)),
            pl.BlockSpec((8, R), lambda i, h: (h, i)),
        ],
        out_specs=[wide, wide, narrow, wide, wide, narrow],
        out_shape=[uw_shape, uw_shape, p_shape, uw_shape, uw_shape, p_shape],
        compiler_params=_params(("parallel", "parallel")),
        name="gdn_delta_solve",
    )(qkv, qkv, qkv, gsum, gsum_t)


def _delta_sweep_kernel(rev, chunks, heads, *refs):
    if rev:
        q_ref, k_ref, u_ref, w_ref, p_ref, g_ref, gt_ref, of_ref, z_ref, nw_ref, o_ref, s_ref = refs
    else:
        q_ref, k_ref, u_ref, w_ref, p_ref, g_ref, gt_ref, o_ref, s_ref = refs
    C = GDN_CHUNK
    d = 1 if rev else 0

    @pl.when(pl.program_id(2) == 0)
    def _():
        s_ref[...] = jnp.zeros_like(s_ref)

    gsel = pltpu.roll(g_ref[...], (LANES - 8 * heads * pl.program_id(1)) % LANES, axis=1)

    states = range(2 * heads)
    order = range(chunks - 1, -1, -1) if rev else range(chunks)
    for ci in order:
        rows = slice(ci * C, (ci + 1) * C)
        cols = [slice(h * LANES, (h + 1) * LANES) for h in states]
        qk_cols = [slice((h // 2) * LANES, (h // 2 + 1) * LANES) for h in states]
        s_old = [s_ref[h] for h in states]
        wqs = [_dot(jnp.concatenate([w_ref[rows, cols[h]], q_ref[rows, qk_cols[h]]], axis=0),
                    s_old[h].astype(BF16)) for h in states]
        gccs, gls, vns = [], [], []
        for h in states:
            lg = 8 * (h // 2) + _gate_lane(d, h % 2, True)
            gcr = gt_ref[lg:lg + 1, rows]
            gccs.append(gsel[rows, lg:lg + 1])
            gls.append(gcr[:, 0:1] if rev else gcr[:, C - 1:C])
            vns.append(u_ref[rows, cols[h]].astype(F32) - wqs[h][:C])
        pvs = [_dot(p_ref[rows, h * C:(h + 1) * C], vns[h].astype(BF16)) for h in states]
        kvs = [_dot_tn(k_ref[rows, qk_cols[h]], (vns[h] * jnp.exp(gls[h] - gccs[h])).astype(BF16))
               for h in states]
        for h in states:
            s_ref[h] = s_old[h] * jnp.exp(gls[h]) + kvs[h]
            o = jnp.exp(gccs[h]) * wqs[h][C:] + pvs[h]
            if rev:
                tot = of_ref[rows, cols[h]] + o
                gate = _silu(z_ref[rows, cols[h]].astype(F32))
                o_ref[rows, cols[h]] = (_rms(tot, nw_ref[...]) * gate).astype(BF16)
            else:
                o_ref[rows, cols[h]] = o


def _gdn_delta_sweep(rev, qkv, u, w, p, gsum, gsum_t, batch, chunks, heads, extra=None):
    T = qkv.shape[0]
    R = chunks * GDN_CHUNK
    nb = T // batch // R
    rows = (lambda b, n: b * nb + (nb - 1 - n)) if rev else (lambda b, n: b * nb + n)
    qk_w = heads * LANES
    v_w = 2 * heads * LANES
    k_block0 = GDN_Q_DIM // qk_w
    wide = pl.BlockSpec((R, v_w), lambda b, h, n: (rows(b, n), h))
    narrow = pl.BlockSpec((R, qk_w), lambda b, h, n: (rows(b, n), h))
    in_specs = [
        narrow,
        pl.BlockSpec((R, qk_w), lambda b, h, n: (rows(b, n), k_block0 + h)),
        wide, wide, narrow,
        pl.BlockSpec((R, LANES), lambda b, h, n: (rows(b, n), 0)),
        pl.BlockSpec((8 * heads, R), lambda b, h, n: (h, rows(b, n))),
    ]
    args = [qkv, qkv, u, w, p, gsum, gsum_t]
    if rev:
        o_fwd, pre, norm_w = extra
        z_block0 = GDN_CONV_DIM // v_w
        in_specs += [
            wide,
            pl.BlockSpec((R, v_w), lambda b, h, n: (rows(b, n), z_block0 + h)),
            pl.BlockSpec((1, LANES), lambda b, h, n: (0, 0)),
        ]
        args += [o_fwd, pre, norm_w]
    return pl.pallas_call(
        functools.partial(_delta_sweep_kernel, rev, chunks, heads),
        grid=(batch, GDN_QK_HEADS // heads, nb),
        in_specs=in_specs,
        out_specs=wide,
        out_shape=jax.ShapeDtypeStruct((T, GDN_V_DIM), BF16 if rev else F32),
        scratch_shapes=[pltpu.VMEM((2 * heads, GDN_DK, GDN_DK), F32)],
        compiler_params=_params(("parallel", "parallel", "arbitrary")),
        name="gdn_sweep_bwd" if rev else "gdn_sweep_fwd",
    )(*args)


def _out_proj_kernel(transposed, a_ref, w_ref, nw_ref, x_ref, o_ref):
    m = _dot_tn(a_ref[...], w_ref[...]) if transposed else _dot(a_ref[...], w_ref[...])
    o_ref[...] = x_ref[...] + _rms(m, nw_ref[...])


def _out_proj(a, w, nw, x, tm, transposed=False):
    K, D = w.shape
    T = x.shape[0]
    a_spec = pl.BlockSpec((K, tm), lambda i: (0, i)) if transposed else pl.BlockSpec((tm, K), lambda i: (i, 0))
    return pl.pallas_call(
        functools.partial(_out_proj_kernel, transposed),
        grid=(T // tm,),
        in_specs=[
            a_spec,
            _resident((K, D), lambda i: (0, 0)),
            pl.BlockSpec((1, D), lambda i: (0, 0)),
            pl.BlockSpec((tm, D), lambda i: (i, 0)),
        ],
        out_specs=pl.BlockSpec((tm, D), lambda i: (i, 0)),
        out_shape=jax.ShapeDtypeStruct((T, D), F32),
        compiler_params=_params(("parallel",)),
        name="out_proj_t" if transposed else "out_proj",
    )(a, w, nw, x)


def _ffn_kernel(x_ref, nw1_ref, w1_ref, w2_ref, nw2_ref, o_ref, xn_ref, acc_ref):
    j = pl.program_id(1)

    @pl.when(j == 0)
    def _():
        xn_ref[...] = _rms(x_ref[...], nw1_ref[...]).astype(BF16)
        acc_ref[...] = jnp.zeros_like(acc_ref)

    h = jnp.maximum(_dot(xn_ref[...], w1_ref[...]), 0.0)
    acc_ref[...] += _dot((h * h).astype(BF16), w2_ref[...])

    @pl.when(j == pl.num_programs(1) - 1)
    def _():
        o_ref[...] = x_ref[...] + _rms(acc_ref[...], nw2_ref[...])


def _ffn(x, nw1, w1, w2, nw2, tm, tf):
    T, D = x.shape
    Fd = w1.shape[1]
    return pl.pallas_call(
        _ffn_kernel,
        grid=(T // tm, Fd // tf),
        in_specs=[
            pl.BlockSpec((tm, D), lambda i, j: (i, 0)),
            pl.BlockSpec((1, D), lambda i, j: (0, 0)),
            pl.BlockSpec((D, tf), lambda i, j: (0, j)),
            pl.BlockSpec((tf, D), lambda i, j: (j, 0)),
            pl.BlockSpec((1, D), lambda i, j: (0, 0)),
        ],
        out_specs=pl.BlockSpec((tm, D), lambda i, j: (i, 0)),
        out_shape=jax.ShapeDtypeStruct((T, D), F32),
        scratch_shapes=[pltpu.VMEM((tm, D), BF16), pltpu.VMEM((tm, D), F32)],
        compiler_params=_params(("parallel", "arbitrary")),
        name="ffn",
    )(x, nw1, w1, w2, nw2)


def _mla_proj_kernel(x_ref, nw_ref, wa_ref, qn_ref, kvn_ref, wqt_ref, wkn_ref, wvt_ref,
                     cos_ref, sin_ref, cost_ref, sint_ref, qt_ref, k_ref, vt_ref):
    xn = _rms(x_ref[...], nw_ref[...]).astype(BF16)
    a = _dot(xn, wa_ref[...])
    cq = _rms(a[:, :MLA_Q_RANK], qn_ref[...]).astype(BF16)
    ckv = _rms(a[:, MLA_Q_RANK:MLA_Q_RANK + MLA_KV_RANK], kvn_ref[...]).astype(BF16)

    kr = a[:, MLA_Q_RANK + MLA_KV_RANK:]
    k_rope = (kr * cos_ref[...] + pltpu.roll(kr, LANES // 2, axis=1) * sin_ref[...]).astype(BF16)
    kn = _dot(ckv, wkn_ref[...])
    for h in range(MLA_HEADS):
        c0 = h * MLA_D_PAD
        k_ref[:, c0:c0 + LANES] = kn[:, h * LANES:(h + 1) * LANES].astype(BF16)
        k_ref[:, c0 + LANES:c0 + 2 * LANES] = k_rope

    scale = MLA_D_QK ** -0.5 * math.log2(math.e)
    qt = _dot_nt(wqt_ref[...], cq)
    cost = cost_ref[...]
    sint = sint_ref[...]
    half = LANES // 2
    for h in range(MLA_HEADS):
        r0 = h * MLA_D_PAD
        qt_ref[r0:r0 + LANES, :] = (qt[r0:r0 + LANES] * scale).astype(BF16)
        blk = qt[r0 + LANES:r0 + 2 * LANES]
        swapped = jnp.concatenate([blk[half:], blk[:half]], axis=0)
        qt_ref[r0 + LANES:r0 + 2 * LANES, :] = ((blk * cost + swapped * sint) * scale).astype(BF16)

    vt = _dot_nt(wvt_ref[...], ckv)
    ones = jnp.ones((MLA_V_ROWS - MLA_D_V, vt.shape[1]), BF16)
    for h in range(MLA_HEADS):
        r0 = h * MLA_V_ROWS
        vt_ref[r0:r0 + MLA_D_V, :] = vt[h * MLA_D_V:(h + 1) * MLA_D_V].astype(BF16)
        vt_ref[r0 + MLA_D_V:r0 + MLA_V_ROWS, :] = ones


def _mla_proj(x, nw, wa, qn, kvn, wqt, wkn, wvt, cos, sin, seq_len, tm):
    T, D = x.shape
    pos_blocks = seq_len // tm
    row = lambda i: (i, 0)
    col = lambda i: (0, i)
    fixed = lambda i: (0, 0)
    return pl.pallas_call(
        _mla_proj_kernel,
        grid=(T // tm,),
        in_specs=[
            pl.BlockSpec((tm, D), row),
            pl.BlockSpec((1, D), fixed),
            _resident(wa.shape, fixed),
            pl.BlockSpec((1, MLA_Q_RANK), fixed),
            pl.BlockSpec((1, MLA_KV_RANK), fixed),
            _resident(wqt.shape, fixed),
            _resident(wkn.shape, fixed),
            _resident(wvt.shape, fixed),
            pl.BlockSpec((tm, LANES), lambda i: (i % pos_blocks, 0)),
            pl.BlockSpec((tm, LANES), lambda i: (i % pos_blocks, 0)),
            pl.BlockSpec((LANES, tm), lambda i: (0, i % pos_blocks)),
            pl.BlockSpec((LANES, tm), lambda i: (0, i % pos_blocks)),
        ],
        out_specs=[
            pl.BlockSpec((MLA_HEADS * MLA_D_PAD, tm), col),
            pl.BlockSpec((tm, MLA_HEADS * MLA_D_PAD), row),
            pl.BlockSpec((MLA_HEADS * MLA_V_ROWS, tm), col),
        ],
        out_shape=[
            jax.ShapeDtypeStruct((MLA_HEADS * MLA_D_PAD, T), BF16),
            jax.ShapeDtypeStruct((T, MLA_HEADS * MLA_D_PAD), BF16),
            jax.ShapeDtypeStruct((MLA_HEADS * MLA_V_ROWS, T), BF16),
        ],
        compiler_params=_params(("parallel",)),
        name="mla_proj",
    )(x, nw, wa, qn, kvn, wqt, wkn, wvt, cos, sin, cos.T, sin.T)


def _attn_kernel(tk, unroll, qt_ref, k_ref, vt_ref, ot_ref):
    tq = qt_ref.shape[1]
    n = k_ref.shape[0] // tk
    qt = qt_ref[...]

    def chunk(t):
        return pl.ds(t * tk if isinstance(t, int) else pl.multiple_of(t * tk, tk), tk)

    def scores(t):
        return _dot(k_ref[chunk(t), :], qt)

    def weighted(t, p):
        return _dot(vt_ref[:, chunk(t)], p)

    def softmax(s, m):
        m_new = jnp.maximum(m, jnp.max(s, axis=0, keepdims=True))
        return m_new, jnp.exp2(s - m_new).astype(BF16), jnp.exp2(m - m_new)

    def body(t, carry):
        s, m, p_prev, a_prev, acc = carry
        acc = a_prev * acc + weighted(t - 1, p_prev)
        s_next = scores(t + 1)
        m, p, a = softmax(s, m)
        return s_next, m, p, a, acc

    m0 = jnp.full((1, tq), -jnp.inf, F32)
    m, p, a = softmax(scores(0), m0)
    carry = (scores(1), m, p, a, jnp.zeros((MLA_V_ROWS, tq), F32))
    s, m, p_prev, a_prev, acc = lax.fori_loop(1, n - 1, body, carry, unroll=unroll)
    acc = a_prev * acc + weighted(n - 2, p_prev)
    m, p, a = softmax(s, m)
    acc = a * acc + weighted(n - 1, p)
    ot_ref[...] = (acc[:MLA_D_V] / acc[MLA_D_V:MLA_D_V + 1]).astype(BF16)


def _attention(qt, k, vt, batch, tq, tk, unroll):
    T = k.shape[0]
    L = T // batch
    nq = L // tq
    return pl.pallas_call(
        functools.partial(_attn_kernel, tk, unroll),
        grid=(batch, MLA_HEADS, nq),
        in_specs=[
            pl.BlockSpec((MLA_D_PAD, tq), lambda b, h, i: (h, b * nq + i)),
            pl.BlockSpec((L, MLA_D_PAD), lambda b, h, i: (b, h)),
            pl.BlockSpec((MLA_V_ROWS, L), lambda b, h, i: (h, b)),
        ],
        out_specs=pl.BlockSpec((MLA_D_V, tq), lambda b, h, i: (h, b * nq + i)),
        out_shape=jax.ShapeDtypeStruct((MLA_HEADS * MLA_D_V, T), BF16),
        compiler_params=_params(("parallel", "parallel", "arbitrary")),
        name="attention",
    )(qt, k, vt)


def _gate_lane_perm():
    lanes = jnp.arange(GDN_GATE_LANES)
    q, e = lanes // 8, lanes % 8
    return GDN_V_HEADS * (e // 2) + 2 * q + e % 2


def _gate_lane_params(a_log, dt_bias):
    lanes = jnp.arange(GDN_GATE_LANES)
    q, e = lanes // 8, lanes % 8
    head = 2 * q + e % 2
    direction = jnp.maximum(e // 2 - 2, 0)
    is_decay = e >= 4
    neg_a = jnp.where(is_decay, -jnp.exp(a_log.astype(F32))[direction, head], 0.0)
    dt = jnp.where(is_decay, dt_bias.astype(F32)[direction, head], 0.0)
    return neg_a, dt


def _pad_rope_cols(w):
    half = MLA_D_ROPE // 2
    z = jnp.zeros(w.shape[:-1] + (half,), w.dtype)
    return jnp.concatenate([w[..., :half], z, w[..., half:], z], axis=-1)


def _rope_tables(L):
    half = MLA_D_ROPE // 2
    inv_freq = ROPE_THETA ** (-jnp.arange(half, dtype=F32) / half)
    ang = jnp.arange(L, dtype=F32)[:, None] * inv_freq[None, :]
    c, s = jnp.cos(ang), jnp.sin(ang)
    z = jnp.zeros_like(c)
    return jnp.concatenate([c, z, c, z], axis=-1), jnp.concatenate([-s, z, s, z], axis=-1)


def _prepare(p):
    w = {}
    g_in = p['gdn_w_in'][0]
    w['gdn_main'] = g_in[:, :GDN_MAIN_DIM].astype(BF16)
    wg = g_in[:, GDN_MAIN_DIM:][:, _gate_lane_perm()].astype(BF16)
    w['gdn_gate'] = wg
    w['gdn_gate_t'] = wg.T
    w['gdn_neg_a'], w['gdn_dt'] = _gate_lane_params(p['gdn_a_log'][0], p['gdn_dt_bias'][0])
    w['gdn_conv'] = p['gdn_conv_w'][0].astype(F32)
    w['gdn_norm'] = p['gdn_norm_w'][0].reshape(1, GDN_DK).astype(F32)
    w['gdn_out'] = p['gdn_w_out'][0].astype(BF16)

    wa = p['mla_w_a'][0]
    rank = MLA_Q_RANK + MLA_KV_RANK
    w['mla_a'] = jnp.concatenate([wa[:, :rank], _pad_rope_cols(wa[:, rank:])], axis=-1).astype(BF16)
    wq = p['mla_w_q_b'][0].reshape(MLA_Q_RANK, MLA_HEADS, MLA_D_QK)
    wq = jnp.concatenate([wq[..., :MLA_D_NOPE], _pad_rope_cols(wq[..., MLA_D_NOPE:])], axis=-1)
    w['mla_q_t'] = wq.reshape(MLA_Q_RANK, MLA_HEADS * MLA_D_PAD).astype(BF16).T
    wkv = p['mla_w_kv_b'][0].reshape(MLA_KV_RANK, MLA_HEADS, MLA_D_NOPE + MLA_D_V)
    w['mla_kn'] = wkv[..., :MLA_D_NOPE].reshape(MLA_KV_RANK, MLA_HEADS * MLA_D_NOPE).astype(BF16)
    w['mla_v_t'] = wkv[..., MLA_D_NOPE:].reshape(MLA_KV_RANK, MLA_HEADS * MLA_D_V).astype(BF16).T
    w['mla_qn'] = p['mla_q_a_norm'][0].reshape(1, MLA_Q_RANK).astype(F32)
    w['mla_kvn'] = p['mla_kv_a_norm'][0].reshape(1, MLA_KV_RANK).astype(F32)
    w['mla_o'] = p['mla_w_o'][0].astype(BF16)

    w['ffn_in'] = p['ffn_w_in'].astype(BF16)
    w['ffn_out'] = p['ffn_w_out'].astype(BF16)
    for name in ('norm_mix_pre', 'norm_mix_post', 'norm_ffn_pre', 'norm_ffn_post'):
        w[name] = p[name].astype(F32)[:, None, :]
    return w


TILES = dict(
    proj_tm=512, proj_tn=1024,
    gate_tr=512,
    conv_tr=512, conv_tc=1024,
    solve_chunks=4,
    sweep_chunks=4, sweep_heads=4,
    out_tm=256,
    ffn_tm=512, ffn_tf=1024,
    mla_tm=256,
    attn_tq=256, attn_tk=512, attn_unroll=1,
)


def _trunk(x3, w, tiles):
    B, L, D = x3.shape
    x = x3.reshape(B * L, D)

    pre, gates, gates_t = _gdn_proj(x, w['norm_mix_pre'][0], w['gdn_main'], w['gdn_gate'], w['gdn_gate_t'],
                                    tiles['proj_tm'], tiles['proj_tn'])
    gsum, gsum_t = _gdn_gates(gates, gates_t, w['gdn_neg_a'], w['gdn_dt'], tiles['gate_tr'])
    qkv = _gdn_conv(pre, w['gdn_conv'], L, tiles['conv_tr'], tiles['conv_tc'])
    uf, wf, pf, ub, wb, pb = _gdn_delta_solve(qkv, gsum, gsum_t, tiles['solve_chunks'])
    sweep = functools.partial(_gdn_delta_sweep, batch=B, chunks=tiles['sweep_chunks'],
                              heads=tiles['sweep_heads'])
    o_fwd = sweep(False, qkv, uf, wf, pf, gsum, gsum_t)
    o = sweep(True, qkv, ub, wb, pb, gsum, gsum_t, extra=(o_fwd, pre, w['gdn_norm']))
    x = _out_proj(o, w['gdn_out'], w['norm_mix_post'][0], x, tiles['out_tm'])
    x = _ffn(x, w['norm_ffn_pre'][0], w['ffn_in'][0], w['ffn_out'][0], w['norm_ffn_post'][0],
             tiles['ffn_tm'], tiles['ffn_tf'])

    cos, sin = _rope_tables(L)
    qt, k, vt = _mla_proj(x, w['norm_mix_pre'][1], w['mla_a'], w['mla_qn'], w['mla_kvn'], w['mla_q_t'],
                          w['mla_kn'], w['mla_v_t'], cos, sin, L, tiles['mla_tm'])
    ot = _attention(qt, k, vt, B, tiles['attn_tq'], tiles['attn_tk'], tiles['attn_unroll'])
    x = _out_proj(ot, w['mla_o'], w['norm_mix_post'][1], x, tiles['out_tm'], transposed=True)
    x = _ffn(x, w['norm_ffn_pre'][1], w['ffn_in'][1], w['ffn_out'][1], w['norm_ffn_post'][1],
             tiles['ffn_tm'], tiles['ffn_tf'])
    return x.reshape(B, L, D)


def kernel(x_prompt, x_sample, norm_mix_pre, norm_mix_post, norm_ffn_pre, norm_ffn_post, gdn_w_in, gdn_conv_w, gdn_a_log, gdn_dt_bias, gdn_norm_w, gdn_w_out, mla_w_a, mla_q_a_norm, mla_w_q_b, mla_kv_a_norm, mla_w_kv_b, mla_w_o, ffn_w_in, ffn_w_out):
    w = _prepare(dict(
        norm_mix_pre=norm_mix_pre, norm_mix_post=norm_mix_post, norm_ffn_pre=norm_ffn_pre,
        norm_ffn_post=norm_ffn_post, gdn_w_in=gdn_w_in, gdn_conv_w=gdn_conv_w, gdn_a_log=gdn_a_log,
        gdn_dt_bias=gdn_dt_bias, gdn_norm_w=gdn_norm_w, gdn_w_out=gdn_w_out, mla_w_a=mla_w_a,
        mla_q_a_norm=mla_q_a_norm, mla_w_q_b=mla_w_q_b, mla_kv_a_norm=mla_kv_a_norm,
        mla_w_kv_b=mla_w_kv_b, mla_w_o=mla_w_o, ffn_w_in=ffn_w_in, ffn_w_out=ffn_w_out))
    return _trunk(x_prompt, w, TILES), _trunk(x_sample, w, TILES)
```

```python
import functools
import math

import jax
import jax.numpy as jnp
from jax import lax
from jax.experimental import pallas as pl
from jax.experimental.pallas import tpu as pltpu

F32 = jnp.float32
BF16 = jnp.bfloat16

RMS_EPS = 1e-6
L2_EPS = 1e-6
LANES = 128
BF16_ROWS = 16

GDN_QK_HEADS = 16
GDN_V_HEADS = 32
GDN_DK = 128
GDN_CONV = 5
GDN_CHUNK = 64
GDN_Q_DIM = GDN_QK_HEADS * GDN_DK
GDN_V_DIM = GDN_V_HEADS * GDN_DK
GDN_CONV_DIM = 2 * GDN_Q_DIM + GDN_V_DIM
GDN_MAIN_DIM = GDN_CONV_DIM + GDN_V_DIM
GDN_GATE_LANES = 4 * GDN_V_HEADS

MLA_HEADS = 16
MLA_Q_RANK = 768
MLA_KV_RANK = 512
MLA_D_NOPE = 128
MLA_D_ROPE = 64
MLA_D_V = 128
MLA_D_QK = MLA_D_NOPE + MLA_D_ROPE
MLA_D_PAD = 2 * LANES
MLA_V_ROWS = MLA_D_V + BF16_ROWS
ROPE_THETA = 10000.0

VMEM_LIMIT = 56 * 1024 * 1024


def _params(sem):
    return pltpu.CompilerParams(dimension_semantics=sem, vmem_limit_bytes=VMEM_LIMIT)


def _resident(shape, index_map):
    return pl.BlockSpec(shape, index_map, pipeline_mode=pl.Buffered(1))


def _rms(x, w):
    return x * lax.rsqrt(jnp.mean(x * x, axis=-1, keepdims=True) + RMS_EPS) * w


def _dot(a, b):
    return jnp.dot(a, b, preferred_element_type=F32)


def _dot_nt(a, b):
    return lax.dot_general(a, b, (((1,), (1,)), ((), ())), preferred_element_type=F32)


def _dot_tn(a, b):
    return lax.dot_general(a, b, (((0,), (0,)), ((), ())), preferred_element_type=F32)


def _split3(x):
    hi = x.astype(BF16)
    r1 = x - hi.astype(F32)
    mid = r1.astype(BF16)
    lo = (r1 - mid.astype(F32)).astype(BF16)
    return hi, mid, lo


def _silu(x):
    return x * (1.0 / (1.0 + jnp.exp(-x)))


def _gdn_proj_kernel(x_ref, nw_ref, w_ref, wg_ref, wgt_ref, o_ref, g_ref, gt_ref, xn_ref):
    @pl.when(pl.program_id(1) == 0)
    def _():
        xn = _rms(x_ref[...], nw_ref[...]).astype(BF16)
        xn_ref[...] = xn
        g_ref[...] = _dot(xn, wg_ref[...])
        gt_ref[...] = _dot_nt(wgt_ref[...], xn)

    o_ref[...] = _dot(xn_ref[...], w_ref[...]).astype(BF16)


def _gdn_proj(x, nw, w, wg, wgt, tm, tn):
    T, D = x.shape
    N = w.shape[1]
    return pl.pallas_call(
        _gdn_proj_kernel,
        grid=(T // tm, N // tn),
        in_specs=[
            pl.BlockSpec((tm, D), lambda i, j: (i, 0)),
            pl.BlockSpec((1, D), lambda i, j: (0, 0)),
            pl.BlockSpec((D, tn), lambda i, j: (0, j)),
            pl.BlockSpec((D, LANES), lambda i, j: (0, 0)),
            pl.BlockSpec((LANES, D), lambda i, j: (0, 0)),
        ],
        out_specs=[
            pl.BlockSpec((tm, tn), lambda i, j: (i, j)),
            pl.BlockSpec((tm, LANES), lambda i, j: (i, 0)),
            pl.BlockSpec((LANES, tm), lambda i, j: (0, i)),
        ],
        out_shape=[
            jax.ShapeDtypeStruct((T, N), BF16),
            jax.ShapeDtypeStruct((T, LANES), F32),
            jax.ShapeDtypeStruct((LANES, T), F32),
        ],
        scratch_shapes=[pltpu.VMEM((tm, D), BF16)],
        compiler_params=_params(("parallel", "arbitrary")),
        name="gdn_proj",
    )(x, nw, w, wg, wgt)


def _softplus(y):
    return jnp.maximum(y, 0.0) + jnp.log1p(jnp.exp(-jnp.abs(y)))


def _gate_values(x, neg_a, dt):
    beta = 1.0 / (1.0 + jnp.exp(-x))
    g = neg_a * _softplus(x + dt)
    return beta, g


def _gdn_gates_kernel(x_ref, xt_ref, na_ref, dt_ref, nat_ref, dtt_ref, o_ref, ot_ref):
    R = x_ref.shape[0]
    ii = lax.broadcasted_iota(jnp.int32, (R, R), 0)
    jj = lax.broadcasted_iota(jnp.int32, (R, R), 1)
    same = (ii // GDN_CHUNK) == (jj // GDN_CHUNK)
    lower = jnp.where(same & (jj <= ii), 1.0, 0.0).astype(BF16)
    upper = jnp.where(same & (jj >= ii), 1.0, 0.0).astype(BF16)

    beta, g = _gate_values(x_ref[...], na_ref[...], dt_ref[...])
    parts = _split3(g)
    cf = sum(_dot(lower, p) for p in parts)
    cb = sum(_dot(upper, p) for p in parts)
    e = lax.broadcasted_iota(jnp.int32, (R, LANES), 1) % 8
    o_ref[...] = jnp.where(e < 4, beta, jnp.where(e < 6, cf, cb))

    beta_t, g_t = _gate_values(xt_ref[...], nat_ref[...], dtt_ref[...])
    parts_t = _split3(g_t)
    cf_t = sum(_dot(p, upper) for p in parts_t)
    cb_t = sum(_dot(p, lower) for p in parts_t)
    e_t = lax.broadcasted_iota(jnp.int32, (LANES, R), 0) % 8
    ot_ref[...] = jnp.where(e_t < 4, beta_t, jnp.where(e_t < 6, cf_t, cb_t))


def _gdn_gates(gates, gates_t, neg_a, dt, tr):
    T = gates.shape[0]
    row = lambda i: (i, 0)
    col = lambda i: (0, i)
    fixed = lambda i: (0, 0)
    return pl.pallas_call(
        _gdn_gates_kernel,
        grid=(T // tr,),
        in_specs=[
            pl.BlockSpec((tr, LANES), row),
            pl.BlockSpec((LANES, tr), col),
            pl.BlockSpec((1, LANES), fixed),
            pl.BlockSpec((1, LANES), fixed),
            pl.BlockSpec((LANES, 1), fixed),
            pl.BlockSpec((LANES, 1), fixed),
        ],
        out_specs=[pl.BlockSpec((tr, LANES), row), pl.BlockSpec((LANES, tr), col)],
        out_shape=[jax.ShapeDtypeStruct((T, LANES), F32), jax.ShapeDtypeStruct((LANES, T), F32)],
        compiler_params=_params(("parallel",)),
        name="gdn_gates",
    )(gates, gates_t, neg_a.reshape(1, LANES), dt.reshape(1, LANES),
      neg_a.reshape(LANES, 1), dt.reshape(LANES, 1))


CONV_HALO = BF16_ROWS
CONV_PAD = 8


def _gdn_conv_kernel(blocks_per_seq, n_q_blocks, n_qk_blocks,
                     prev_ref, main_ref, next_ref, w_ref, o_ref, xs_ref):
    i = pl.program_id(0)
    j = pl.program_id(1)
    tr, tc = main_ref.shape
    r = GDN_CONV // 2
    pos = i % blocks_per_seq
    keep_prev = jnp.where(pos == 0, 0.0, 1.0)
    keep_next = jnp.where(pos == blocks_per_seq - 1, 0.0, 1.0)
    xs_ref[CONV_PAD - r:CONV_PAD, :] = prev_ref[CONV_HALO - r:, :].astype(F32) * keep_prev
    xs_ref[CONV_PAD:CONV_PAD + tr, :] = main_ref[...].astype(F32)
    xs_ref[CONV_PAD + tr:CONV_PAD + tr + r, :] = next_ref[:r, :].astype(F32) * keep_next
    w = w_ref[...]
    acc = xs_ref[CONV_PAD - r:CONV_PAD - r + tr, :] * w[0:1, :]
    for t in range(1, GDN_CONV):
        acc = acc + xs_ref[CONV_PAD - r + t:CONV_PAD - r + t + tr, :] * w[t:t + 1, :]
    y = _silu(acc)

    @pl.when(j >= n_qk_blocks)
    def _():
        o_ref[...] = y.astype(BF16)

    @pl.when(j < n_qk_blocks)
    def _():
        q_scale = jnp.where(j < n_q_blocks, GDN_DK ** -0.5, 1.0)
        for h in range(tc // LANES):
            blk = y[:, h * LANES:(h + 1) * LANES]
            inv = lax.rsqrt(jnp.sum(blk * blk, axis=-1, keepdims=True) + L2_EPS) * q_scale
            o_ref[:, h * LANES:(h + 1) * LANES] = (blk * inv).astype(BF16)


def _gdn_conv(pre, conv_w, seq_len, tr, tc):
    T = pre.shape[0]
    C = conv_w.shape[1]
    hb = tr // CONV_HALO
    n_halo = T // CONV_HALO
    kern = functools.partial(_gdn_conv_kernel, seq_len // tr, GDN_Q_DIM // tc, 2 * GDN_Q_DIM // tc)
    return pl.pallas_call(
        kern,
        grid=(T // tr, C // tc),
        in_specs=[
            pl.BlockSpec((CONV_HALO, tc), lambda i, j: (jnp.maximum(i * hb - 1, 0), j)),
            pl.BlockSpec((tr, tc), lambda i, j: (i, j)),
            pl.BlockSpec((CONV_HALO, tc), lambda i, j: (jnp.minimum((i + 1) * hb, n_halo - 1), j)),
            pl.BlockSpec((GDN_CONV, tc), lambda i, j: (0, j)),
        ],
        out_specs=pl.BlockSpec((tr, tc), lambda i, j: (i, j)),
        out_shape=jax.ShapeDtypeStruct((T, C), BF16),
        scratch_shapes=[pltpu.VMEM((tr + 2 * CONV_PAD, tc), F32)],
        compiler_params=_params(("parallel", "parallel")),
        name="gdn_conv",
    )(pre, pre, pre, conv_w)


NEUMANN_DOUBLINGS = 5


def _gate_lane(direction, v_head, decay):
    return (4 if decay else 0) + 2 * direction + v_head


def _delta_solve_kernel(chunks, q_ref, k_ref, v_ref, g_ref, gt_ref,
                        uf_ref, wf_ref, pf_ref, qf_ref, kf_ref, ub_ref, wb_ref, pb_ref, qb_ref, kb_ref):
    C = GDN_CHUNK
    qh = pl.program_id(1)
    gsel = pltpu.roll(g_ref[...], (LANES - 8 * qh) % LANES, axis=1)
    ii = lax.broadcasted_iota(jnp.int32, (C, C), 0)
    jj = lax.broadcasted_iota(jnp.int32, (C, C), 1)
    eye = jnp.where(ii == jj, 1.0, 0.0)
    masks = ((ii >= jj, ii > jj), (ii <= jj, ii < jj))
    outs = ((uf_ref, wf_ref, pf_ref, qf_ref, kf_ref), (ub_ref, wb_ref, pb_ref, qb_ref, kb_ref))
    eye_k = jnp.where(lax.broadcasted_iota(jnp.int32, (GDN_DK, GDN_DK), 0)
                      == lax.broadcasted_iota(jnp.int32, (GDN_DK, GDN_DK), 1), 1.0, 0.0).astype(BF16)

    kqs = []
    for ci in range(chunks):
        rows = slice(ci * C, (ci + 1) * C)
        kc = k_ref[rows, :]
        kqs.append(_dot_nt(jnp.concatenate([kc, q_ref[rows, :], eye_k], axis=0), kc))

    chains = [(ci, d, hv) for ci in range(chunks) for d in range(2) for hv in range(2)]
    ts, xs, rhs = [], [], []
    for ci, d, hv in chains:
        rows = slice(ci * C, (ci + 1) * C)
        cols = slice(hv * LANES, (hv + 1) * LANES)
        incl, strict = masks[d]
        lb = _gate_lane(d, hv, False)
        lg = _gate_lane(d, hv, True)
        beta = gsel[rows, lb:lb + 1]
        gcc = gsel[rows, lg:lg + 1]
        gcr = gt_ref[lg:lg + 1, rows]
        gl = gcr[:, 0:1] if d else gcr[:, C - 1:C]
        decay = jnp.where(incl, jnp.exp(jnp.where(incl, gcc - gcr, 0.0)), 0.0)
        a = jnp.where(strict, beta * kqs[ci][:C] * decay, 0.0)
        ts.append(eye - a)
        xs.append(a)
        eg = jnp.exp(gcc)
        u_ref, w_ref, p_ref, qd_ref, kdt_ref = outs[d]
        p_ref[rows, hv * C:(hv + 1) * C] = (kqs[ci][C:2 * C] * decay).astype(BF16)
        qd_ref[rows, cols] = (q_ref[rows, :].astype(F32) * eg).astype(BF16)
        kdt_ref[cols, rows] = (kqs[ci][2 * C:] * jnp.exp(gl - gcr)).astype(BF16)
        vb = v_ref[rows, cols].astype(F32) * beta
        kb = k_ref[rows, :].astype(F32) * (beta * eg)
        rhs.append(jnp.concatenate([vb, kb], axis=1).astype(BF16))

    for _ in range(NEUMANN_DOUBLINGS):
        xs = [_dot(x.astype(BF16), x.astype(BF16)) for x in xs]
        ts = [t + _dot(t.astype(BF16), x.astype(BF16)) for t, x in zip(ts, xs)]

    uws = [_dot(t.astype(BF16), r) for t, r in zip(ts, rhs)]
    for (ci, d, hv), uw in zip(chains, uws):
        rows = slice(ci * C, (ci + 1) * C)
        cols = slice(hv * LANES, (hv + 1) * LANES)
        outs[d][0][rows, cols] = uw[:, :LANES].astype(BF16)
        outs[d][1][rows, cols] = uw[:, LANES:].astype(BF16)


def _gdn_delta_solve(qkv, gsum, gsum_t, chunks):
    T = qkv.shape[0]
    R = chunks * GDN_CHUNK
    k_block0 = GDN_Q_DIM // LANES
    v_block0 = 2 * GDN_Q_DIM // (2 * LANES)
    wide = pl.BlockSpec((R, 2 * LANES), lambda i, h: (i, h))
    narrow = pl.BlockSpec((R, LANES), lambda i, h: (i, h))
    tall = pl.BlockSpec((2 * GDN_DK, R), lambda i, h: (h, i))
    uw_shape = jax.ShapeDtypeStruct((T, GDN_V_DIM), BF16)
    p_shape = jax.ShapeDtypeStruct((T, GDN_V_HEADS * GDN_CHUNK), BF16)
    kt_shape = jax.ShapeDtypeStruct((GDN_V_HEADS * GDN_DK, T), BF16)
    return pl.pallas_call(
        functools.partial(_delta_solve_kernel, chunks),
        grid=(T // R, GDN_QK_HEADS),
        in_specs=[
            narrow,
            pl.BlockSpec((R, LANES), lambda i, h: (i, k_block0 + h)),
            pl.BlockSpec((R, 2 * LANES), lambda i, h: (i, v_block0 + h)),
            pl.BlockSpec((R, LANES), lambda i, h: (i, 0)),
            pl.BlockSpec((8, R), lambda i, h: (h, i)),
        ],
        out_specs=[wide, wide, narrow, wide, tall] * 2,
        out_shape=[uw_shape, uw_shape, p_shape, uw_shape, kt_shape] * 2,
        compiler_params=_params(("parallel", "parallel")),
        name="gdn_delta_solve",
    )(qkv, qkv, qkv, gsum, gsum_t)


def _delta_sweep_kernel(rev, chunks, heads, *refs):
    if rev:
        u_ref, w_ref, p_ref, qd_ref, kdt_ref, gt_ref, of_ref, z_ref, nw_ref, o_ref, s_ref = refs
    else:
        u_ref, w_ref, p_ref, qd_ref, kdt_ref, gt_ref, o_ref, s_ref = refs
    C = GDN_CHUNK
    d = 1 if rev else 0

    @pl.when(pl.program_id(2) == 0)
    def _():
        s_ref[...] = jnp.zeros_like(s_ref)

    states = range(2 * heads)
    zeros = jnp.zeros((C, GDN_DK), BF16)
    pad = lambda v, half: jnp.concatenate([zeros, v] if half else [v, zeros], axis=0)
    order = range(chunks - 1, -1, -1) if rev else range(chunks)
    for ci in order:
        rows = slice(ci * C, (ci + 1) * C)
        pair = slice((ci // 2) * 2 * C, (ci // 2 + 1) * 2 * C)
        last = ci * C + (0 if rev else C - 1)
        cols = [slice(h * LANES, (h + 1) * LANES) for h in states]
        s_old = [s_ref[h] for h in states]
        wqs = [_dot(jnp.concatenate([w_ref[rows, cols[h]], qd_ref[rows, cols[h]]], axis=0),
                    s_old[h].astype(BF16)) for h in states]
        vns = [(u_ref[rows, cols[h]].astype(F32) - wqs[h][:C]).astype(BF16) for h in states]
        pvs = [_dot(p_ref[rows, (h // 2) * LANES:(h // 2 + 1) * LANES], pad(vns[h], h % 2)) for h in states]
        kvs = [_dot(kdt_ref[cols[h], pair], pad(vns[h], ci % 2)) for h in states]
        for h in states:
            lg = 8 * (h // 2) + _gate_lane(d, h % 2, True)
            s_ref[h] = s_old[h] * jnp.exp(gt_ref[lg:lg + 1, last:last + 1]) + kvs[h]
            o = wqs[h][C:] + pvs[h]
            if rev:
                tot = of_ref[rows, cols[h]] + o
                gate = _silu(z_ref[rows, cols[h]].astype(F32))
                o_ref[rows, cols[h]] = (_rms(tot, nw_ref[...]) * gate).astype(BF16)
            else:
                o_ref[rows, cols[h]] = o


def _gdn_delta_sweep(rev, u, w, p, qd, kdt, gsum_t, batch, chunks, heads, extra=None):
    T = u.shape[0]
    R = chunks * GDN_CHUNK
    nb = T // batch // R
    assert chunks % 2 == 0
    rows = (lambda b, n: b * nb + (nb - 1 - n)) if rev else (lambda b, n: b * nb + n)
    qk_w = heads * LANES
    v_w = 2 * heads * LANES
    wide = pl.BlockSpec((R, v_w), lambda b, h, n: (rows(b, n), h))
    narrow = pl.BlockSpec((R, qk_w), lambda b, h, n: (rows(b, n), h))
    in_specs = [
        wide, wide, narrow, wide,
        pl.BlockSpec((v_w, R), lambda b, h, n: (h, rows(b, n))),
        pl.BlockSpec((8 * heads, R), lambda b, h, n: (h, rows(b, n))),
    ]
    args = [u, w, p, qd, kdt, gsum_t]
    if rev:
        o_fwd, pre, norm_w = extra
        z_block0 = GDN_CONV_DIM // v_w
        in_specs += [
            wide,
            pl.BlockSpec((R, v_w), lambda b, h, n: (rows(b, n), z_block0 + h)),
            pl.BlockSpec((1, LANES), lambda b, h, n: (0, 0)),
        ]
        args += [o_fwd, pre, norm_w]
    return pl.pallas_call(
        functools.partial(_delta_sweep_kernel, rev, chunks, heads),
        grid=(batch, GDN_QK_HEADS // heads, nb),
        in_specs=in_specs,
        out_specs=wide,
        out_shape=jax.ShapeDtypeStruct((T, GDN_V_DIM), BF16 if rev else F32),
        scratch_shapes=[pltpu.VMEM((2 * heads, GDN_DK, GDN_DK), F32)],
        compiler_params=_params(("parallel", "parallel", "arbitrary")),
        name="gdn_sweep_bwd" if rev else "gdn_sweep_fwd",
    )(*args)


def _out_proj_kernel(transposed, a_ref, w_ref, nw_ref, x_ref, o_ref):
    m = _dot_tn(a_ref[...], w_ref[...]) if transposed else _dot(a_ref[...], w_ref[...])
    o_ref[...] = x_ref[...] + _rms(m, nw_ref[...])


def _out_proj(a, w, nw, x, tm, transposed=False):
    K, D = w.shape
    T = x.shape[0]
    a_spec = pl.BlockSpec((K, tm), lambda i: (0, i)) if transposed else pl.BlockSpec((tm, K), lambda i: (i, 0))
    return pl.pallas_call(
        functools.partial(_out_proj_kernel, transposed),
        grid=(T // tm,),
        in_specs=[
            a_spec,
            _resident((K, D), lambda i: (0, 0)),
            pl.BlockSpec((1, D), lambda i: (0, 0)),
            pl.BlockSpec((tm, D), lambda i: (i, 0)),
        ],
        out_specs=pl.BlockSpec((tm, D), lambda i: (i, 0)),
        out_shape=jax.ShapeDtypeStruct((T, D), F32),
        compiler_params=_params(("parallel",)),
        name="out_proj_t" if transposed else "out_proj",
    )(a, w, nw, x)


def _ffn_kernel(x_ref, nw1_ref, w1_ref, w2_ref, nw2_ref, o_ref, xn_ref, acc_ref):
    j = pl.program_id(1)

    @pl.when(j == 0)
    def _():
        xn_ref[...] = _rms(x_ref[...], nw1_ref[...]).astype(BF16)
        acc_ref[...] = jnp.zeros_like(acc_ref)

    h = jnp.maximum(_dot(xn_ref[...], w1_ref[...]), 0.0)
    acc_ref[...] += _dot((h * h).astype(BF16), w2_ref[...])

    @pl.when(j == pl.num_programs(1) - 1)
    def _():
        o_ref[...] = x_ref[...] + _rms(acc_ref[...], nw2_ref[...])


def _ffn(x, nw1, w1, w2, nw2, tm, tf):
    T, D = x.shape
    Fd = w1.shape[1]
    return pl.pallas_call(
        _ffn_kernel,
        grid=(T // tm, Fd // tf),
        in_specs=[
            pl.BlockSpec((tm, D), lambda i, j: (i, 0)),
            pl.BlockSpec((1, D), lambda i, j: (0, 0)),
            pl.BlockSpec((D, tf), lambda i, j: (0, j)),
            pl.BlockSpec((tf, D), lambda i, j: (j, 0)),
            pl.BlockSpec((1, D), lambda i, j: (0, 0)),
        ],
        out_specs=pl.BlockSpec((tm, D), lambda i, j: (i, 0)),
        out_shape=jax.ShapeDtypeStruct((T, D), F32),
        scratch_shapes=[pltpu.VMEM((tm, D), BF16), pltpu.VMEM((tm, D), F32)],
        compiler_params=_params(("parallel", "arbitrary")),
        name="ffn",
    )(x, nw1, w1, w2, nw2)


def _mla_proj_kernel(x_ref, nw_ref, wa_ref, qn_ref, kvn_ref, wqt_ref, wkn_ref, wvt_ref,
                     cos_ref, sin_ref, cost_ref, sint_ref, qt_ref, k_ref, vt_ref):
    xn = _rms(x_ref[...], nw_ref[...]).astype(BF16)
    a = _dot(xn, wa_ref[...])
    cq = _rms(a[:, :MLA_Q_RANK], qn_ref[...]).astype(BF16)
    ckv = _rms(a[:, MLA_Q_RANK:MLA_Q_RANK + MLA_KV_RANK], kvn_ref[...]).astype(BF16)

    kr = a[:, MLA_Q_RANK + MLA_KV_RANK:]
    k_rope = (kr * cos_ref[...] + pltpu.roll(kr, LANES // 2, axis=1) * sin_ref[...]).astype(BF16)
    kn = _dot(ckv, wkn_ref[...])
    for h in range(MLA_HEADS):
        c0 = h * MLA_D_PAD
        k_ref[:, c0:c0 + LANES] = kn[:, h * LANES:(h + 1) * LANES].astype(BF16)
        k_ref[:, c0 + LANES:c0 + 2 * LANES] = k_rope

    scale = MLA_D_QK ** -0.5 * math.log2(math.e)
    qt = _dot_nt(wqt_ref[...], cq)
    cost = cost_ref[...]
    sint = sint_ref[...]
    half = LANES // 2
    for h in range(MLA_HEADS):
        r0 = h * MLA_D_PAD
        qt_ref[r0:r0 + LANES, :] = (qt[r0:r0 + LANES] * scale).astype(BF16)
        blk = qt[r0 + LANES:r0 + 2 * LANES]
        swapped = jnp.concatenate([blk[half:], blk[:half]], axis=0)
        qt_ref[r0 + LANES:r0 + 2 * LANES, :] = ((blk * cost + swapped * sint) * scale).astype(BF16)

    vt = _dot_nt(wvt_ref[...], ckv)
    ones = jnp.ones((MLA_V_ROWS - MLA_D_V, vt.shape[1]), BF16)
    for h in range(MLA_HEADS):
        r0 = h * MLA_V_ROWS
        vt_ref[r0:r0 + MLA_D_V, :] = vt[h * MLA_D_V:(h + 1) * MLA_D_V].astype(BF16)
        vt_ref[r0 + MLA_D_V:r0 + MLA_V_ROWS, :] = ones


def _mla_proj(x, nw, wa, qn, kvn, wqt, wkn, wvt, cos, sin, seq_len, tm):
    T, D = x.shape
    pos_blocks = seq_len // tm
    row = lambda i: (i, 0)
    col = lambda i: (0, i)
    fixed = lambda i: (0, 0)
    return pl.pallas_call(
        _mla_proj_kernel,
        grid=(T // tm,),
        in_specs=[
            pl.BlockSpec((tm, D), row),
            pl.BlockSpec((1, D), fixed),
            _resident(wa.shape, fixed),
            pl.BlockSpec((1, MLA_Q_RANK), fixed),
            pl.BlockSpec((1, MLA_KV_RANK), fixed),
            _resident(wqt.shape, fixed),
            _resident(wkn.shape, fixed),
            _resident(wvt.shape, fixed),
            pl.BlockSpec((tm, LANES), lambda i: (i % pos_blocks, 0)),
            pl.BlockSpec((tm, LANES), lambda i: (i % pos_blocks, 0)),
            pl.BlockSpec((LANES, tm), lambda i: (0, i % pos_blocks)),
            pl.BlockSpec((LANES, tm), lambda i: (0, i % pos_blocks)),
        ],
        out_specs=[
            pl.BlockSpec((MLA_HEADS * MLA_D_PAD, tm), col),
            pl.BlockSpec((tm, MLA_HEADS * MLA_D_PAD), row),
            pl.BlockSpec((MLA_HEADS * MLA_V_ROWS, tm), col),
        ],
        out_shape=[
            jax.ShapeDtypeStruct((MLA_HEADS * MLA_D_PAD, T), BF16),
            jax.ShapeDtypeStruct((T, MLA_HEADS * MLA_D_PAD), BF16),
            jax.ShapeDtypeStruct((MLA_HEADS * MLA_V_ROWS, T), BF16),
        ],
        compiler_params=_params(("parallel",)),
        name="mla_proj",
    )(x, nw, wa, qn, kvn, wqt, wkn, wvt, cos, sin, cos.T, sin.T)


ATTN_SOFTMAX_ROWS = 64


def _attn_kernel(tk, unroll, qt_ref, k_ref, vt_ref, ot_ref, s_ref, p_ref, acc_ref):
    tq = qt_ref.shape[1]
    n = k_ref.shape[0] // tk
    R = ATTN_SOFTMAX_ROWS

    def chunk(t):
        return pl.ds(t * tk if isinstance(t, int) else pl.multiple_of(t * tk, tk), tk)

    def scores(t, slot):
        s_ref[slot] = _dot(k_ref[chunk(t), :], qt_ref[...])

    def weighted(t, slot, a):
        acc_ref[...] = a * acc_ref[...] + _dot(vt_ref[:, chunk(t)], p_ref[slot])

    def softmax(slot, m):
        mx = s_ref[slot, 0:R, :]
        for r in range(R, tk, R):
            mx = jnp.maximum(mx, s_ref[slot, r:r + R, :])
        m_new = jnp.maximum(m, jnp.max(mx, axis=0, keepdims=True))
        for r in range(0, tk, R):
            p_ref[slot, r:r + R, :] = jnp.exp2(s_ref[slot, r:r + R, :] - m_new).astype(BF16)
        return m_new, jnp.exp2(m - m_new)

    def trip(t, parity, m, a):
        weighted(t - 1, 1 - parity, a)
        scores(t + 1, 1 - parity)
        return softmax(parity, m)

    def body(i, carry):
        m, a = carry
        t0 = 1 + unroll * i
        for j in range(unroll):
            m, a = trip(t0 + j, (1 + j) % 2, m, a)
        return m, a

    acc_ref[...] = jnp.zeros_like(acc_ref)
    scores(0, 0)
    m, a = softmax(0, jnp.full((1, tq), -jnp.inf, F32))
    scores(1, 1)
    bodies = (n - 2) // unroll
    m, a = lax.fori_loop(0, bodies, body, (m, a))
    for t in range(1 + bodies * unroll, n - 1):
        m, a = trip(t, t % 2, m, a)
    weighted(n - 2, (n - 2) % 2, a)
    m, a = softmax((n - 1) % 2, m)
    weighted(n - 1, (n - 1) % 2, a)
    acc = acc_ref[...]
    ot_ref[...] = (acc[:MLA_D_V] / acc[MLA_D_V:MLA_D_V + 1]).astype(BF16)


def _attention(qt, k, vt, batch, tq, tk, unroll):
    T = k.shape[0]
    L = T // batch
    nq = L // tq
    assert unroll % 2 == 0 and L // tk >= 2
    return pl.pallas_call(
        functools.partial(_attn_kernel, tk, unroll),
        grid=(batch, MLA_HEADS, nq),
        in_specs=[
            pl.BlockSpec((MLA_D_PAD, tq), lambda b, h, i: (h, b * nq + i)),
            pl.BlockSpec((L, MLA_D_PAD), lambda b, h, i: (b, h)),
            pl.BlockSpec((MLA_V_ROWS, L), lambda b, h, i: (h, b)),
        ],
        out_specs=pl.BlockSpec((MLA_D_V, tq), lambda b, h, i: (h, b * nq + i)),
        out_shape=jax.ShapeDtypeStruct((MLA_HEADS * MLA_D_V, T), BF16),
        scratch_shapes=[pltpu.VMEM((2, tk, tq), F32), pltpu.VMEM((2, tk, tq), BF16),
                        pltpu.VMEM((MLA_V_ROWS, tq), F32)],
        compiler_params=_params(("parallel", "parallel", "arbitrary")),
        name="attention",
    )(qt, k, vt)


def _gate_lane_perm():
    lanes = jnp.arange(GDN_GATE_LANES)
    q, e = lanes // 8, lanes % 8
    return GDN_V_HEADS * (e // 2) + 2 * q + e % 2


def _gate_lane_params(a_log, dt_bias):
    lanes = jnp.arange(GDN_GATE_LANES)
    q, e = lanes // 8, lanes % 8
    head = 2 * q + e % 2
    direction = jnp.maximum(e // 2 - 2, 0)
    is_decay = e >= 4
    neg_a = jnp.where(is_decay, -jnp.exp(a_log.astype(F32))[direction, head], 0.0)
    dt = jnp.where(is_decay, dt_bias.astype(F32)[direction, head], 0.0)
    return neg_a, dt


def _pad_rope_cols(w):
    half = MLA_D_ROPE // 2
    z = jnp.zeros(w.shape[:-1] + (half,), w.dtype)
    return jnp.concatenate([w[..., :half], z, w[..., half:], z], axis=-1)


def _rope_tables(L):
    half = MLA_D_ROPE // 2
    inv_freq = ROPE_THETA ** (-jnp.arange(half, dtype=F32) / half)
    ang = jnp.arange(L, dtype=F32)[:, None] * inv_freq[None, :]
    c, s = jnp.cos(ang), jnp.sin(ang)
    z = jnp.zeros_like(c)
    return jnp.concatenate([c, z, c, z], axis=-1), jnp.concatenate([-s, z, s, z], axis=-1)


def _prepare(p):
    w = {}
    g_in = p['gdn_w_in'][0]
    w['gdn_main'] = g_in[:, :GDN_MAIN_DIM].astype(BF16)
    wg = g_in[:, GDN_MAIN_DIM:][:, _gate_lane_perm()].astype(BF16)
    w['gdn_gate'] = wg
    w['gdn_gate_t'] = wg.T
    w['gdn_neg_a'], w['gdn_dt'] = _gate_lane_params(p['gdn_a_log'][0], p['gdn_dt_bias'][0])
    w['gdn_conv'] = p['gdn_conv_w'][0].astype(F32)
    w['gdn_norm'] = p['gdn_norm_w'][0].reshape(1, GDN_DK).astype(F32)
    w['gdn_out'] = p['gdn_w_out'][0].astype(BF16)

    wa = p['mla_w_a'][0]
    rank = MLA_Q_RANK + MLA_KV_RANK
    w['mla_a'] = jnp.concatenate([wa[:, :rank], _pad_rope_cols(wa[:, rank:])], axis=-1).astype(BF16)
    wq = p['mla_w_q_b'][0].reshape(MLA_Q_RANK, MLA_HEADS, MLA_D_QK)
    wq = jnp.concatenate([wq[..., :MLA_D_NOPE], _pad_rope_cols(wq[..., MLA_D_NOPE:])], axis=-1)
    w['mla_q_t'] = wq.reshape(MLA_Q_RANK, MLA_HEADS * MLA_D_PAD).astype(BF16).T
    wkv = p['mla_w_kv_b'][0].reshape(MLA_KV_RANK, MLA_HEADS, MLA_D_NOPE + MLA_D_V)
    w['mla_kn'] = wkv[..., :MLA_D_NOPE].reshape(MLA_KV_RANK, MLA_HEADS * MLA_D_NOPE).astype(BF16)
    w['mla_v_t'] = wkv[..., MLA_D_NOPE:].reshape(MLA_KV_RANK, MLA_HEADS * MLA_D_V).astype(BF16).T
    w['mla_qn'] = p['mla_q_a_norm'][0].reshape(1, MLA_Q_RANK).astype(F32)
    w['mla_kvn'] = p['mla_kv_a_norm'][0].reshape(1, MLA_KV_RANK).astype(F32)
    w['mla_o'] = p['mla_w_o'][0].astype(BF16)

    w['ffn_in'] = p['ffn_w_in'].astype(BF16)
    w['ffn_out'] = p['ffn_w_out'].astype(BF16)
    for name in ('norm_mix_pre', 'norm_mix_post', 'norm_ffn_pre', 'norm_ffn_post'):
        w[name] = p[name].astype(F32)[:, None, :]
    return w


TILES = dict(
    proj_tm=512, proj_tn=1024,
    gate_tr=512,
    conv_tr=512, conv_tc=1024,
    solve_chunks=4,
    sweep_chunks=4, sweep_heads=4,
    out_tm=256,
    ffn_tm=512, ffn_tf=1024,
    mla_tm=256,
    attn_tq=256, attn_tk=512, attn_unroll=2,
)


def _trunk(x3, w, tiles):
    B, L, D = x3.shape
    x = x3.reshape(B * L, D)

    pre, gates, gates_t = _gdn_proj(x, w['norm_mix_pre'][0], w['gdn_main'], w['gdn_gate'], w['gdn_gate_t'],
                                    tiles['proj_tm'], tiles['proj_tn'])
    gsum, gsum_t = _gdn_gates(gates, gates_t, w['gdn_neg_a'], w['gdn_dt'], tiles['gate_tr'])
    qkv = _gdn_conv(pre, w['gdn_conv'], L, tiles['conv_tr'], tiles['conv_tc'])
    solved = _gdn_delta_solve(qkv, gsum, gsum_t, tiles['solve_chunks'])
    sweep = functools.partial(_gdn_delta_sweep, gsum_t=gsum_t, batch=B, chunks=tiles['sweep_chunks'],
                              heads=tiles['sweep_heads'])
    o_fwd = sweep(False, *solved[:5])
    o = sweep(True, *solved[5:], extra=(o_fwd, pre, w['gdn_norm']))
    x = _out_proj(o, w['gdn_out'], w['norm_mix_post'][0], x, tiles['out_tm'])
    x = _ffn(x, w['norm_ffn_pre'][0], w['ffn_in'][0], w['ffn_out'][0], w['norm_ffn_post'][0],
             tiles['ffn_tm'], tiles['ffn_tf'])

    cos, sin = _rope_tables(L)
    qt, k, vt = _mla_proj(x, w['norm_mix_pre'][1], w['mla_a'], w['mla_qn'], w['mla_kvn'], w['mla_q_t'],
                          w['mla_kn'], w['mla_v_t'], cos, sin, L, tiles['mla_tm'])
    ot = _attention(qt, k, vt, B, tiles['attn_tq'], tiles['attn_tk'], tiles['attn_unroll'])
    x = _out_proj(ot, w['mla_o'], w['norm_mix_post'][1], x, tiles['out_tm'], transposed=True)
    x = _ffn(x, w['norm_ffn_pre'][1], w['ffn_in'][1], w['ffn_out'][1], w['norm_ffn_post'][1],
             tiles['ffn_tm'], tiles['ffn_tf'])
    return x.reshape(B, L, D)


def kernel(x_prompt, x_sample, norm_mix_pre, norm_mix_post, norm_ffn_pre, norm_ffn_post, gdn_w_in, gdn_conv_w, gdn_a_log, gdn_dt_bias, gdn_norm_w, gdn_w_out, mla_w_a, mla_q_a_norm, mla_w_q_b, mla_kv_a_norm, mla_w_kv_b, mla_w_o, ffn_w_in, ffn_w_out):
    w = _prepare(dict(
        norm_mix_pre=norm_mix_pre, norm_mix_post=norm_mix_post, norm_ffn_pre=norm_ffn_pre,
        norm_ffn_post=norm_ffn_post, gdn_w_in=gdn_w_in, gdn_conv_w=gdn_conv_w, gdn_a_log=gdn_a_log,
        gdn_dt_bias=gdn_dt_bias, gdn_norm_w=gdn_norm_w, gdn_w_out=gdn_w_out, mla_w_a=mla_w_a,
        mla_q_a_norm=mla_q_a_norm, mla_w_q_b=mla_w_q_b, mla_kv_a_norm=mla_kv_a_norm,
        mla_w_kv_b=mla_w_kv_b, mla_w_o=mla_w_o, ffn_w_in=ffn_w_in, ffn_w_out=ffn_w_out))
    return _trunk(x_prompt, w, TILES), _trunk(x_sample, w, TILES)
```

```python
import functools
import math

import jax
import jax.numpy as jnp
from jax import lax
from jax.experimental import pallas as pl
from jax.experimental.pallas import tpu as pltpu

F32 = jnp.float32
BF16 = jnp.bfloat16

RMS_EPS = 1e-6
L2_EPS = 1e-6
LANES = 128
BF16_ROWS = 16

GDN_QK_HEADS = 16
GDN_V_HEADS = 32
GDN_DK = 128
GDN_CONV = 5
GDN_CHUNK = 64
GDN_Q_DIM = GDN_QK_HEADS * GDN_DK
GDN_V_DIM = GDN_V_HEADS * GDN_DK
GDN_CONV_DIM = 2 * GDN_Q_DIM + GDN_V_DIM
GDN_MAIN_DIM = GDN_CONV_DIM + GDN_V_DIM
GDN_GATE_LANES = 4 * GDN_V_HEADS

MLA_HEADS = 16
MLA_Q_RANK = 768
MLA_KV_RANK = 512
MLA_D_NOPE = 128
MLA_D_ROPE = 64
MLA_D_V = 128
MLA_D_QK = MLA_D_NOPE + MLA_D_ROPE
MLA_D_PAD = 2 * LANES
MLA_V_ROWS = MLA_D_V + BF16_ROWS
ROPE_THETA = 10000.0

VMEM_LIMIT = 56 * 1024 * 1024


def _params(sem):
    return pltpu.CompilerParams(dimension_semantics=sem, vmem_limit_bytes=VMEM_LIMIT)


def _resident(shape, index_map):
    return pl.BlockSpec(shape, index_map, pipeline_mode=pl.Buffered(1))


def _rms(x, w):
    return x * lax.rsqrt(jnp.mean(x * x, axis=-1, keepdims=True) + RMS_EPS) * w


def _dot(a, b):
    return jnp.dot(a, b, preferred_element_type=F32)


def _dot_nt(a, b):
    return lax.dot_general(a, b, (((1,), (1,)), ((), ())), preferred_element_type=F32)


def _dot_tn(a, b):
    return lax.dot_general(a, b, (((0,), (0,)), ((), ())), preferred_element_type=F32)


def _split3(x):
    hi = x.astype(BF16)
    r1 = x - hi.astype(F32)
    mid = r1.astype(BF16)
    lo = (r1 - mid.astype(F32)).astype(BF16)
    return hi, mid, lo


def _silu(x):
    return x * (1.0 / (1.0 + jnp.exp(-x)))


def _gdn_proj_kernel(x_ref, nw_ref, w_ref, wg_ref, wgt_ref, o_ref, g_ref, gt_ref, xn_ref):
    @pl.when(pl.program_id(1) == 0)
    def _():
        xn = _rms(x_ref[...], nw_ref[...]).astype(BF16)
        xn_ref[...] = xn
        g_ref[...] = _dot(xn, wg_ref[...])
        gt_ref[...] = _dot_nt(wgt_ref[...], xn)

    o_ref[...] = _dot(xn_ref[...], w_ref[...]).astype(BF16)


def _gdn_proj(x, nw, w, wg, wgt, tm, tn):
    T, D = x.shape
    N = w.shape[1]
    return pl.pallas_call(
        _gdn_proj_kernel,
        grid=(T // tm, N // tn),
        in_specs=[
            pl.BlockSpec((tm, D), lambda i, j: (i, 0)),
            pl.BlockSpec((1, D), lambda i, j: (0, 0)),
            pl.BlockSpec((D, tn), lambda i, j: (0, j)),
            pl.BlockSpec((D, LANES), lambda i, j: (0, 0)),
            pl.BlockSpec((LANES, D), lambda i, j: (0, 0)),
        ],
        out_specs=[
            pl.BlockSpec((tm, tn), lambda i, j: (i, j)),
            pl.BlockSpec((tm, LANES), lambda i, j: (i, 0)),
            pl.BlockSpec((LANES, tm), lambda i, j: (0, i)),
        ],
        out_shape=[
            jax.ShapeDtypeStruct((T, N), BF16),
            jax.ShapeDtypeStruct((T, LANES), F32),
            jax.ShapeDtypeStruct((LANES, T), F32),
        ],
        scratch_shapes=[pltpu.VMEM((tm, D), BF16)],
        compiler_params=_params(("parallel", "arbitrary")),
        name="gdn_proj",
    )(x, nw, w, wg, wgt)


def _softplus(y):
    return jnp.maximum(y, 0.0) + jnp.log1p(jnp.exp(-jnp.abs(y)))


def _gate_values(x, neg_a, dt):
    beta = 1.0 / (1.0 + jnp.exp(-x))
    g = neg_a * _softplus(x + dt)
    return beta, g


def _gdn_gates_kernel(x_ref, xt_ref, na_ref, dt_ref, nat_ref, dtt_ref, o_ref, ot_ref):
    R = x_ref.shape[0]
    ii = lax.broadcasted_iota(jnp.int32, (R, R), 0)
    jj = lax.broadcasted_iota(jnp.int32, (R, R), 1)
    same = (ii // GDN_CHUNK) == (jj // GDN_CHUNK)
    lower = jnp.where(same & (jj <= ii), 1.0, 0.0).astype(BF16)
    upper = jnp.where(same & (jj >= ii), 1.0, 0.0).astype(BF16)

    beta, g = _gate_values(x_ref[...], na_ref[...], dt_ref[...])
    parts = _split3(g)
    cf = sum(_dot(lower, p) for p in parts)
    cb = sum(_dot(upper, p) for p in parts)
    e = lax.broadcasted_iota(jnp.int32, (R, LANES), 1) % 8
    o_ref[...] = jnp.where(e < 4, beta, jnp.where(e < 6, cf, cb))

    beta_t, g_t = _gate_values(xt_ref[...], nat_ref[...], dtt_ref[...])
    parts_t = _split3(g_t)
    cf_t = sum(_dot(p, upper) for p in parts_t)
    cb_t = sum(_dot(p, lower) for p in parts_t)
    e_t = lax.broadcasted_iota(jnp.int32, (LANES, R), 0) % 8
    ot_ref[...] = jnp.where(e_t < 4, beta_t, jnp.where(e_t < 6, cf_t, cb_t))


def _gdn_gates(gates, gates_t, neg_a, dt, tr):
    T = gates.shape[0]
    row = lambda i: (i, 0)
    col = lambda i: (0, i)
    fixed = lambda i: (0, 0)
    return pl.pallas_call(
        _gdn_gates_kernel,
        grid=(T // tr,),
        in_specs=[
            pl.BlockSpec((tr, LANES), row),
            pl.BlockSpec((LANES, tr), col),
            pl.BlockSpec((1, LANES), fixed),
            pl.BlockSpec((1, LANES), fixed),
            pl.BlockSpec((LANES, 1), fixed),
            pl.BlockSpec((LANES, 1), fixed),
        ],
        out_specs=[pl.BlockSpec((tr, LANES), row), pl.BlockSpec((LANES, tr), col)],
        out_shape=[jax.ShapeDtypeStruct((T, LANES), F32), jax.ShapeDtypeStruct((LANES, T), F32)],
        compiler_params=_params(("parallel",)),
        name="gdn_gates",
    )(gates, gates_t, neg_a.reshape(1, LANES), dt.reshape(1, LANES),
      neg_a.reshape(LANES, 1), dt.reshape(LANES, 1))


CONV_HALO = BF16_ROWS
CONV_PAD = 8


def _gdn_conv_kernel(blocks_per_seq, n_q_blocks, n_qk_blocks,
                     prev_ref, main_ref, next_ref, w_ref, o_ref, xs_ref):
    i = pl.program_id(0)
    j = pl.program_id(1)
    tr, tc = main_ref.shape
    r = GDN_CONV // 2
    pos = i % blocks_per_seq
    keep_prev = jnp.where(pos == 0, 0.0, 1.0)
    keep_next = jnp.where(pos == blocks_per_seq - 1, 0.0, 1.0)
    xs_ref[CONV_PAD - r:CONV_PAD, :] = prev_ref[CONV_HALO - r:, :].astype(F32) * keep_prev
    xs_ref[CONV_PAD:CONV_PAD + tr, :] = main_ref[...].astype(F32)
    xs_ref[CONV_PAD + tr:CONV_PAD + tr + r, :] = next_ref[:r, :].astype(F32) * keep_next
    w = w_ref[...]
    acc = xs_ref[CONV_PAD - r:CONV_PAD - r + tr, :] * w[0:1, :]
    for t in range(1, GDN_CONV):
        acc = acc + xs_ref[CONV_PAD - r + t:CONV_PAD - r + t + tr, :] * w[t:t + 1, :]
    y = _silu(acc)

    @pl.when(j >= n_qk_blocks)
    def _():
        o_ref[...] = y.astype(BF16)

    @pl.when(j < n_qk_blocks)
    def _():
        q_scale = jnp.where(j < n_q_blocks, GDN_DK ** -0.5, 1.0)
        for h in range(tc // LANES):
            blk = y[:, h * LANES:(h + 1) * LANES]
            inv = lax.rsqrt(jnp.sum(blk * blk, axis=-1, keepdims=True) + L2_EPS) * q_scale
            o_ref[:, h * LANES:(h + 1) * LANES] = (blk * inv).astype(BF16)


def _gdn_conv(pre, conv_w, seq_len, tr, tc):
    T = pre.shape[0]
    C = conv_w.shape[1]
    hb = tr // CONV_HALO
    n_halo = T // CONV_HALO
    kern = functools.partial(_gdn_conv_kernel, seq_len // tr, GDN_Q_DIM // tc, 2 * GDN_Q_DIM // tc)
    return pl.pallas_call(
        kern,
        grid=(T // tr, C // tc),
        in_specs=[
            pl.BlockSpec((CONV_HALO, tc), lambda i, j: (jnp.maximum(i * hb - 1, 0), j)),
            pl.BlockSpec((tr, tc), lambda i, j: (i, j)),
            pl.BlockSpec((CONV_HALO, tc), lambda i, j: (jnp.minimum((i + 1) * hb, n_halo - 1), j)),
            pl.BlockSpec((GDN_CONV, tc), lambda i, j: (0, j)),
        ],
        out_specs=pl.BlockSpec((tr, tc), lambda i, j: (i, j)),
        out_shape=jax.ShapeDtypeStruct((T, C), BF16),
        scratch_shapes=[pltpu.VMEM((tr + 2 * CONV_PAD, tc), F32)],
        compiler_params=_params(("parallel", "parallel")),
        name="gdn_conv",
    )(pre, pre, pre, conv_w)


NEUMANN_DOUBLINGS = 5


def _gate_lane(direction, v_head, decay):
    return (4 if decay else 0) + 2 * direction + v_head


def _delta_solve_kernel(chunks, q_ref, k_ref, v_ref, g_ref, gt_ref,
                        uf_ref, wf_ref, pf_ref, qf_ref, kf_ref, ub_ref, wb_ref, pb_ref, qb_ref, kb_ref):
    C = GDN_CHUNK
    qh = pl.program_id(1)
    gsel = pltpu.roll(g_ref[...], (LANES - 8 * qh) % LANES, axis=1)
    ii = lax.broadcasted_iota(jnp.int32, (C, C), 0)
    jj = lax.broadcasted_iota(jnp.int32, (C, C), 1)
    eye = jnp.where(ii == jj, 1.0, 0.0)
    masks = ((ii >= jj, ii > jj), (ii <= jj, ii < jj))
    outs = ((uf_ref, wf_ref, pf_ref, qf_ref, kf_ref), (ub_ref, wb_ref, pb_ref, qb_ref, kb_ref))
    eye_k = jnp.where(lax.broadcasted_iota(jnp.int32, (GDN_DK, GDN_DK), 0)
                      == lax.broadcasted_iota(jnp.int32, (GDN_DK, GDN_DK), 1), 1.0, 0.0).astype(BF16)

    kqs = []
    for ci in range(chunks):
        rows = slice(ci * C, (ci + 1) * C)
        kc = k_ref[rows, :]
        kqs.append(_dot_nt(jnp.concatenate([kc, q_ref[rows, :], eye_k], axis=0), kc))

    chains = [(ci, d, hv) for ci in range(chunks) for d in range(2) for hv in range(2)]
    ts, xs, rhs = [], [], []
    for ci, d, hv in chains:
        rows = slice(ci * C, (ci + 1) * C)
        cols = slice(hv * LANES, (hv + 1) * LANES)
        incl, strict = masks[d]
        lb = _gate_lane(d, hv, False)
        lg = _gate_lane(d, hv, True)
        beta = jnp.broadcast_to(gsel[rows, lb:lb + 1], (C, LANES))
        gcc = jnp.broadcast_to(gsel[rows, lg:lg + 1], (C, LANES))
        gcr = gt_ref[lg:lg + 1, rows]
        gl = gcr[:, 0:1] if d else gcr[:, C - 1:C]
        decay = jnp.where(incl, jnp.exp(jnp.where(incl, gcc[:, :C] - gcr, 0.0)), 0.0)
        a = jnp.where(strict, beta[:, :C] * kqs[ci][:C] * decay, 0.0)
        ts.append(eye - a)
        xs.append(a)
        eg = jnp.exp(gcc)
        u_ref, w_ref, p_ref, qd_ref, kdt_ref = outs[d]
        p_ref[rows, hv * C:(hv + 1) * C] = (kqs[ci][C:2 * C] * decay).astype(BF16)
        qd_ref[rows, cols] = (q_ref[rows, :].astype(F32) * eg).astype(BF16)
        kdt_ref[cols, rows] = (kqs[ci][2 * C:] * jnp.exp(gl - gcr)).astype(BF16)
        vb = v_ref[rows, cols].astype(F32) * beta
        kb = k_ref[rows, :].astype(F32) * (beta * eg)
        rhs.append(jnp.concatenate([vb, kb], axis=1).astype(BF16))

    xs = [x.astype(BF16) for x in xs]
    xs = [_dot(x, x).astype(BF16) for x in xs]
    for step in range(NEUMANN_DOUBLINGS):
        ts = [t + _dot(t.astype(BF16), x) for t, x in zip(ts, xs)]
        if step + 1 < NEUMANN_DOUBLINGS:
            xs = [_dot(x, x).astype(BF16) for x in xs]

    uws = [_dot(t.astype(BF16), r) for t, r in zip(ts, rhs)]
    for (ci, d, hv), uw in zip(chains, uws):
        rows = slice(ci * C, (ci + 1) * C)
        cols = slice(hv * LANES, (hv + 1) * LANES)
        outs[d][0][rows, cols] = uw[:, :LANES].astype(BF16)
        outs[d][1][rows, cols] = uw[:, LANES:].astype(BF16)


def _gdn_delta_solve(qkv, gsum, gsum_t, chunks):
    T = qkv.shape[0]
    R = chunks * GDN_CHUNK
    k_block0 = GDN_Q_DIM // LANES
    v_block0 = 2 * GDN_Q_DIM // (2 * LANES)
    wide = pl.BlockSpec((R, 2 * LANES), lambda i, h: (i, h))
    narrow = pl.BlockSpec((R, LANES), lambda i, h: (i, h))
    tall = pl.BlockSpec((2 * GDN_DK, R), lambda i, h: (h, i))
    uw_shape = jax.ShapeDtypeStruct((T, GDN_V_DIM), BF16)
    p_shape = jax.ShapeDtypeStruct((T, GDN_V_HEADS * GDN_CHUNK), BF16)
    kt_shape = jax.ShapeDtypeStruct((GDN_V_HEADS * GDN_DK, T), BF16)
    return pl.pallas_call(
        functools.partial(_delta_solve_kernel, chunks),
        grid=(T // R, GDN_QK_HEADS),
        in_specs=[
            narrow,
            pl.BlockSpec((R, LANES), lambda i, h: (i, k_block0 + h)),
            pl.BlockSpec((R, 2 * LANES), lambda i, h: (i, v_block0 + h)),
            pl.BlockSpec((R, LANES), lambda i, h: (i, 0)),
            pl.BlockSpec((8, R), lambda i, h: (h, i)),
        ],
        out_specs=[wide, wide, narrow, wide, tall] * 2,
        out_shape=[uw_shape, uw_shape, p_shape, uw_shape, kt_shape] * 2,
        compiler_params=_params(("parallel", "parallel")),
        name="gdn_delta_solve",
    )(qkv, qkv, qkv, gsum, gsum_t)


def _delta_sweep_kernel(rev, chunks, heads, *refs):
    if rev:
        u_ref, w_ref, p_ref, qd_ref, kdt_ref, gt_ref, of_ref, z_ref, nw_ref, o_ref, s_ref = refs
    else:
        u_ref, w_ref, p_ref, qd_ref, kdt_ref, gt_ref, o_ref, s_ref = refs
    C = GDN_CHUNK
    d = 1 if rev else 0

    @pl.when(pl.program_id(2) == 0)
    def _():
        s_ref[...] = jnp.zeros_like(s_ref)

    states = range(2 * heads)
    zeros = jnp.zeros((C, GDN_DK), BF16)
    pad = lambda v, half: jnp.concatenate([zeros, v] if half else [v, zeros], axis=0)
    order = range(chunks - 1, -1, -1) if rev else range(chunks)
    for ci in order:
        rows = slice(ci * C, (ci + 1) * C)
        pair = slice((ci // 2) * 2 * C, (ci // 2 + 1) * 2 * C)
        last = ci * C + (0 if rev else C - 1)
        cols = [slice(h * LANES, (h + 1) * LANES) for h in states]
        s_old = [s_ref[h] for h in states]
        wqs = [_dot(jnp.concatenate([w_ref[rows, cols[h]], qd_ref[rows, cols[h]]], axis=0),
                    s_old[h].astype(BF16)) for h in states]
        vns = [(u_ref[rows, cols[h]].astype(F32) - wqs[h][:C]).astype(BF16) for h in states]
        pvs = [_dot(p_ref[rows, (h // 2) * LANES:(h // 2 + 1) * LANES], pad(vns[h], h % 2)) for h in states]
        kvs = [_dot(kdt_ref[cols[h], pair], pad(vns[h], ci % 2)) for h in states]
        for h in states:
            lg = 8 * (h // 2) + _gate_lane(d, h % 2, True)
            s_ref[h] = s_old[h] * jnp.exp(gt_ref[lg:lg + 1, last:last + 1]) + kvs[h]
            o = wqs[h][C:] + pvs[h]
            if rev:
                tot = of_ref[rows, cols[h]] + o
                gate = _silu(z_ref[rows, cols[h]].astype(F32))
                o_ref[rows, cols[h]] = (_rms(tot, nw_ref[...]) * gate).astype(BF16)
            else:
                o_ref[rows, cols[h]] = o


def _gdn_delta_sweep(rev, u, w, p, qd, kdt, gsum_t, batch, chunks, heads, extra=None):
    T = u.shape[0]
    R = chunks * GDN_CHUNK
    nb = T // batch // R
    assert chunks % 2 == 0
    rows = (lambda b, n: b * nb + (nb - 1 - n)) if rev else (lambda b, n: b * nb + n)
    qk_w = heads * LANES
    v_w = 2 * heads * LANES
    wide = pl.BlockSpec((R, v_w), lambda b, h, n: (rows(b, n), h))
    narrow = pl.BlockSpec((R, qk_w), lambda b, h, n: (rows(b, n), h))
    in_specs = [
        wide, wide, narrow, wide,
        pl.BlockSpec((v_w, R), lambda b, h, n: (h, rows(b, n))),
        pl.BlockSpec((8 * heads, R), lambda b, h, n: (h, rows(b, n))),
    ]
    args = [u, w, p, qd, kdt, gsum_t]
    if rev:
        o_fwd, pre, norm_w = extra
        z_block0 = GDN_CONV_DIM // v_w
        in_specs += [
            wide,
            pl.BlockSpec((R, v_w), lambda b, h, n: (rows(b, n), z_block0 + h)),
            pl.BlockSpec((1, LANES), lambda b, h, n: (0, 0)),
        ]
        args += [o_fwd, pre, norm_w]
    return pl.pallas_call(
        functools.partial(_delta_sweep_kernel, rev, chunks, heads),
        grid=(batch, GDN_QK_HEADS // heads, nb),
        in_specs=in_specs,
        out_specs=wide,
        out_shape=jax.ShapeDtypeStruct((T, GDN_V_DIM), BF16 if rev else F32),
        scratch_shapes=[pltpu.VMEM((2 * heads, GDN_DK, GDN_DK), F32)],
        compiler_params=_params(("parallel", "parallel", "arbitrary")),
        name="gdn_sweep_bwd" if rev else "gdn_sweep_fwd",
    )(*args)


def _out_proj_kernel(transposed, a_ref, w_ref, nw_ref, x_ref, o_ref):
    m = _dot_tn(a_ref[...], w_ref[...]) if transposed else _dot(a_ref[...], w_ref[...])
    o_ref[...] = x_ref[...] + _rms(m, nw_ref[...])


def _out_proj(a, w, nw, x, tm, transposed=False):
    K, D = w.shape
    T = x.shape[0]
    a_spec = pl.BlockSpec((K, tm), lambda i: (0, i)) if transposed else pl.BlockSpec((tm, K), lambda i: (i, 0))
    return pl.pallas_call(
        functools.partial(_out_proj_kernel, transposed),
        grid=(T // tm,),
        in_specs=[
            a_spec,
            _resident((K, D), lambda i: (0, 0)),
            pl.BlockSpec((1, D), lambda i: (0, 0)),
            pl.BlockSpec((tm, D), lambda i: (i, 0)),
        ],
        out_specs=pl.BlockSpec((tm, D), lambda i: (i, 0)),
        out_shape=jax.ShapeDtypeStruct((T, D), F32),
        compiler_params=_params(("parallel",)),
        name="out_proj_t" if transposed else "out_proj",
    )(a, w, nw, x)


def _ffn_kernel(x_ref, nw1_ref, w1_ref, w2_ref, nw2_ref, o_ref, xn_ref, acc_ref):
    j = pl.program_id(1)

    @pl.when(j == 0)
    def _():
        xn_ref[...] = _rms(x_ref[...], nw1_ref[...]).astype(BF16)
        acc_ref[...] = jnp.zeros_like(acc_ref)

    h = jnp.maximum(_dot(xn_ref[...], w1_ref[...]), 0.0)
    acc_ref[...] += _dot((h * h).astype(BF16), w2_ref[...])

    @pl.when(j == pl.num_programs(1) - 1)
    def _():
        o_ref[...] = x_ref[...] + _rms(acc_ref[...], nw2_ref[...])


def _ffn(x, nw1, w1, w2, nw2, tm, tf):
    T, D = x.shape
    Fd = w1.shape[1]
    return pl.pallas_call(
        _ffn_kernel,
        grid=(T // tm, Fd // tf),
        in_specs=[
            pl.BlockSpec((tm, D), lambda i, j: (i, 0)),
            pl.BlockSpec((1, D), lambda i, j: (0, 0)),
            pl.BlockSpec((D, tf), lambda i, j: (0, j)),
            pl.BlockSpec((tf, D), lambda i, j: (j, 0)),
            pl.BlockSpec((1, D), lambda i, j: (0, 0)),
        ],
        out_specs=pl.BlockSpec((tm, D), lambda i, j: (i, 0)),
        out_shape=jax.ShapeDtypeStruct((T, D), F32),
        scratch_shapes=[pltpu.VMEM((tm, D), BF16), pltpu.VMEM((tm, D), F32)],
        compiler_params=_params(("parallel", "arbitrary")),
        name="ffn",
    )(x, nw1, w1, w2, nw2)


def _mla_proj_kernel(x_ref, nw_ref, wa_ref, qn_ref, kvn_ref, wqt_ref, wkn_ref, wvt_ref,
                     cos_ref, sin_ref, cost_ref, sint_ref, qt_ref, k_ref, vt_ref):
    xn = _rms(x_ref[...], nw_ref[...]).astype(BF16)
    a = _dot(xn, wa_ref[...])
    cq = _rms(a[:, :MLA_Q_RANK], qn_ref[...]).astype(BF16)
    ckv = _rms(a[:, MLA_Q_RANK:MLA_Q_RANK + MLA_KV_RANK], kvn_ref[...]).astype(BF16)

    kr = a[:, MLA_Q_RANK + MLA_KV_RANK:]
    k_rope = (kr * cos_ref[...] + pltpu.roll(kr, LANES // 2, axis=1) * sin_ref[...]).astype(BF16)
    kn = _dot(ckv, wkn_ref[...])
    for h in range(MLA_HEADS):
        c0 = h * MLA_D_PAD
        k_ref[:, c0:c0 + LANES] = kn[:, h * LANES:(h + 1) * LANES].astype(BF16)
        k_ref[:, c0 + LANES:c0 + 2 * LANES] = k_rope

    scale = MLA_D_QK ** -0.5 * math.log2(math.e)
    qt = _dot_nt(wqt_ref[...], cq)
    cost = cost_ref[...]
    sint = sint_ref[...]
    half = LANES // 2
    for h in range(MLA_HEADS):
        r0 = h * MLA_D_PAD
        qt_ref[r0:r0 + LANES, :] = (qt[r0:r0 + LANES] * scale).astype(BF16)
        blk = qt[r0 + LANES:r0 + 2 * LANES]
        swapped = jnp.concatenate([blk[half:], blk[:half]], axis=0)
        qt_ref[r0 + LANES:r0 + 2 * LANES, :] = ((blk * cost + swapped * sint) * scale).astype(BF16)

    vt = _dot_nt(wvt_ref[...], ckv)
    ones = jnp.ones((MLA_V_ROWS - MLA_D_V, vt.shape[1]), BF16)
    for h in range(MLA_HEADS):
        r0 = h * MLA_V_ROWS
        vt_ref[r0:r0 + MLA_D_V, :] = vt[h * MLA_D_V:(h + 1) * MLA_D_V].astype(BF16)
        vt_ref[r0 + MLA_D_V:r0 + MLA_V_ROWS, :] = ones


def _mla_proj(x, nw, wa, qn, kvn, wqt, wkn, wvt, cos, sin, seq_len, tm):
    T, D = x.shape
    pos_blocks = seq_len // tm
    row = lambda i: (i, 0)
    col = lambda i: (0, i)
    fixed = lambda i: (0, 0)
    return pl.pallas_call(
        _mla_proj_kernel,
        grid=(T // tm,),
        in_specs=[
            pl.BlockSpec((tm, D), row),
            pl.BlockSpec((1, D), fixed),
            _resident(wa.shape, fixed),
            pl.BlockSpec((1, MLA_Q_RANK), fixed),
            pl.BlockSpec((1, MLA_KV_RANK), fixed),
            _resident(wqt.shape, fixed),
            _resident(wkn.shape, fixed),
            _resident(wvt.shape, fixed),
            pl.BlockSpec((tm, LANES), lambda i: (i % pos_blocks, 0)),
            pl.BlockSpec((tm, LANES), lambda i: (i % pos_blocks, 0)),
            pl.BlockSpec((LANES, tm), lambda i: (0, i % pos_blocks)),
            pl.BlockSpec((LANES, tm), lambda i: (0, i % pos_blocks)),
        ],
        out_specs=[
            pl.BlockSpec((MLA_HEADS * MLA_D_PAD, tm), col),
            pl.BlockSpec((tm, MLA_HEADS * MLA_D_PAD), row),
            pl.BlockSpec((MLA_HEADS * MLA_V_ROWS, tm), col),
        ],
        out_shape=[
            jax.ShapeDtypeStruct((MLA_HEADS * MLA_D_PAD, T), BF16),
            jax.ShapeDtypeStruct((T, MLA_HEADS * MLA_D_PAD), BF16),
            jax.ShapeDtypeStruct((MLA_HEADS * MLA_V_ROWS, T), BF16),
        ],
        compiler_params=_params(("parallel",)),
        name="mla_proj",
    )(x, nw, wa, qn, kvn, wqt, wkn, wvt, cos, sin, cos.T, sin.T)


ATTN_SOFTMAX_ROWS = 64
ATTN_SCORE_SLOTS = 4


def _attn_kernel(tk, unroll, qt_ref, k_ref, vt_ref, ot_ref, s_ref, p_ref, acc_ref):
    tq = qt_ref.shape[1]
    n = k_ref.shape[0] // tk
    R = ATTN_SOFTMAX_ROWS

    def chunk(t):
        return pl.ds(t * tk if isinstance(t, int) else pl.multiple_of(t * tk, tk), tk)

    def scores(t, slot):
        s_ref[slot] = _dot(k_ref[chunk(t), :], qt_ref[...])

    def weighted(t, slot, a):
        acc_ref[...] = a * acc_ref[...] + _dot(vt_ref[:, chunk(t)], p_ref[slot])

    def softmax(slot, m, p_slot):
        mx = s_ref[slot, 0:R, :]
        for r in range(R, tk, R):
            mx = jnp.maximum(mx, s_ref[slot, r:r + R, :])
        m_new = jnp.maximum(m, jnp.max(mx, axis=0, keepdims=True))
        for r in range(0, tk, R):
            p_ref[p_slot, r:r + R, :] = jnp.exp2(s_ref[slot, r:r + R, :] - m_new).astype(BF16)
        return m_new, jnp.exp2(m - m_new)

    def trip(t, phase, m, a, prefetch=True):
        if prefetch:
            scores(t + 2, (phase + 2) % ATTN_SCORE_SLOTS)
        weighted(t - 1, (phase - 1) % 2, a)
        return softmax(phase % ATTN_SCORE_SLOTS, m, phase % 2)

    def body(i, carry):
        m, a = carry
        t0 = 1 + unroll * i
        for j in range(unroll):
            m, a = trip(t0 + j, 1 + j, m, a)
        return m, a

    acc_ref[...] = jnp.zeros_like(acc_ref)
    scores(0, 0)
    scores(1, 1)
    m, a = softmax(0, jnp.full((1, tq), -jnp.inf, F32), 0)
    scores(2, 2)
    bodies = (n - 3) // unroll
    m, a = lax.fori_loop(0, bodies, body, (m, a))
    for t in range(1 + bodies * unroll, n):
        m, a = trip(t, t % ATTN_SCORE_SLOTS, m, a, prefetch=t + 2 < n)
    weighted(n - 1, (n - 1) % 2, a)
    acc = acc_ref[...]
    ot_ref[...] = (acc[:MLA_D_V] / acc[MLA_D_V:MLA_D_V + 1]).astype(BF16)


def _attention(qt, k, vt, batch, tq, tk, unroll):
    T = k.shape[0]
    L = T // batch
    nq = L // tq
    assert unroll % ATTN_SCORE_SLOTS == 0 and L // tk >= 3
    return pl.pallas_call(
        functools.partial(_attn_kernel, tk, unroll),
        grid=(batch, MLA_HEADS, nq),
        in_specs=[
            pl.BlockSpec((MLA_D_PAD, tq), lambda b, h, i: (h, b * nq + i)),
            pl.BlockSpec((L, MLA_D_PAD), lambda b, h, i: (b, h)),
            pl.BlockSpec((MLA_V_ROWS, L), lambda b, h, i: (h, b)),
        ],
        out_specs=pl.BlockSpec((MLA_D_V, tq), lambda b, h, i: (h, b * nq + i)),
        out_shape=jax.ShapeDtypeStruct((MLA_HEADS * MLA_D_V, T), BF16),
        scratch_shapes=[pltpu.VMEM((ATTN_SCORE_SLOTS, tk, tq), F32), pltpu.VMEM((2, tk, tq), BF16),
                        pltpu.VMEM((MLA_V_ROWS, tq), F32)],
        compiler_params=_params(("parallel", "parallel", "arbitrary")),
        name="attention",
    )(qt, k, vt)


def _gate_lane_perm():
    lanes = jnp.arange(GDN_GATE_LANES)
    q, e = lanes // 8, lanes % 8
    return GDN_V_HEADS * (e // 2) + 2 * q + e % 2


def _gate_lane_params(a_log, dt_bias):
    lanes = jnp.arange(GDN_GATE_LANES)
    q, e = lanes // 8, lanes % 8
    head = 2 * q + e % 2
    direction = jnp.maximum(e // 2 - 2, 0)
    is_decay = e >= 4
    neg_a = jnp.where(is_decay, -jnp.exp(a_log.astype(F32))[direction, head], 0.0)
    dt = jnp.where(is_decay, dt_bias.astype(F32)[direction, head], 0.0)
    return neg_a, dt


def _pad_rope_cols(w):
    half = MLA_D_ROPE // 2
    z = jnp.zeros(w.shape[:-1] + (half,), w.dtype)
    return jnp.concatenate([w[..., :half], z, w[..., half:], z], axis=-1)


def _rope_tables(L):
    half = MLA_D_ROPE // 2
    inv_freq = ROPE_THETA ** (-jnp.arange(half, dtype=F32) / half)
    ang = jnp.arange(L, dtype=F32)[:, None] * inv_freq[None, :]
    c, s = jnp.cos(ang), jnp.sin(ang)
    z = jnp.zeros_like(c)
    return jnp.concatenate([c, z, c, z], axis=-1), jnp.concatenate([-s, z, s, z], axis=-1)


def _prepare(p):
    w = {}
    g_in = p['gdn_w_in'][0]
    w['gdn_main'] = g_in[:, :GDN_MAIN_DIM].astype(BF16)
    wg = g_in[:, GDN_MAIN_DIM:][:, _gate_lane_perm()].astype(BF16)
    w['gdn_gate'] = wg
    w['gdn_gate_t'] = wg.T
    w['gdn_neg_a'], w['gdn_dt'] = _gate_lane_params(p['gdn_a_log'][0], p['gdn_dt_bias'][0])
    w['gdn_conv'] = p['gdn_conv_w'][0].astype(F32)
    w['gdn_norm'] = p['gdn_norm_w'][0].reshape(1, GDN_DK).astype(F32)
    w['gdn_out'] = p['gdn_w_out'][0].astype(BF16)

    wa = p['mla_w_a'][0]
    rank = MLA_Q_RANK + MLA_KV_RANK
    w['mla_a'] = jnp.concatenate([wa[:, :rank], _pad_rope_cols(wa[:, rank:])], axis=-1).astype(BF16)
    wq = p['mla_w_q_b'][0].reshape(MLA_Q_RANK, MLA_HEADS, MLA_D_QK)
    wq = jnp.concatenate([wq[..., :MLA_D_NOPE], _pad_rope_cols(wq[..., MLA_D_NOPE:])], axis=-1)
    w['mla_q_t'] = wq.reshape(MLA_Q_RANK, MLA_HEADS * MLA_D_PAD).astype(BF16).T
    wkv = p['mla_w_kv_b'][0].reshape(MLA_KV_RANK, MLA_HEADS, MLA_D_NOPE + MLA_D_V)
    w['mla_kn'] = wkv[..., :MLA_D_NOPE].reshape(MLA_KV_RANK, MLA_HEADS * MLA_D_NOPE).astype(BF16)
    w['mla_v_t'] = wkv[..., MLA_D_NOPE:].reshape(MLA_KV_RANK, MLA_HEADS * MLA_D_V).astype(BF16).T
    w['mla_qn'] = p['mla_q_a_norm'][0].reshape(1, MLA_Q_RANK).astype(F32)
    w['mla_kvn'] = p['mla_kv_a_norm'][0].reshape(1, MLA_KV_RANK).astype(F32)
    w['mla_o'] = p['mla_w_o'][0].astype(BF16)

    w['ffn_in'] = p['ffn_w_in'].astype(BF16)
    w['ffn_out'] = p['ffn_w_out'].astype(BF16)
    for name in ('norm_mix_pre', 'norm_mix_post', 'norm_ffn_pre', 'norm_ffn_post'):
        w[name] = p[name].astype(F32)[:, None, :]
    return w


TILES = dict(
    proj_tm=512, proj_tn=1024,
    gate_tr=512,
    conv_tr=512, conv_tc=1024,
    solve_chunks=4,
    sweep_chunks=4, sweep_heads=4,
    out_tm=256,
    ffn_tm=512, ffn_tf=1024,
    mla_tm=256,
    attn_tq=256, attn_tk=512, attn_unroll=32,
)


def _trunk(x3, w, tiles):
    B, L, D = x3.shape
    x = x3.reshape(B * L, D)

    pre, gates, gates_t = _gdn_proj(x, w['norm_mix_pre'][0], w['gdn_main'], w['gdn_gate'], w['gdn_gate_t'],
                                    tiles['proj_tm'], tiles['proj_tn'])
    gsum, gsum_t = _gdn_gates(gates, gates_t, w['gdn_neg_a'], w['gdn_dt'], tiles['gate_tr'])
    qkv = _gdn_conv(pre, w['gdn_conv'], L, tiles['conv_tr'], tiles['conv_tc'])
    solved = _gdn_delta_solve(qkv, gsum, gsum_t, tiles['solve_chunks'])
    sweep = functools.partial(_gdn_delta_sweep, gsum_t=gsum_t, batch=B, chunks=tiles['sweep_chunks'],
                              heads=tiles['sweep_heads'])
    o_fwd = sweep(False, *solved[:5])
    o = sweep(True, *solved[5:], extra=(o_fwd, pre, w['gdn_norm']))
    x = _out_proj(o, w['gdn_out'], w['norm_mix_post'][0], x, tiles['out_tm'])
    x = _ffn(x, w['norm_ffn_pre'][0], w['ffn_in'][0], w['ffn_out'][0], w['norm_ffn_post'][0],
             tiles['ffn_tm'], tiles['ffn_tf'])

    cos, sin = _rope_tables(L)
    qt, k, vt = _mla_proj(x, w['norm_mix_pre'][1], w['mla_a'], w['mla_qn'], w['mla_kvn'], w['mla_q_t'],
                          w['mla_kn'], w['mla_v_t'], cos, sin, L, tiles['mla_tm'])
    ot = _attention(qt, k, vt, B, tiles['attn_tq'], tiles['attn_tk'], tiles['attn_unroll'])
    x = _out_proj(ot, w['mla_o'], w['norm_mix_post'][1], x, tiles['out_tm'], transposed=True)
    x = _ffn(x, w['norm_ffn_pre'][1], w['ffn_in'][1], w['ffn_out'][1], w['norm_ffn_post'][1],
             tiles['ffn_tm'], tiles['ffn_tf'])
    return x.reshape(B, L, D)


def kernel(x_prompt, x_sample, norm_mix_pre, norm_mix_post, norm_ffn_pre, norm_ffn_post, gdn_w_in, gdn_conv_w, gdn_a_log, gdn_dt_bias, gdn_norm_w, gdn_w_out, mla_w_a, mla_q_a_norm, mla_w_q_b, mla_kv_a_norm, mla_w_kv_b, mla_w_o, ffn_w_in, ffn_w_out):
    w = _prepare(dict(
        norm_mix_pre=norm_mix_pre, norm_mix_post=norm_mix_post, norm_ffn_pre=norm_ffn_pre,
        norm_ffn_post=norm_ffn_post, gdn_w_in=gdn_w_in, gdn_conv_w=gdn_conv_w, gdn_a_log=gdn_a_log,
        gdn_dt_bias=gdn_dt_bias, gdn_norm_w=gdn_norm_w, gdn_w_out=gdn_w_out, mla_w_a=mla_w_a,
        mla_q_a_norm=mla_q_a_norm, mla_w_q_b=mla_w_q_b, mla_kv_a_norm=mla_kv_a_norm,
        mla_w_kv_b=mla_w_kv_b, mla_w_o=mla_w_o, ffn_w_in=ffn_w_in, ffn_w_out=ffn_w_out))
    return _trunk(x_prompt, w, TILES), _trunk(x_sample, w, TILES)
```

```python
import functools
import math

import jax
import jax.numpy as jnp
from jax import lax
from jax.experimental import pallas as pl
from jax.experimental.pallas import tpu as pltpu

F32 = jnp.float32
BF16 = jnp.bfloat16

RMS_EPS = 1e-6
L2_EPS = 1e-6
LANES = 128
BF16_ROWS = 16

GDN_QK_HEADS = 16
GDN_V_HEADS = 32
GDN_DK = 128
GDN_CONV = 5
GDN_CHUNK = 64
GDN_Q_DIM = GDN_QK_HEADS * GDN_DK
GDN_V_DIM = GDN_V_HEADS * GDN_DK
GDN_CONV_DIM = 2 * GDN_Q_DIM + GDN_V_DIM
GDN_MAIN_DIM = GDN_CONV_DIM + GDN_V_DIM
GDN_GATE_LANES = 4 * GDN_V_HEADS

MLA_HEADS = 16
MLA_Q_RANK = 768
MLA_KV_RANK = 512
MLA_D_NOPE = 128
MLA_D_ROPE = 64
MLA_D_V = 128
MLA_D_QK = MLA_D_NOPE + MLA_D_ROPE
MLA_D_PAD = 2 * LANES
MLA_V_ROWS = MLA_D_V + BF16_ROWS
ROPE_THETA = 10000.0

VMEM_LIMIT = 56 * 1024 * 1024


def _params(sem):
    return pltpu.CompilerParams(dimension_semantics=sem, vmem_limit_bytes=VMEM_LIMIT)


def _resident(shape, index_map):
    return pl.BlockSpec(shape, index_map, pipeline_mode=pl.Buffered(1))


def _rms(x, w):
    return x * lax.rsqrt(jnp.mean(x * x, axis=-1, keepdims=True) + RMS_EPS) * w


def _dot(a, b):
    return jnp.dot(a, b, preferred_element_type=F32)


def _dot_nt(a, b):
    return lax.dot_general(a, b, (((1,), (1,)), ((), ())), preferred_element_type=F32)


def _dot_tn(a, b):
    return lax.dot_general(a, b, (((0,), (0,)), ((), ())), preferred_element_type=F32)


def _split3(x):
    hi = x.astype(BF16)
    r1 = x - hi.astype(F32)
    mid = r1.astype(BF16)
    lo = (r1 - mid.astype(F32)).astype(BF16)
    return hi, mid, lo


def _silu(x):
    return x * (1.0 / (1.0 + jnp.exp(-x)))


def _gdn_proj_kernel(x_ref, nw_ref, w_ref, wg_ref, wgt_ref, o_ref, g_ref, gt_ref, xn_ref):
    @pl.when(pl.program_id(1) == 0)
    def _():
        xn = _rms(x_ref[...], nw_ref[...]).astype(BF16)
        xn_ref[...] = xn
        g_ref[...] = _dot(xn, wg_ref[...])
        gt_ref[...] = _dot_nt(wgt_ref[...], xn)

    o_ref[...] = _dot(xn_ref[...], w_ref[...]).astype(BF16)


def _gdn_proj(x, nw, w, wg, wgt, tm, tn):
    T, D = x.shape
    N = w.shape[1]
    return pl.pallas_call(
        _gdn_proj_kernel,
        grid=(T // tm, N // tn),
        in_specs=[
            pl.BlockSpec((tm, D), lambda i, j: (i, 0)),
            pl.BlockSpec((1, D), lambda i, j: (0, 0)),
            pl.BlockSpec((D, tn), lambda i, j: (0, j)),
            pl.BlockSpec((D, LANES), lambda i, j: (0, 0)),
            pl.BlockSpec((LANES, D), lambda i, j: (0, 0)),
        ],
        out_specs=[
            pl.BlockSpec((tm, tn), lambda i, j: (i, j)),
            pl.BlockSpec((tm, LANES), lambda i, j: (i, 0)),
            pl.BlockSpec((LANES, tm), lambda i, j: (0, i)),
        ],
        out_shape=[
            jax.ShapeDtypeStruct((T, N), BF16),
            jax.ShapeDtypeStruct((T, LANES), F32),
            jax.ShapeDtypeStruct((LANES, T), F32),
        ],
        scratch_shapes=[pltpu.VMEM((tm, D), BF16)],
        compiler_params=_params(("parallel", "arbitrary")),
        name="gdn_proj",
    )(x, nw, w, wg, wgt)


def _softplus(y):
    return jnp.maximum(y, 0.0) + jnp.log1p(jnp.exp(-jnp.abs(y)))


def _gate_values(x, neg_a, dt):
    beta = 1.0 / (1.0 + jnp.exp(-x))
    g = neg_a * _softplus(x + dt)
    return beta, g


def _gdn_gates_kernel(x_ref, xt_ref, na_ref, dt_ref, nat_ref, dtt_ref, o_ref, ot_ref):
    R = x_ref.shape[0]
    ii = lax.broadcasted_iota(jnp.int32, (R, R), 0)
    jj = lax.broadcasted_iota(jnp.int32, (R, R), 1)
    same = (ii // GDN_CHUNK) == (jj // GDN_CHUNK)
    lower = jnp.where(same & (jj <= ii), 1.0, 0.0).astype(BF16)
    upper = jnp.where(same & (jj >= ii), 1.0, 0.0).astype(BF16)

    beta, g = _gate_values(x_ref[...], na_ref[...], dt_ref[...])
    parts = _split3(g)
    cf = sum(_dot(lower, p) for p in parts)
    cb = sum(_dot(upper, p) for p in parts)
    e = lax.broadcasted_iota(jnp.int32, (R, LANES), 1) % 8
    o_ref[...] = jnp.where(e < 4, beta, jnp.where(e < 6, cf, cb))

    beta_t, g_t = _gate_values(xt_ref[...], nat_ref[...], dtt_ref[...])
    parts_t = _split3(g_t)
    cf_t = sum(_dot(p, upper) for p in parts_t)
    cb_t = sum(_dot(p, lower) for p in parts_t)
    e_t = lax.broadcasted_iota(jnp.int32, (LANES, R), 0) % 8
    ot_ref[...] = jnp.where(e_t < 4, beta_t, jnp.where(e_t < 6, cf_t, cb_t))


def _gdn_gates(gates, gates_t, neg_a, dt, tr):
    T = gates.shape[0]
    row = lambda i: (i, 0)
    col = lambda i: (0, i)
    fixed = lambda i: (0, 0)
    return pl.pallas_call(
        _gdn_gates_kernel,
        grid=(T // tr,),
        in_specs=[
            pl.BlockSpec((tr, LANES), row),
            pl.BlockSpec((LANES, tr), col),
            pl.BlockSpec((1, LANES), fixed),
            pl.BlockSpec((1, LANES), fixed),
            pl.BlockSpec((LANES, 1), fixed),
            pl.BlockSpec((LANES, 1), fixed),
        ],
        out_specs=[pl.BlockSpec((tr, LANES), row), pl.BlockSpec((LANES, tr), col)],
        out_shape=[jax.ShapeDtypeStruct((T, LANES), F32), jax.ShapeDtypeStruct((LANES, T), F32)],
        compiler_params=_params(("parallel",)),
        name="gdn_gates",
    )(gates, gates_t, neg_a.reshape(1, LANES), dt.reshape(1, LANES),
      neg_a.reshape(LANES, 1), dt.reshape(LANES, 1))


CONV_HALO = BF16_ROWS
CONV_PAD = 8


def _gdn_conv_kernel(blocks_per_seq, n_q_blocks, n_qk_blocks,
                     prev_ref, main_ref, next_ref, w_ref, o_ref, xs_ref):
    i = pl.program_id(0)
    j = pl.program_id(1)
    tr, tc = main_ref.shape
    r = GDN_CONV // 2
    pos = i % blocks_per_seq
    keep_prev = jnp.where(pos == 0, 0.0, 1.0)
    keep_next = jnp.where(pos == blocks_per_seq - 1, 0.0, 1.0)
    xs_ref[CONV_PAD - r:CONV_PAD, :] = prev_ref[CONV_HALO - r:, :].astype(F32) * keep_prev
    xs_ref[CONV_PAD:CONV_PAD + tr, :] = main_ref[...].astype(F32)
    xs_ref[CONV_PAD + tr:CONV_PAD + tr + r, :] = next_ref[:r, :].astype(F32) * keep_next
    w = w_ref[...]
    acc = xs_ref[CONV_PAD - r:CONV_PAD - r + tr, :] * w[0:1, :]
    for t in range(1, GDN_CONV):
        acc = acc + xs_ref[CONV_PAD - r + t:CONV_PAD - r + t + tr, :] * w[t:t + 1, :]
    y = _silu(acc)

    @pl.when(j >= n_qk_blocks)
    def _():
        o_ref[...] = y.astype(BF16)

    @pl.when(j < n_qk_blocks)
    def _():
        q_scale = jnp.where(j < n_q_blocks, GDN_DK ** -0.5, 1.0)
        for h in range(tc // LANES):
            blk = y[:, h * LANES:(h + 1) * LANES]
            inv = lax.rsqrt(jnp.sum(blk * blk, axis=-1, keepdims=True) + L2_EPS) * q_scale
            o_ref[:, h * LANES:(h + 1) * LANES] = (blk * inv).astype(BF16)


def _gdn_conv(pre, conv_w, seq_len, tr, tc):
    T = pre.shape[0]
    C = conv_w.shape[1]
    hb = tr // CONV_HALO
    n_halo = T // CONV_HALO
    kern = functools.partial(_gdn_conv_kernel, seq_len // tr, GDN_Q_DIM // tc, 2 * GDN_Q_DIM // tc)
    return pl.pallas_call(
        kern,
        grid=(T // tr, C // tc),
        in_specs=[
            pl.BlockSpec((CONV_HALO, tc), lambda i, j: (jnp.maximum(i * hb - 1, 0), j)),
            pl.BlockSpec((tr, tc), lambda i, j: (i, j)),
            pl.BlockSpec((CONV_HALO, tc), lambda i, j: (jnp.minimum((i + 1) * hb, n_halo - 1), j)),
            pl.BlockSpec((GDN_CONV, tc), lambda i, j: (0, j)),
        ],
        out_specs=pl.BlockSpec((tr, tc), lambda i, j: (i, j)),
        out_shape=jax.ShapeDtypeStruct((T, C), BF16),
        scratch_shapes=[pltpu.VMEM((tr + 2 * CONV_PAD, tc), F32)],
        compiler_params=_params(("parallel", "parallel")),
        name="gdn_conv",
    )(pre, pre, pre, conv_w)


NEUMANN_DOUBLINGS = 5


def _gate_lane(direction, v_head, decay):
    return (4 if decay else 0) + 2 * direction + v_head


def _delta_solve_kernel(chunks, q_ref, k_ref, v_ref, g_ref, gt_ref,
                        uf_ref, wf_ref, pf_ref, qf_ref, kf_ref, ub_ref, wb_ref, pb_ref, qb_ref, kb_ref):
    C = GDN_CHUNK
    W = 4 * C
    qh = pl.program_id(1)
    gsel = pltpu.roll(g_ref[...], (LANES - 8 * qh) % LANES, axis=1)
    ii = lax.broadcasted_iota(jnp.int32, (C, W), 0)
    ll = lax.broadcasted_iota(jnp.int32, (C, W), 1)
    jj = ll % C
    group = ll // C
    ahead = jnp.where(group >= 2, jj - ii, ii - jj)
    incl = ahead >= 0
    strict = ahead > 0
    eye = jnp.where(ii == jj, 1.0, 0.0)
    own = [group == p for p in range(4)]
    low_half = lax.broadcasted_iota(jnp.int32, (C, LANES), 1) < C
    outs = ((uf_ref, wf_ref, pf_ref, qf_ref, kf_ref), (ub_ref, wb_ref, pb_ref, qb_ref, kb_ref))
    eye_k = jnp.where(lax.broadcasted_iota(jnp.int32, (GDN_DK, GDN_DK), 0)
                      == lax.broadcasted_iota(jnp.int32, (GDN_DK, GDN_DK), 1), 1.0, 0.0).astype(BF16)
    zeros = jnp.zeros((C, 2 * LANES), BF16)

    def side_by_side(cols):
        return jnp.concatenate([jnp.where(low_half, cols[0], cols[1]),
                                jnp.where(low_half, cols[2], cols[3])], axis=1)

    def block_diag(x):
        return jnp.concatenate([jnp.where(own[p], x, 0.0).astype(BF16) for p in range(4)], axis=0)

    ts, xs, rhs = [], [], []
    for ci in range(chunks):
        rows = slice(ci * C, (ci + 1) * C)
        kc = k_ref[rows, :]
        qc = q_ref[rows, :]
        kq = _dot_nt(jnp.concatenate([kc, qc, eye_k], axis=0), jnp.concatenate([kc] * 4, axis=0))
        betas = [jnp.broadcast_to(gsel[rows, p:p + 1], (C, LANES)) for p in range(4)]
        gccs = [jnp.broadcast_to(gsel[rows, 4 + p:5 + p], (C, LANES)) for p in range(4)]
        g_pair = gt_ref[:, (ci // 2) * 2 * C:(ci // 2 + 1) * 2 * C]
        g_swap = pltpu.roll(g_pair, C, axis=1)
        on_low, on_high = (g_pair, g_swap) if ci % 2 == 0 else (g_swap, g_pair)
        gcrs = [on_low[4 + p:5 + p, :C] for p in range(4)]
        gcr_rows = [jnp.where(low_half[0:1], on_low[4 + p:5 + p], on_high[5 + p:6 + p]) for p in (0, 2)]
        decay = jnp.concatenate([jnp.where(low_half, gccs[p], gccs[p + 1]) - gcr_rows[p // 2]
                                 for p in (0, 2)], axis=1)
        decay = jnp.where(incl, jnp.exp(jnp.where(incl, decay, 0.0)), 0.0)
        a = jnp.where(strict, side_by_side(betas) * kq[:C] * decay, 0.0)
        ts.append(eye - a)
        xs.append(a)
        pm = (kq[C:2 * C] * decay).astype(BF16)
        pf_ref[rows, :] = pm[:, :2 * C]
        pb_ref[rows, :] = pm[:, 2 * C:]
        kcf = kc.astype(F32)
        qcf = qc.astype(F32)
        for p in range(4):
            d, hv = divmod(p, 2)
            cols = slice(hv * LANES, (hv + 1) * LANES)
            gl = gcrs[p][:, 0:1] if d else gcrs[p][:, C - 1:C]
            eg = jnp.exp(gccs[p])
            outs[d][3][rows, cols] = (qcf * eg).astype(BF16)
            outs[d][4][cols, rows] = (kq[2 * C:, :C] * jnp.exp(gl - gcrs[p])).astype(BF16)
            vb = v_ref[rows, cols].astype(F32) * betas[p]
            kb = kcf * (betas[p] * eg)
            rhs.append(jnp.concatenate([vb, kb], axis=1).astype(BF16))

    xs = [_dot(x.astype(BF16), block_diag(x)) for x in xs]
    for step in range(NEUMANN_DOUBLINGS):
        if step + 1 < NEUMANN_DOUBLINGS:
            both = [_dot(jnp.concatenate([t, x], axis=0).astype(BF16), block_diag(x)) for t, x in zip(ts, xs)]
            ts = [t + b[:C] for t, b in zip(ts, both)]
            xs = [b[C:] for b in both]
        else:
            ts = [t + _dot(t.astype(BF16), block_diag(x)) for t, x in zip(ts, xs)]

    for ci in range(chunks):
        rows = slice(ci * C, (ci + 1) * C)
        for p in range(4):
            d, hv = divmod(p, 2)
            cols = slice(hv * LANES, (hv + 1) * LANES)
            lhs = jnp.where(own[p], ts[ci], 0.0).astype(BF16)
            padded = jnp.concatenate([zeros] * p + [rhs[4 * ci + p]] + [zeros] * (3 - p), axis=0)
            uw = _dot(lhs, padded)
            outs[d][0][rows, cols] = uw[:, :LANES].astype(BF16)
            outs[d][1][rows, cols] = uw[:, LANES:].astype(BF16)


def _gdn_delta_solve(qkv, gsum, gsum_t, chunks):
    T = qkv.shape[0]
    R = chunks * GDN_CHUNK
    k_block0 = GDN_Q_DIM // LANES
    v_block0 = 2 * GDN_Q_DIM // (2 * LANES)
    wide = pl.BlockSpec((R, 2 * LANES), lambda i, h: (i, h))
    narrow = pl.BlockSpec((R, LANES), lambda i, h: (i, h))
    tall = pl.BlockSpec((2 * GDN_DK, R), lambda i, h: (h, i))
    uw_shape = jax.ShapeDtypeStruct((T, GDN_V_DIM), BF16)
    p_shape = jax.ShapeDtypeStruct((T, GDN_V_HEADS * GDN_CHUNK), BF16)
    kt_shape = jax.ShapeDtypeStruct((GDN_V_HEADS * GDN_DK, T), BF16)
    return pl.pallas_call(
        functools.partial(_delta_solve_kernel, chunks),
        grid=(T // R, GDN_QK_HEADS),
        in_specs=[
            narrow,
            pl.BlockSpec((R, LANES), lambda i, h: (i, k_block0 + h)),
            pl.BlockSpec((R, 2 * LANES), lambda i, h: (i, v_block0 + h)),
            pl.BlockSpec((R, LANES), lambda i, h: (i, 0)),
            pl.BlockSpec((8, R), lambda i, h: (h, i)),
        ],
        out_specs=[wide, wide, narrow, wide, tall] * 2,
        out_shape=[uw_shape, uw_shape, p_shape, uw_shape, kt_shape] * 2,
        compiler_params=_params(("parallel", "parallel")),
        name="gdn_delta_solve",
    )(qkv, qkv, qkv, gsum, gsum_t)


def _delta_sweep_kernel(rev, chunks, heads, *refs):
    if rev:
        u_ref, w_ref, p_ref, qd_ref, kdt_ref, gt_ref, of_ref, z_ref, nw_ref, o_ref, s_ref = refs
    else:
        u_ref, w_ref, p_ref, qd_ref, kdt_ref, gt_ref, o_ref, s_ref = refs
    C = GDN_CHUNK
    d = 1 if rev else 0

    @pl.when(pl.program_id(2) == 0)
    def _():
        s_ref[...] = jnp.zeros_like(s_ref)

    states = range(2 * heads)
    zeros = jnp.zeros((C, GDN_DK), BF16)
    pad = lambda v, half: jnp.concatenate([zeros, v] if half else [v, zeros], axis=0)
    order = range(chunks - 1, -1, -1) if rev else range(chunks)
    for ci in order:
        rows = slice(ci * C, (ci + 1) * C)
        pair = slice((ci // 2) * 2 * C, (ci // 2 + 1) * 2 * C)
        last = ci * C + (0 if rev else C - 1)
        cols = [slice(h * LANES, (h + 1) * LANES) for h in states]
        s_old = [s_ref[h] for h in states]
        wqs = [_dot(jnp.concatenate([w_ref[rows, cols[h]], qd_ref[rows, cols[h]]], axis=0),
                    s_old[h].astype(BF16)) for h in states]
        vns = [(u_ref[rows, cols[h]].astype(F32) - wqs[h][:C]).astype(BF16) for h in states]
        pvs = [_dot(p_ref[rows, (h // 2) * LANES:(h // 2 + 1) * LANES], pad(vns[h], h % 2)) for h in states]
        kvs = [_dot(kdt_ref[cols[h], pair], pad(vns[h], ci % 2)) for h in states]
        for h in states:
            lg = 8 * (h // 2) + _gate_lane(d, h % 2, True)
            s_ref[h] = s_old[h] * jnp.exp(gt_ref[lg:lg + 1, last:last + 1]) + kvs[h]
            o = wqs[h][C:] + pvs[h]
            if rev:
                tot = of_ref[rows, cols[h]] + o
                gate = _silu(z_ref[rows, cols[h]].astype(F32))
                o_ref[rows, cols[h]] = (_rms(tot, nw_ref[...]) * gate).astype(BF16)
            else:
                o_ref[rows, cols[h]] = o


def _gdn_delta_sweep(rev, u, w, p, qd, kdt, gsum_t, batch, chunks, heads, extra=None):
    T = u.shape[0]
    R = chunks * GDN_CHUNK
    nb = T // batch // R
    assert chunks % 2 == 0
    rows = (lambda b, n: b * nb + (nb - 1 - n)) if rev else (lambda b, n: b * nb + n)
    qk_w = heads * LANES
    v_w = 2 * heads * LANES
    wide = pl.BlockSpec((R, v_w), lambda b, h, n: (rows(b, n), h))
    narrow = pl.BlockSpec((R, qk_w), lambda b, h, n: (rows(b, n), h))
    in_specs = [
        wide, wide, narrow, wide,
        pl.BlockSpec((v_w, R), lambda b, h, n: (h, rows(b, n))),
        pl.BlockSpec((8 * heads, R), lambda b, h, n: (h, rows(b, n))),
    ]
    args = [u, w, p, qd, kdt, gsum_t]
    if rev:
        o_fwd, pre, norm_w = extra
        z_block0 = GDN_CONV_DIM // v_w
        in_specs += [
            wide,
            pl.BlockSpec((R, v_w), lambda b, h, n: (rows(b, n), z_block0 + h)),
            pl.BlockSpec((1, LANES), lambda b, h, n: (0, 0)),
        ]
        args += [o_fwd, pre, norm_w]
    return pl.pallas_call(
        functools.partial(_delta_sweep_kernel, rev, chunks, heads),
        grid=(batch, GDN_QK_HEADS // heads, nb),
        in_specs=in_specs,
        out_specs=wide,
        out_shape=jax.ShapeDtypeStruct((T, GDN_V_DIM), BF16 if rev else F32),
        scratch_shapes=[pltpu.VMEM((2 * heads, GDN_DK, GDN_DK), F32)],
        compiler_params=_params(("parallel", "parallel", "arbitrary")),
        name="gdn_sweep_bwd" if rev else "gdn_sweep_fwd",
    )(*args)


def _out_proj_kernel(transposed, a_ref, w_ref, nw_ref, x_ref, o_ref):
    m = _dot_tn(a_ref[...], w_ref[...]) if transposed else _dot(a_ref[...], w_ref[...])
    o_ref[...] = x_ref[...] + _rms(m, nw_ref[...])


def _out_proj(a, w, nw, x, tm, transposed=False):
    K, D = w.shape
    T = x.shape[0]
    a_spec = pl.BlockSpec((K, tm), lambda i: (0, i)) if transposed else pl.BlockSpec((tm, K), lambda i: (i, 0))
    return pl.pallas_call(
        functools.partial(_out_proj_kernel, transposed),
        grid=(T // tm,),
        in_specs=[
            a_spec,
            _resident((K, D), lambda i: (0, 0)),
            pl.BlockSpec((1, D), lambda i: (0, 0)),
            pl.BlockSpec((tm, D), lambda i: (i, 0)),
        ],
        out_specs=pl.BlockSpec((tm, D), lambda i: (i, 0)),
        out_shape=jax.ShapeDtypeStruct((T, D), F32),
        compiler_params=_params(("parallel",)),
        name="out_proj_t" if transposed else "out_proj",
    )(a, w, nw, x)


def _ffn_kernel(x_ref, nw1_ref, w1_ref, w2_ref, nw2_ref, o_ref, xn_ref, acc_ref):
    j = pl.program_id(1)

    @pl.when(j == 0)
    def _():
        xn_ref[...] = _rms(x_ref[...], nw1_ref[...]).astype(BF16)
        acc_ref[...] = jnp.zeros_like(acc_ref)

    h = jnp.maximum(_dot(xn_ref[...], w1_ref[...]), 0.0)
    acc_ref[...] += _dot((h * h).astype(BF16), w2_ref[...])

    @pl.when(j == pl.num_programs(1) - 1)
    def _():
        o_ref[...] = x_ref[...] + _rms(acc_ref[...], nw2_ref[...])


def _ffn(x, nw1, w1, w2, nw2, tm, tf):
    T, D = x.shape
    Fd = w1.shape[1]
    return pl.pallas_call(
        _ffn_kernel,
        grid=(T // tm, Fd // tf),
        in_specs=[
            pl.BlockSpec((tm, D), lambda i, j: (i, 0)),
            pl.BlockSpec((1, D), lambda i, j: (0, 0)),
            pl.BlockSpec((D, tf), lambda i, j: (0, j)),
            pl.BlockSpec((tf, D), lambda i, j: (j, 0)),
            pl.BlockSpec((1, D), lambda i, j: (0, 0)),
        ],
        out_specs=pl.BlockSpec((tm, D), lambda i, j: (i, 0)),
        out_shape=jax.ShapeDtypeStruct((T, D), F32),
        scratch_shapes=[pltpu.VMEM((tm, D), BF16), pltpu.VMEM((tm, D), F32)],
        compiler_params=_params(("parallel", "arbitrary")),
        name="ffn",
    )(x, nw1, w1, w2, nw2)


def _mla_proj_kernel(x_ref, nw_ref, wa_ref, qn_ref, kvn_ref, wqt_ref, wkn_ref, wvt_ref,
                     cos_ref, sin_ref, cost_ref, sint_ref, qt_ref, k_ref, vt_ref):
    xn = _rms(x_ref[...], nw_ref[...]).astype(BF16)
    a = _dot(xn, wa_ref[...])
    cq = _rms(a[:, :MLA_Q_RANK], qn_ref[...]).astype(BF16)
    ckv = _rms(a[:, MLA_Q_RANK:MLA_Q_RANK + MLA_KV_RANK], kvn_ref[...]).astype(BF16)

    kr = a[:, MLA_Q_RANK + MLA_KV_RANK:]
    k_rope = (kr * cos_ref[...] + pltpu.roll(kr, LANES // 2, axis=1) * sin_ref[...]).astype(BF16)
    kn = _dot(ckv, wkn_ref[...])
    for h in range(MLA_HEADS):
        c0 = h * MLA_D_PAD
        k_ref[:, c0:c0 + LANES] = kn[:, h * LANES:(h + 1) * LANES].astype(BF16)
        k_ref[:, c0 + LANES:c0 + 2 * LANES] = k_rope

    scale = MLA_D_QK ** -0.5 * math.log2(math.e)
    qt = _dot_nt(wqt_ref[...], cq)
    cost = cost_ref[...]
    sint = sint_ref[...]
    half = LANES // 2
    for h in range(MLA_HEADS):
        r0 = h * MLA_D_PAD
        qt_ref[r0:r0 + LANES, :] = (qt[r0:r0 + LANES] * scale).astype(BF16)
        blk = qt[r0 + LANES:r0 + 2 * LANES]
        swapped = jnp.concatenate([blk[half:], blk[:half]], axis=0)
        qt_ref[r0 + LANES:r0 + 2 * LANES, :] = ((blk * cost + swapped * sint) * scale).astype(BF16)

    vt = _dot_nt(wvt_ref[...], ckv)
    ones = jnp.ones((MLA_V_ROWS - MLA_D_V, vt.shape[1]), BF16)
    for h in range(MLA_HEADS):
        r0 = h * MLA_V_ROWS
        vt_ref[r0:r0 + MLA_D_V, :] = vt[h * MLA_D_V:(h + 1) * MLA_D_V].astype(BF16)
        vt_ref[r0 + MLA_D_V:r0 + MLA_V_ROWS, :] = ones


def _mla_proj(x, nw, wa, qn, kvn, wqt, wkn, wvt, cos, sin, seq_len, tm):
    T, D = x.shape
    pos_blocks = seq_len // tm
    row = lambda i: (i, 0)
    col = lambda i: (0, i)
    fixed = lambda i: (0, 0)
    return pl.pallas_call(
        _mla_proj_kernel,
        grid=(T // tm,),
        in_specs=[
            pl.BlockSpec((tm, D), row),
            pl.BlockSpec((1, D), fixed),
            _resident(wa.shape, fixed),
            pl.BlockSpec((1, MLA_Q_RANK), fixed),
            pl.BlockSpec((1, MLA_KV_RANK), fixed),
            _resident(wqt.shape, fixed),
            _resident(wkn.shape, fixed),
            _resident(wvt.shape, fixed),
            pl.BlockSpec((tm, LANES), lambda i: (i % pos_blocks, 0)),
            pl.BlockSpec((tm, LANES), lambda i: (i % pos_blocks, 0)),
            pl.BlockSpec((LANES, tm), lambda i: (0, i % pos_blocks)),
            pl.BlockSpec((LANES, tm), lambda i: (0, i % pos_blocks)),
        ],
        out_specs=[
            pl.BlockSpec((MLA_HEADS * MLA_D_PAD, tm), col),
            pl.BlockSpec((tm, MLA_HEADS * MLA_D_PAD), row),
            pl.BlockSpec((MLA_HEADS * MLA_V_ROWS, tm), col),
        ],
        out_shape=[
            jax.ShapeDtypeStruct((MLA_HEADS * MLA_D_PAD, T), BF16),
            jax.ShapeDtypeStruct((T, MLA_HEADS * MLA_D_PAD), BF16),
            jax.ShapeDtypeStruct((MLA_HEADS * MLA_V_ROWS, T), BF16),
        ],
        compiler_params=_params(("parallel",)),
        name="mla_proj",
    )(x, nw, wa, qn, kvn, wqt, wkn, wvt, cos, sin, cos.T, sin.T)


ATTN_SOFTMAX_ROWS = 64
ATTN_SCORE_SLOTS = 4


def _attn_kernel(tk, unroll, qt_ref, k_ref, vt_ref, ot_ref, s_ref, p_ref, acc_ref):
    tq = qt_ref.shape[1]
    n = k_ref.shape[0] // tk
    R = ATTN_SOFTMAX_ROWS

    def chunk(t):
        return pl.ds(t * tk if isinstance(t, int) else pl.multiple_of(t * tk, tk), tk)

    def scores(t, slot):
        s_ref[slot] = _dot(k_ref[chunk(t), :], qt_ref[...])

    def weighted(t, slot, a):
        acc_ref[...] = a * acc_ref[...] + _dot(vt_ref[:, chunk(t)], p_ref[slot])

    def softmax(slot, m, p_slot):
        mx = s_ref[slot, 0:R, :]
        for r in range(R, tk, R):
            mx = jnp.maximum(mx, s_ref[slot, r:r + R, :])
        m_new = jnp.maximum(m, jnp.max(mx, axis=0, keepdims=True))
        for r in range(0, tk, R):
            p_ref[p_slot, r:r + R, :] = jnp.exp2(s_ref[slot, r:r + R, :] - m_new).astype(BF16)
        return m_new, jnp.exp2(m - m_new)

    def trip(t, phase, m, a, prefetch=True):
        if prefetch:
            scores(t + 2, (phase + 2) % ATTN_SCORE_SLOTS)
        weighted(t - 1, (phase - 1) % 2, a)
        return softmax(phase % ATTN_SCORE_SLOTS, m, phase % 2)

    def body(i, carry):
        m, a = carry
        t0 = 1 + unroll * i
        for j in range(unroll):
            m, a = trip(t0 + j, 1 + j, m, a)
        return m, a

    acc_ref[...] = jnp.zeros_like(acc_ref)
    scores(0, 0)
    scores(1, 1)
    m, a = softmax(0, jnp.full((1, tq), -jnp.inf, F32), 0)
    scores(2, 2)
    bodies = (n - 3) // unroll
    m, a = lax.fori_loop(0, bodies, body, (m, a))
    for t in range(1 + bodies * unroll, n):
        m, a = trip(t, t % ATTN_SCORE_SLOTS, m, a, prefetch=t + 2 < n)
    weighted(n - 1, (n - 1) % 2, a)
    acc = acc_ref[...]
    ot_ref[...] = (acc[:MLA_D_V] / acc[MLA_D_V:MLA_D_V + 1]).astype(BF16)


def _attention(qt, k, vt, batch, tq, tk, unroll):
    T = k.shape[0]
    L = T // batch
    nq = L // tq
    assert unroll % ATTN_SCORE_SLOTS == 0 and L // tk >= 3
    return pl.pallas_call(
        functools.partial(_attn_kernel, tk, unroll),
        grid=(batch, MLA_HEADS, nq),
        in_specs=[
            pl.BlockSpec((MLA_D_PAD, tq), lambda b, h, i: (h, b * nq + i)),
            pl.BlockSpec((L, MLA_D_PAD), lambda b, h, i: (b, h)),
            pl.BlockSpec((MLA_V_ROWS, L), lambda b, h, i: (h, b)),
        ],
        out_specs=pl.BlockSpec((MLA_D_V, tq), lambda b, h, i: (h, b * nq + i)),
        out_shape=jax.ShapeDtypeStruct((MLA_HEADS * MLA_D_V, T), BF16),
        scratch_shapes=[pltpu.VMEM((ATTN_SCORE_SLOTS, tk, tq), F32), pltpu.VMEM((2, tk, tq), BF16),
                        pltpu.VMEM((MLA_V_ROWS, tq), F32)],
        compiler_params=_params(("parallel", "parallel", "arbitrary")),
        name="attention",
    )(qt, k, vt)


def _gate_lane_perm():
    lanes = jnp.arange(GDN_GATE_LANES)
    q, e = lanes // 8, lanes % 8
    return GDN_V_HEADS * (e // 2) + 2 * q + e % 2


def _gate_lane_params(a_log, dt_bias):
    lanes = jnp.arange(GDN_GATE_LANES)
    q, e = lanes // 8, lanes % 8
    head = 2 * q + e % 2
    direction = jnp.maximum(e // 2 - 2, 0)
    is_decay = e >= 4
    neg_a = jnp.where(is_decay, -jnp.exp(a_log.astype(F32))[direction, head], 0.0)
    dt = jnp.where(is_decay, dt_bias.astype(F32)[direction, head], 0.0)
    return neg_a, dt


def _pad_rope_cols(w):
    half = MLA_D_ROPE // 2
    z = jnp.zeros(w.shape[:-1] + (half,), w.dtype)
    return jnp.concatenate([w[..., :half], z, w[..., half:], z], axis=-1)


def _rope_tables(L):
    half = MLA_D_ROPE // 2
    inv_freq = ROPE_THETA ** (-jnp.arange(half, dtype=F32) / half)
    ang = jnp.arange(L, dtype=F32)[:, None] * inv_freq[None, :]
    c, s = jnp.cos(ang), jnp.sin(ang)
    z = jnp.zeros_like(c)
    return jnp.concatenate([c, z, c, z], axis=-1), jnp.concatenate([-s, z, s, z], axis=-1)


def _prepare(p):
    w = {}
    g_in = p['gdn_w_in'][0]
    w['gdn_main'] = g_in[:, :GDN_MAIN_DIM].astype(BF16)
    wg = g_in[:, GDN_MAIN_DIM:][:, _gate_lane_perm()].astype(BF16)
    w['gdn_gate'] = wg
    w['gdn_gate_t'] = wg.T
    w['gdn_neg_a'], w['gdn_dt'] = _gate_lane_params(p['gdn_a_log'][0], p['gdn_dt_bias'][0])
    w['gdn_conv'] = p['gdn_conv_w'][0].astype(F32)
    w['gdn_norm'] = p['gdn_norm_w'][0].reshape(1, GDN_DK).astype(F32)
    w['gdn_out'] = p['gdn_w_out'][0].astype(BF16)

    wa = p['mla_w_a'][0]
    rank = MLA_Q_RANK + MLA_KV_RANK
    w['mla_a'] = jnp.concatenate([wa[:, :rank], _pad_rope_cols(wa[:, rank:])], axis=-1).astype(BF16)
    wq = p['mla_w_q_b'][0].reshape(MLA_Q_RANK, MLA_HEADS, MLA_D_QK)
    wq = jnp.concatenate([wq[..., :MLA_D_NOPE], _pad_rope_cols(wq[..., MLA_D_NOPE:])], axis=-1)
    w['mla_q_t'] = wq.reshape(MLA_Q_RANK, MLA_HEADS * MLA_D_PAD).astype(BF16).T
    wkv = p['mla_w_kv_b'][0].reshape(MLA_KV_RANK, MLA_HEADS, MLA_D_NOPE + MLA_D_V)
    w['mla_kn'] = wkv[..., :MLA_D_NOPE].reshape(MLA_KV_RANK, MLA_HEADS * MLA_D_NOPE).astype(BF16)
    w['mla_v_t'] = wkv[..., MLA_D_NOPE:].reshape(MLA_KV_RANK, MLA_HEADS * MLA_D_V).astype(BF16).T
    w['mla_qn'] = p['mla_q_a_norm'][0].reshape(1, MLA_Q_RANK).astype(F32)
    w['mla_kvn'] = p['mla_kv_a_norm'][0].reshape(1, MLA_KV_RANK).astype(F32)
    w['mla_o'] = p['mla_w_o'][0].astype(BF16)

    w['ffn_in'] = p['ffn_w_in'].astype(BF16)
    w['ffn_out'] = p['ffn_w_out'].astype(BF16)
    for name in ('norm_mix_pre', 'norm_mix_post', 'norm_ffn_pre', 'norm_ffn_post'):
        w[name] = p[name].astype(F32)[:, None, :]
    return w


TILES = dict(
    proj_tm=1024, proj_tn=1024,
    gate_tr=512,
    conv_tr=512, conv_tc=1024,
    solve_chunks=16,
    sweep_chunks=4, sweep_heads=4,
    out_tm=256,
    ffn_tm=512, ffn_tf=1024,
    mla_tm=256,
    attn_tq=256, attn_tk=512, attn_unroll=32,
)


def _trunk(x3, w, tiles):
    B, L, D = x3.shape
    x = x3.reshape(B * L, D)

    pre, gates, gates_t = _gdn_proj(x, w['norm_mix_pre'][0], w['gdn_main'], w['gdn_gate'], w['gdn_gate_t'],
                                    tiles['proj_tm'], tiles['proj_tn'])
    gsum, gsum_t = _gdn_gates(gates, gates_t, w['gdn_neg_a'], w['gdn_dt'], tiles['gate_tr'])
    qkv = _gdn_conv(pre, w['gdn_conv'], L, tiles['conv_tr'], tiles['conv_tc'])
    solved = _gdn_delta_solve(qkv, gsum, gsum_t, tiles['solve_chunks'])
    sweep = functools.partial(_gdn_delta_sweep, gsum_t=gsum_t, batch=B, chunks=tiles['sweep_chunks'],
                              heads=tiles['sweep_heads'])
    o_fwd = sweep(False, *solved[:5])
    o = sweep(True, *solved[5:], extra=(o_fwd, pre, w['gdn_norm']))
    x = _out_proj(o, w['gdn_out'], w['norm_mix_post'][0], x, tiles['out_tm'])
    x = _ffn(x, w['norm_ffn_pre'][0], w['ffn_in'][0], w['ffn_out'][0], w['norm_ffn_post'][0],
             tiles['ffn_tm'], tiles['ffn_tf'])

    cos, sin = _rope_tables(L)
    qt, k, vt = _mla_proj(x, w['norm_mix_pre'][1], w['mla_a'], w['mla_qn'], w['mla_kvn'], w['mla_q_t'],
                          w['mla_kn'], w['mla_v_t'], cos, sin, L, tiles['mla_tm'])
    ot = _attention(qt, k, vt, B, tiles['attn_tq'], tiles['attn_tk'], tiles['attn_unroll'])
    x = _out_proj(ot, w['mla_o'], w['norm_mix_post'][1], x, tiles['out_tm'], transposed=True)
    x = _ffn(x, w['norm_ffn_pre'][1], w['ffn_in'][1], w['ffn_out'][1], w['norm_ffn_post'][1],
             tiles['ffn_tm'], tiles['ffn_tf'])
    return x.reshape(B, L, D)


def kernel(x_prompt, x_sample, norm_mix_pre, norm_mix_post, norm_ffn_pre, norm_ffn_post, gdn_w_in, gdn_conv_w, gdn_a_log, gdn_dt_bias, gdn_norm_w, gdn_w_out, mla_w_a, mla_q_a_norm, mla_w_q_b, mla_kv_a_norm, mla_w_kv_b, mla_w_o, ffn_w_in, ffn_w_out):
    w = _prepare(dict(
        norm_mix_pre=norm_mix_pre, norm_mix_post=norm_mix_post, norm_ffn_pre=norm_ffn_pre,
        norm_ffn_post=norm_ffn_post, gdn_w_in=gdn_w_in, gdn_conv_w=gdn_conv_w, gdn_a_log=gdn_a_log,
        gdn_dt_bias=gdn_dt_bias, gdn_norm_w=gdn_norm_w, gdn_w_out=gdn_w_out, mla_w_a=mla_w_a,
        mla_q_a_norm=mla_q_a_norm, mla_w_q_b=mla_w_q_b, mla_kv_a_norm=mla_kv_a_norm,
        mla_w_kv_b=mla_w_kv_b, mla_w_o=mla_w_o, ffn_w_in=ffn_w_in, ffn_w_out=ffn_w_out))
    return _trunk(x_prompt, w, TILES), _trunk(x_sample, w, TILES)
```

```python
import functools
import math

import jax
import jax.numpy as jnp
from jax import lax
from jax.experimental import pallas as pl
from jax.experimental.pallas import tpu as pltpu

F32 = jnp.float32
BF16 = jnp.bfloat16

RMS_EPS = 1e-6
L2_EPS = 1e-6
LANES = 128
BF16_ROWS = 16

GDN_QK_HEADS = 16
GDN_V_HEADS = 32
GDN_DK = 128
GDN_CONV = 5
GDN_CHUNK = 64
GDN_Q_DIM = GDN_QK_HEADS * GDN_DK
GDN_V_DIM = GDN_V_HEADS * GDN_DK
GDN_CONV_DIM = 2 * GDN_Q_DIM + GDN_V_DIM
GDN_MAIN_DIM = GDN_CONV_DIM + GDN_V_DIM
GDN_GATE_LANES = 4 * GDN_V_HEADS

MLA_HEADS = 16
MLA_Q_RANK = 768
MLA_KV_RANK = 512
MLA_D_NOPE = 128
MLA_D_ROPE = 64
MLA_D_V = 128
MLA_D_QK = MLA_D_NOPE + MLA_D_ROPE
MLA_D_PAD = 2 * LANES
MLA_V_ROWS = MLA_D_V + BF16_ROWS
ROPE_THETA = 10000.0

VMEM_LIMIT = 56 * 1024 * 1024


def _params(sem):
    return pltpu.CompilerParams(dimension_semantics=sem, vmem_limit_bytes=VMEM_LIMIT)


def _resident(shape, index_map):
    return pl.BlockSpec(shape, index_map, pipeline_mode=pl.Buffered(1))


def _rms(x, w):
    return x * lax.rsqrt(jnp.mean(x * x, axis=-1, keepdims=True) + RMS_EPS) * w


def _dot(a, b):
    return jnp.dot(a, b, preferred_element_type=F32)


def _dot_nt(a, b):
    return lax.dot_general(a, b, (((1,), (1,)), ((), ())), preferred_element_type=F32)


def _dot_tn(a, b):
    return lax.dot_general(a, b, (((0,), (0,)), ((), ())), preferred_element_type=F32)


def _split3(x):
    hi = x.astype(BF16)
    r1 = x - hi.astype(F32)
    mid = r1.astype(BF16)
    lo = (r1 - mid.astype(F32)).astype(BF16)
    return hi, mid, lo


def _silu(x):
    return x * (1.0 / (1.0 + jnp.exp(-x)))


def _gdn_proj_kernel(x_ref, nw_ref, w_ref, wg_ref, wgt_ref, o_ref, g_ref, gt_ref, xn_ref):
    @pl.when(pl.program_id(1) == 0)
    def _():
        xn = _rms(x_ref[...], nw_ref[...]).astype(BF16)
        xn_ref[...] = xn
        g_ref[...] = _dot(xn, wg_ref[...])
        gt_ref[...] = _dot_nt(wgt_ref[...], xn)

    o_ref[...] = _dot(xn_ref[...], w_ref[...]).astype(BF16)


def _gdn_proj(x, nw, w, wg, wgt, tm, tn):
    T, D = x.shape
    N = w.shape[1]
    return pl.pallas_call(
        _gdn_proj_kernel,
        grid=(T // tm, N // tn),
        in_specs=[
            pl.BlockSpec((tm, D), lambda i, j: (i, 0)),
            pl.BlockSpec((1, D), lambda i, j: (0, 0)),
            pl.BlockSpec((D, tn), lambda i, j: (0, j)),
            pl.BlockSpec((D, LANES), lambda i, j: (0, 0)),
            pl.BlockSpec((LANES, D), lambda i, j: (0, 0)),
        ],
        out_specs=[
            pl.BlockSpec((tm, tn), lambda i, j: (i, j)),
            pl.BlockSpec((tm, LANES), lambda i, j: (i, 0)),
            pl.BlockSpec((LANES, tm), lambda i, j: (0, i)),
        ],
        out_shape=[
            jax.ShapeDtypeStruct((T, N), BF16),
            jax.ShapeDtypeStruct((T, LANES), F32),
            jax.ShapeDtypeStruct((LANES, T), F32),
        ],
        scratch_shapes=[pltpu.VMEM((tm, D), BF16)],
        compiler_params=_params(("parallel", "arbitrary")),
        name="gdn_proj",
    )(x, nw, w, wg, wgt)


def _softplus(y):
    return jnp.maximum(y, 0.0) + jnp.log1p(jnp.exp(-jnp.abs(y)))


def _gate_values(x, neg_a, dt):
    beta = 1.0 / (1.0 + jnp.exp(-x))
    g = neg_a * _softplus(x + dt)
    return beta, g


def _gdn_gates_kernel(x_ref, xt_ref, na_ref, dt_ref, nat_ref, dtt_ref, o_ref, ot_ref):
    R = x_ref.shape[0]
    ii = lax.broadcasted_iota(jnp.int32, (R, R), 0)
    jj = lax.broadcasted_iota(jnp.int32, (R, R), 1)
    same = (ii // GDN_CHUNK) == (jj // GDN_CHUNK)
    lower = jnp.where(same & (jj <= ii), 1.0, 0.0).astype(BF16)
    upper = jnp.where(same & (jj >= ii), 1.0, 0.0).astype(BF16)

    beta, g = _gate_values(x_ref[...], na_ref[...], dt_ref[...])
    parts = _split3(g)
    cf = sum(_dot(lower, p) for p in parts)
    cb = sum(_dot(upper, p) for p in parts)
    e = lax.broadcasted_iota(jnp.int32, (R, LANES), 1) % 8
    o_ref[...] = jnp.where(e < 4, beta, jnp.where(e < 6, cf, cb))

    beta_t, g_t = _gate_values(xt_ref[...], nat_ref[...], dtt_ref[...])
    parts_t = _split3(g_t)
    cf_t = sum(_dot(p, upper) for p in parts_t)
    cb_t = sum(_dot(p, lower) for p in parts_t)
    e_t = lax.broadcasted_iota(jnp.int32, (LANES, R), 0) % 8
    ot_ref[...] = jnp.where(e_t < 4, beta_t, jnp.where(e_t < 6, cf_t, cb_t))


def _gdn_gates(gates, gates_t, neg_a, dt, tr):
    T = gates.shape[0]
    row = lambda i: (i, 0)
    col = lambda i: (0, i)
    fixed = lambda i: (0, 0)
    return pl.pallas_call(
        _gdn_gates_kernel,
        grid=(T // tr,),
        in_specs=[
            pl.BlockSpec((tr, LANES), row),
            pl.BlockSpec((LANES, tr), col),
            pl.BlockSpec((1, LANES), fixed),
            pl.BlockSpec((1, LANES), fixed),
            pl.BlockSpec((LANES, 1), fixed),
            pl.BlockSpec((LANES, 1), fixed),
        ],
        out_specs=[pl.BlockSpec((tr, LANES), row), pl.BlockSpec((LANES, tr), col)],
        out_shape=[jax.ShapeDtypeStruct((T, LANES), F32), jax.ShapeDtypeStruct((LANES, T), F32)],
        compiler_params=_params(("parallel",)),
        name="gdn_gates",
    )(gates, gates_t, neg_a.reshape(1, LANES), dt.reshape(1, LANES),
      neg_a.reshape(LANES, 1), dt.reshape(LANES, 1))


CONV_HALO = BF16_ROWS
CONV_PAD = 8


def _gdn_conv_kernel(blocks_per_seq, n_q_blocks, n_qk_blocks,
                     prev_ref, main_ref, next_ref, w_ref, o_ref, xs_ref):
    i = pl.program_id(0)
    j = pl.program_id(1)
    tr, tc = main_ref.shape
    r = GDN_CONV // 2
    pos = i % blocks_per_seq
    keep_prev = jnp.where(pos == 0, 0.0, 1.0)
    keep_next = jnp.where(pos == blocks_per_seq - 1, 0.0, 1.0)
    xs_ref[CONV_PAD - r:CONV_PAD, :] = prev_ref[CONV_HALO - r:, :].astype(F32) * keep_prev
    xs_ref[CONV_PAD:CONV_PAD + tr, :] = main_ref[...].astype(F32)
    xs_ref[CONV_PAD + tr:CONV_PAD + tr + r, :] = next_ref[:r, :].astype(F32) * keep_next
    w = w_ref[...]
    acc = xs_ref[CONV_PAD - r:CONV_PAD - r + tr, :] * w[0:1, :]
    for t in range(1, GDN_CONV):
        acc = acc + xs_ref[CONV_PAD - r + t:CONV_PAD - r + t + tr, :] * w[t:t + 1, :]
    y = _silu(acc)

    @pl.when(j >= n_qk_blocks)
    def _():
        o_ref[...] = y.astype(BF16)

    @pl.when(j < n_qk_blocks)
    def _():
        q_scale = jnp.where(j < n_q_blocks, GDN_DK ** -0.5, 1.0)
        for h in range(tc // LANES):
            blk = y[:, h * LANES:(h + 1) * LANES]
            inv = lax.rsqrt(jnp.sum(blk * blk, axis=-1, keepdims=True) + L2_EPS) * q_scale
            o_ref[:, h * LANES:(h + 1) * LANES] = (blk * inv).astype(BF16)


def _gdn_conv(pre, conv_w, seq_len, tr, tc):
    T = pre.shape[0]
    C = conv_w.shape[1]
    hb = tr // CONV_HALO
    n_halo = T // CONV_HALO
    kern = functools.partial(_gdn_conv_kernel, seq_len // tr, GDN_Q_DIM // tc, 2 * GDN_Q_DIM // tc)
    return pl.pallas_call(
        kern,
        grid=(T // tr, C // tc),
        in_specs=[
            pl.BlockSpec((CONV_HALO, tc), lambda i, j: (jnp.maximum(i * hb - 1, 0), j)),
            pl.BlockSpec((tr, tc), lambda i, j: (i, j)),
            pl.BlockSpec((CONV_HALO, tc), lambda i, j: (jnp.minimum((i + 1) * hb, n_halo - 1), j)),
            pl.BlockSpec((GDN_CONV, tc), lambda i, j: (0, j)),
        ],
        out_specs=pl.BlockSpec((tr, tc), lambda i, j: (i, j)),
        out_shape=jax.ShapeDtypeStruct((T, C), BF16),
        scratch_shapes=[pltpu.VMEM((tr + 2 * CONV_PAD, tc), F32)],
        compiler_params=_params(("parallel", "parallel")),
        name="gdn_conv",
    )(pre, pre, pre, conv_w)


NEUMANN_DOUBLINGS = 5


def _gate_lane(direction, v_head, decay):
    return (4 if decay else 0) + 2 * direction + v_head


def _delta_solve_kernel(chunks, q_ref, k_ref, v_ref, g_ref, gt_ref,
                        uf_ref, wf_ref, pf_ref, qf_ref, kf_ref, ub_ref, wb_ref, pb_ref, qb_ref, kb_ref):
    C = GDN_CHUNK
    W = 4 * C
    qh = pl.program_id(1)
    gsel = pltpu.roll(g_ref[...], (LANES - 8 * qh) % LANES, axis=1)
    ii = lax.broadcasted_iota(jnp.int32, (C, W), 0)
    ll = lax.broadcasted_iota(jnp.int32, (C, W), 1)
    jj = ll % C
    group = ll // C
    ahead = jnp.where(group >= 2, jj - ii, ii - jj)
    incl = ahead >= 0
    strict = ahead > 0
    eye = jnp.where(ii == jj, 1.0, 0.0)
    own = [group == p for p in range(4)]
    low_half = lax.broadcasted_iota(jnp.int32, (C, LANES), 1) < C
    outs = ((uf_ref, wf_ref, pf_ref, qf_ref, kf_ref), (ub_ref, wb_ref, pb_ref, qb_ref, kb_ref))
    eye_k = jnp.where(lax.broadcasted_iota(jnp.int32, (GDN_DK, GDN_DK), 0)
                      == lax.broadcasted_iota(jnp.int32, (GDN_DK, GDN_DK), 1), 1.0, 0.0).astype(BF16)
    zeros = jnp.zeros((C, 2 * LANES), BF16)

    def side_by_side(cols):
        return jnp.concatenate([jnp.where(low_half, cols[0], cols[1]),
                                jnp.where(low_half, cols[2], cols[3])], axis=1)

    def block_diag(x):
        return jnp.concatenate([jnp.where(own[p], x, 0.0).astype(BF16) for p in range(4)], axis=0)

    ts, xs, rhs = [], [], []
    for ci in range(chunks):
        rows = slice(ci * C, (ci + 1) * C)
        kc = k_ref[rows, :]
        qc = q_ref[rows, :]
        kq = _dot_nt(jnp.concatenate([kc, qc, eye_k], axis=0), jnp.concatenate([kc] * 4, axis=0))
        betas = [jnp.broadcast_to(gsel[rows, p:p + 1], (C, LANES)) for p in range(4)]
        gccs = [jnp.broadcast_to(gsel[rows, 4 + p:5 + p], (C, LANES)) for p in range(4)]
        g_pair = gt_ref[:, (ci // 2) * 2 * C:(ci // 2 + 1) * 2 * C]
        g_swap = pltpu.roll(g_pair, C, axis=1)
        on_low, on_high = (g_pair, g_swap) if ci % 2 == 0 else (g_swap, g_pair)
        gcrs = [on_low[4 + p:5 + p, :C] for p in range(4)]
        gcr_rows = [jnp.where(low_half[0:1], on_low[4 + p:5 + p], on_high[5 + p:6 + p]) for p in (0, 2)]
        decay = jnp.concatenate([jnp.where(low_half, gccs[p], gccs[p + 1]) - gcr_rows[p // 2]
                                 for p in (0, 2)], axis=1)
        decay = jnp.where(incl, jnp.exp(jnp.where(incl, decay, 0.0)), 0.0)
        a = jnp.where(strict, side_by_side(betas) * kq[:C] * decay, 0.0)
        ts.append(eye - a)
        xs.append(a)
        pm = (kq[C:2 * C] * decay).astype(BF16)
        pf_ref[rows, :] = pm[:, :2 * C]
        pb_ref[rows, :] = pm[:, 2 * C:]
        kcf = kc.astype(F32)
        qcf = qc.astype(F32)
        for p in range(4):
            d, hv = divmod(p, 2)
            cols = slice(hv * LANES, (hv + 1) * LANES)
            gl = gcrs[p][:, 0:1] if d else gcrs[p][:, C - 1:C]
            eg = jnp.exp(gccs[p])
            outs[d][3][rows, cols] = (qcf * eg).astype(BF16)
            outs[d][4][cols, rows] = (kq[2 * C:, :C] * jnp.exp(gl - gcrs[p])).astype(BF16)
            vb = v_ref[rows, cols].astype(F32) * betas[p]
            kb = kcf * (betas[p] * eg)
            rhs.append(jnp.concatenate([vb, kb], axis=1).astype(BF16))

    xs = [_dot(x.astype(BF16), block_diag(x)) for x in xs]
    for step in range(NEUMANN_DOUBLINGS):
        if step + 1 < NEUMANN_DOUBLINGS:
            both = [_dot(jnp.concatenate([t, x], axis=0).astype(BF16), block_diag(x)) for t, x in zip(ts, xs)]
            ts = [t + b[:C] for t, b in zip(ts, both)]
            xs = [b[C:] for b in both]
        else:
            ts = [t + _dot(t.astype(BF16), block_diag(x)) for t, x in zip(ts, xs)]

    for ci in range(chunks):
        rows = slice(ci * C, (ci + 1) * C)
        for p in range(4):
            d, hv = divmod(p, 2)
            cols = slice(hv * LANES, (hv + 1) * LANES)
            lhs = jnp.where(own[p], ts[ci], 0.0).astype(BF16)
            padded = jnp.concatenate([zeros] * p + [rhs[4 * ci + p]] + [zeros] * (3 - p), axis=0)
            uw = _dot(lhs, padded)
            outs[d][0][rows, cols] = uw[:, :LANES].astype(BF16)
            outs[d][1][rows, cols] = uw[:, LANES:].astype(BF16)


def _gdn_delta_solve(qkv, gsum, gsum_t, chunks):
    T = qkv.shape[0]
    R = chunks * GDN_CHUNK
    k_block0 = GDN_Q_DIM // LANES
    v_block0 = 2 * GDN_Q_DIM // (2 * LANES)
    wide = pl.BlockSpec((R, 2 * LANES), lambda i, h: (i, h))
    narrow = pl.BlockSpec((R, LANES), lambda i, h: (i, h))
    tall = pl.BlockSpec((2 * GDN_DK, R), lambda i, h: (h, i))
    uw_shape = jax.ShapeDtypeStruct((T, GDN_V_DIM), BF16)
    p_shape = jax.ShapeDtypeStruct((T, GDN_V_HEADS * GDN_CHUNK), BF16)
    kt_shape = jax.ShapeDtypeStruct((GDN_V_HEADS * GDN_DK, T), BF16)
    return pl.pallas_call(
        functools.partial(_delta_solve_kernel, chunks),
        grid=(T // R, GDN_QK_HEADS),
        in_specs=[
            narrow,
            pl.BlockSpec((R, LANES), lambda i, h: (i, k_block0 + h)),
            pl.BlockSpec((R, 2 * LANES), lambda i, h: (i, v_block0 + h)),
            pl.BlockSpec((R, LANES), lambda i, h: (i, 0)),
            pl.BlockSpec((8, R), lambda i, h: (h, i)),
        ],
        out_specs=[wide, wide, narrow, wide, tall] * 2,
        out_shape=[uw_shape, uw_shape, p_shape, uw_shape, kt_shape] * 2,
        compiler_params=_params(("parallel", "parallel")),
        name="gdn_delta_solve",
    )(qkv, qkv, qkv, gsum, gsum_t)


def _delta_sweep_kernel(rev, chunks, heads, *refs):
    if rev:
        u_ref, w_ref, p_ref, qd_ref, kdt_ref, gt_ref, of_ref, z_ref, nw_ref, o_ref, s_ref = refs
    else:
        u_ref, w_ref, p_ref, qd_ref, kdt_ref, gt_ref, o_ref, s_ref = refs
    C = GDN_CHUNK
    d = 1 if rev else 0

    @pl.when(pl.program_id(2) == 0)
    def _():
        s_ref[...] = jnp.zeros_like(s_ref)

    states = range(2 * heads)
    zc = jnp.zeros((C, GDN_DK), BF16)
    zs = jnp.zeros((GDN_DK, GDN_DK), BF16)
    diag = lambda a, b, z: jnp.concatenate([jnp.concatenate([a, z], axis=1),
                                            jnp.concatenate([z, b], axis=1)], axis=0)
    order = range(chunks - 1, -1, -1) if rev else range(chunks)
    for ci in order:
        rows = slice(ci * C, (ci + 1) * C)
        pair = slice((ci // 2) * 2 * C, (ci // 2 + 1) * 2 * C)
        last = ci * C + (0 if rev else C - 1)
        cols = [slice(h * LANES, (h + 1) * LANES) for h in states]
        both = [slice(2 * j * LANES, (2 * j + 2) * LANES) for j in range(heads)]
        s_old = [s_ref[h] for h in states]
        wqs = [_dot(jnp.concatenate([w_ref[rows, both[j]], qd_ref[rows, both[j]]], axis=0),
                    diag(s_old[2 * j].astype(BF16), s_old[2 * j + 1].astype(BF16), zs)) for j in range(heads)]
        vns = [(u_ref[rows, both[j]].astype(F32) - wqs[j][:C]).astype(BF16) for j in range(heads)]
        vn = [vns[h // 2][:, (h % 2) * LANES:(h % 2 + 1) * LANES] for h in states]
        pvs = [_dot(p_ref[rows, j * LANES:(j + 1) * LANES], diag(vn[2 * j], vn[2 * j + 1], zc))
               for j in range(heads)]
        kvs = [_dot(kdt_ref[cols[h], pair], jnp.concatenate([zc, vn[h]] if ci % 2 else [vn[h], zc], axis=0))
               for h in states]
        for h in states:
            lg = 8 * (h // 2) + _gate_lane(d, h % 2, True)
            s_ref[h] = s_old[h] * jnp.exp(gt_ref[lg:lg + 1, last:last + 1]) + kvs[h]
            half = slice((h % 2) * LANES, (h % 2 + 1) * LANES)
            o = wqs[h // 2][C:, half] + pvs[h // 2][:, half]
            if rev:
                tot = of_ref[rows, cols[h]] + o
                gate = _silu(z_ref[rows, cols[h]].astype(F32))
                o_ref[rows, cols[h]] = (_rms(tot, nw_ref[...]) * gate).astype(BF16)
            else:
                o_ref[rows, cols[h]] = o


def _gdn_delta_sweep(rev, u, w, p, qd, kdt, gsum_t, batch, chunks, heads, extra=None):
    T = u.shape[0]
    R = chunks * GDN_CHUNK
    nb = T // batch // R
    assert chunks % 2 == 0
    rows = (lambda b, n: b * nb + (nb - 1 - n)) if rev else (lambda b, n: b * nb + n)
    qk_w = heads * LANES
    v_w = 2 * heads * LANES
    wide = pl.BlockSpec((R, v_w), lambda b, h, n: (rows(b, n), h))
    narrow = pl.BlockSpec((R, qk_w), lambda b, h, n: (rows(b, n), h))
    in_specs = [
        wide, wide, narrow, wide,
        pl.BlockSpec((v_w, R), lambda b, h, n: (h, rows(b, n))),
        pl.BlockSpec((8 * heads, R), lambda b, h, n: (h, rows(b, n))),
    ]
    args = [u, w, p, qd, kdt, gsum_t]
    if rev:
        o_fwd, pre, norm_w = extra
        z_block0 = GDN_CONV_DIM // v_w
        in_specs += [
            wide,
            pl.BlockSpec((R, v_w), lambda b, h, n: (rows(b, n), z_block0 + h)),
            pl.BlockSpec((1, LANES), lambda b, h, n: (0, 0)),
        ]
        args += [o_fwd, pre, norm_w]
    return pl.pallas_call(
        functools.partial(_delta_sweep_kernel, rev, chunks, heads),
        grid=(batch, GDN_QK_HEADS // heads, nb),
        in_specs=in_specs,
        out_specs=wide,
        out_shape=jax.ShapeDtypeStruct((T, GDN_V_DIM), BF16 if rev else F32),
        scratch_shapes=[pltpu.VMEM((2 * heads, GDN_DK, GDN_DK), F32)],
        compiler_params=_params(("parallel", "parallel", "arbitrary")),
        name="gdn_sweep_bwd" if rev else "gdn_sweep_fwd",
    )(*args)


def _out_proj_kernel(transposed, a_ref, w_ref, nw_ref, x_ref, o_ref):
    m = _dot_tn(a_ref[...], w_ref[...]) if transposed else _dot(a_ref[...], w_ref[...])
    o_ref[...] = x_ref[...] + _rms(m, nw_ref[...])


def _out_proj(a, w, nw, x, tm, transposed=False):
    K, D = w.shape
    T = x.shape[0]
    a_spec = pl.BlockSpec((K, tm), lambda i: (0, i)) if transposed else pl.BlockSpec((tm, K), lambda i: (i, 0))
    return pl.pallas_call(
        functools.partial(_out_proj_kernel, transposed),
        grid=(T // tm,),
        in_specs=[
            a_spec,
            _resident((K, D), lambda i: (0, 0)),
            pl.BlockSpec((1, D), lambda i: (0, 0)),
            pl.BlockSpec((tm, D), lambda i: (i, 0)),
        ],
        out_specs=pl.BlockSpec((tm, D), lambda i: (i, 0)),
        out_shape=jax.ShapeDtypeStruct((T, D), F32),
        compiler_params=_params(("parallel",)),
        name="out_proj_t" if transposed else "out_proj",
    )(a, w, nw, x)


def _ffn_kernel(x_ref, nw1_ref, w1_ref, w2_ref, nw2_ref, o_ref, xn_ref, acc_ref):
    j = pl.program_id(1)

    @pl.when(j == 0)
    def _():
        xn_ref[...] = _rms(x_ref[...], nw1_ref[...]).astype(BF16)
        acc_ref[...] = jnp.zeros_like(acc_ref)

    h = jnp.maximum(_dot(xn_ref[...], w1_ref[...]), 0.0)
    acc_ref[...] += _dot((h * h).astype(BF16), w2_ref[...])

    @pl.when(j == pl.num_programs(1) - 1)
    def _():
        o_ref[...] = x_ref[...] + _rms(acc_ref[...], nw2_ref[...])


def _ffn(x, nw1, w1, w2, nw2, tm, tf):
    T, D = x.shape
    Fd = w1.shape[1]
    return pl.pallas_call(
        _ffn_kernel,
        grid=(T // tm, Fd // tf),
        in_specs=[
            pl.BlockSpec((tm, D), lambda i, j: (i, 0)),
            pl.BlockSpec((1, D), lambda i, j: (0, 0)),
            pl.BlockSpec((D, tf), lambda i, j: (0, j)),
            pl.BlockSpec((tf, D), lambda i, j: (j, 0)),
            pl.BlockSpec((1, D), lambda i, j: (0, 0)),
        ],
        out_specs=pl.BlockSpec((tm, D), lambda i, j: (i, 0)),
        out_shape=jax.ShapeDtypeStruct((T, D), F32),
        scratch_shapes=[pltpu.VMEM((tm, D), BF16), pltpu.VMEM((tm, D), F32)],
        compiler_params=_params(("parallel", "arbitrary")),
        name="ffn",
    )(x, nw1, w1, w2, nw2)


def _mla_proj_kernel(x_ref, nw_ref, wa_ref, qn_ref, kvn_ref, wqt_ref, wkn_ref, wvt_ref,
                     cos_ref, sin_ref, cost_ref, sint_ref, qt_ref, k_ref, vt_ref):
    xn = _rms(x_ref[...], nw_ref[...]).astype(BF16)
    a = _dot(xn, wa_ref[...])
    cq = _rms(a[:, :MLA_Q_RANK], qn_ref[...]).astype(BF16)
    ckv = _rms(a[:, MLA_Q_RANK:MLA_Q_RANK + MLA_KV_RANK], kvn_ref[...]).astype(BF16)

    kr = a[:, MLA_Q_RANK + MLA_KV_RANK:]
    k_rope = (kr * cos_ref[...] + pltpu.roll(kr, LANES // 2, axis=1) * sin_ref[...]).astype(BF16)
    kn = _dot(ckv, wkn_ref[...])
    for h in range(MLA_HEADS):
        c0 = h * MLA_D_PAD
        k_ref[:, c0:c0 + LANES] = kn[:, h * LANES:(h + 1) * LANES].astype(BF16)
        k_ref[:, c0 + LANES:c0 + 2 * LANES] = k_rope

    scale = MLA_D_QK ** -0.5 * math.log2(math.e)
    qt = _dot_nt(wqt_ref[...], cq)
    cost = cost_ref[...]
    sint = sint_ref[...]
    half = LANES // 2
    for h in range(MLA_HEADS):
        r0 = h * MLA_D_PAD
        qt_ref[r0:r0 + LANES, :] = (qt[r0:r0 + LANES] * scale).astype(BF16)
        blk = qt[r0 + LANES:r0 + 2 * LANES]
        swapped = jnp.concatenate([blk[half:], blk[:half]], axis=0)
        qt_ref[r0 + LANES:r0 + 2 * LANES, :] = ((blk * cost + swapped * sint) * scale).astype(BF16)

    vt = _dot_nt(wvt_ref[...], ckv)
    ones = jnp.ones((MLA_V_ROWS - MLA_D_V, vt.shape[1]), BF16)
    for h in range(MLA_HEADS):
        r0 = h * MLA_V_ROWS
        vt_ref[r0:r0 + MLA_D_V, :] = vt[h * MLA_D_V:(h + 1) * MLA_D_V].astype(BF16)
        vt_ref[r0 + MLA_D_V:r0 + MLA_V_ROWS, :] = ones


def _mla_proj(x, nw, wa, qn, kvn, wqt, wkn, wvt, cos, sin, seq_len, tm):
    T, D = x.shape
    pos_blocks = seq_len // tm
    row = lambda i: (i, 0)
    col = lambda i: (0, i)
    fixed = lambda i: (0, 0)
    return pl.pallas_call(
        _mla_proj_kernel,
        grid=(T // tm,),
        in_specs=[
            pl.BlockSpec((tm, D), row),
            pl.BlockSpec((1, D), fixed),
            _resident(wa.shape, fixed),
            pl.BlockSpec((1, MLA_Q_RANK), fixed),
            pl.BlockSpec((1, MLA_KV_RANK), fixed),
            _resident(wqt.shape, fixed),
            _resident(wkn.shape, fixed),
            _resident(wvt.shape, fixed),
            pl.BlockSpec((tm, LANES), lambda i: (i % pos_blocks, 0)),
            pl.BlockSpec((tm, LANES), lambda i: (i % pos_blocks, 0)),
            pl.BlockSpec((LANES, tm), lambda i: (0, i % pos_blocks)),
            pl.BlockSpec((LANES, tm), lambda i: (0, i % pos_blocks)),
        ],
        out_specs=[
            pl.BlockSpec((MLA_HEADS * MLA_D_PAD, tm), col),
            pl.BlockSpec((tm, MLA_HEADS * MLA_D_PAD), row),
            pl.BlockSpec((MLA_HEADS * MLA_V_ROWS, tm), col),
        ],
        out_shape=[
            jax.ShapeDtypeStruct((MLA_HEADS * MLA_D_PAD, T), BF16),
            jax.ShapeDtypeStruct((T, MLA_HEADS * MLA_D_PAD), BF16),
            jax.ShapeDtypeStruct((MLA_HEADS * MLA_V_ROWS, T), BF16),
        ],
        compiler_params=_params(("parallel",)),
        name="mla_proj",
    )(x, nw, wa, qn, kvn, wqt, wkn, wvt, cos, sin, cos.T, sin.T)


ATTN_SOFTMAX_ROWS = 64
ATTN_SCORE_SLOTS = 4


def _attn_kernel(tk, tq, qt_ref, k_ref, vt_ref, ot_ref, s_ref, p_ref, acc_ref):
    n = k_ref.shape[0] // tk
    blocks = qt_ref.shape[1] // tq
    R = ATTN_SOFTMAX_ROWS

    def keys(g):
        return slice((g % n) * tk, (g % n + 1) * tk)

    def queries(g):
        return slice((g // n) * tq, (g // n + 1) * tq)

    def scores(g):
        s_ref[g % ATTN_SCORE_SLOTS] = _dot(k_ref[keys(g), :], qt_ref[:, queries(g)])

    def weighted(g, a):
        acc = a * acc_ref[g // n] + _dot(vt_ref[:, keys(g)], p_ref[g % 2])
        if g % n == n - 1:
            ot_ref[:, queries(g)] = (acc[:MLA_D_V] / acc[MLA_D_V:MLA_D_V + 1]).astype(BF16)
        else:
            acc_ref[g // n] = acc

    def softmax(g, m):
        slot = g % ATTN_SCORE_SLOTS
        if g % n == 0:
            m = jnp.full((1, tq), -jnp.inf, F32)
        mx = s_ref[slot, 0:R, :]
        for r in range(R, tk, R):
            mx = jnp.maximum(mx, s_ref[slot, r:r + R, :])
        m_new = jnp.maximum(m, jnp.max(mx, axis=0, keepdims=True))
        for r in range(0, tk, R):
            p_ref[g % 2, r:r + R, :] = jnp.exp2(s_ref[slot, r:r + R, :] - m_new).astype(BF16)
        return m_new, jnp.exp2(m - m_new)

    total = blocks * n
    acc_ref[...] = jnp.zeros_like(acc_ref)
    scores(0)
    scores(1)
    m, a = softmax(0, None)
    scores(2)
    for g in range(1, total):
        if g + 2 < total:
            scores(g + 2)
        weighted(g - 1, a)
        m, a = softmax(g, m)
    weighted(total - 1, a)


def _attention(qt, k, vt, batch, tq, tk, blocks):
    T = k.shape[0]
    L = T // batch
    nq = L // (tq * blocks)
    assert blocks * (L // tk) >= 3
    return pl.pallas_call(
        functools.partial(_attn_kernel, tk, tq),
        grid=(batch, MLA_HEADS, nq),
        in_specs=[
            pl.BlockSpec((MLA_D_PAD, blocks * tq), lambda b, h, i: (h, b * nq + i)),
            pl.BlockSpec((L, MLA_D_PAD), lambda b, h, i: (b, h)),
            pl.BlockSpec((MLA_V_ROWS, L), lambda b, h, i: (h, b)),
        ],
        out_specs=pl.BlockSpec((MLA_D_V, blocks * tq), lambda b, h, i: (h, b * nq + i)),
        out_shape=jax.ShapeDtypeStruct((MLA_HEADS * MLA_D_V, T), BF16),
        scratch_shapes=[pltpu.VMEM((ATTN_SCORE_SLOTS, tk, tq), F32), pltpu.VMEM((2, tk, tq), BF16),
                        pltpu.VMEM((blocks, MLA_V_ROWS, tq), F32)],
        compiler_params=_params(("parallel", "parallel", "arbitrary")),
        name="attention",
    )(qt, k, vt)


def _gate_lane_perm():
    lanes = jnp.arange(GDN_GATE_LANES)
    q, e = lanes // 8, lanes % 8
    return GDN_V_HEADS * (e // 2) + 2 * q + e % 2


def _gate_lane_params(a_log, dt_bias):
    lanes = jnp.arange(GDN_GATE_LANES)
    q, e = lanes // 8, lanes % 8
    head = 2 * q + e % 2
    direction = jnp.maximum(e // 2 - 2, 0)
    is_decay = e >= 4
    neg_a = jnp.where(is_decay, -jnp.exp(a_log.astype(F32))[direction, head], 0.0)
    dt = jnp.where(is_decay, dt_bias.astype(F32)[direction, head], 0.0)
    return neg_a, dt


def _pad_rope_cols(w):
    half = MLA_D_ROPE // 2
    z = jnp.zeros(w.shape[:-1] + (half,), w.dtype)
    return jnp.concatenate([w[..., :half], z, w[..., half:], z], axis=-1)


def _rope_tables(L):
    half = MLA_D_ROPE // 2
    inv_freq = ROPE_THETA ** (-jnp.arange(half, dtype=F32) / half)
    ang = jnp.arange(L, dtype=F32)[:, None] * inv_freq[None, :]
    c, s = jnp.cos(ang), jnp.sin(ang)
    z = jnp.zeros_like(c)
    return jnp.concatenate([c, z, c, z], axis=-1), jnp.concatenate([-s, z, s, z], axis=-1)


def _prepare(p):
    w = {}
    g_in = p['gdn_w_in'][0]
    w['gdn_main'] = g_in[:, :GDN_MAIN_DIM].astype(BF16)
    wg = g_in[:, GDN_MAIN_DIM:][:, _gate_lane_perm()].astype(BF16)
    w['gdn_gate'] = wg
    w['gdn_gate_t'] = wg.T
    w['gdn_neg_a'], w['gdn_dt'] = _gate_lane_params(p['gdn_a_log'][0], p['gdn_dt_bias'][0])
    w['gdn_conv'] = p['gdn_conv_w'][0].astype(F32)
    w['gdn_norm'] = p['gdn_norm_w'][0].reshape(1, GDN_DK).astype(F32)
    w['gdn_out'] = p['gdn_w_out'][0].astype(BF16)

    wa = p['mla_w_a'][0]
    rank = MLA_Q_RANK + MLA_KV_RANK
    w['mla_a'] = jnp.concatenate([wa[:, :rank], _pad_rope_cols(wa[:, rank:])], axis=-1).astype(BF16)
    wq = p['mla_w_q_b'][0].reshape(MLA_Q_RANK, MLA_HEADS, MLA_D_QK)
    wq = jnp.concatenate([wq[..., :MLA_D_NOPE], _pad_rope_cols(wq[..., MLA_D_NOPE:])], axis=-1)
    w['mla_q_t'] = wq.reshape(MLA_Q_RANK, MLA_HEADS * MLA_D_PAD).astype(BF16).T
    wkv = p['mla_w_kv_b'][0].reshape(MLA_KV_RANK, MLA_HEADS, MLA_D_NOPE + MLA_D_V)
    w['mla_kn'] = wkv[..., :MLA_D_NOPE].reshape(MLA_KV_RANK, MLA_HEADS * MLA_D_NOPE).astype(BF16)
    w['mla_v_t'] = wkv[..., MLA_D_NOPE:].reshape(MLA_KV_RANK, MLA_HEADS * MLA_D_V).astype(BF16).T
    w['mla_qn'] = p['mla_q_a_norm'][0].reshape(1, MLA_Q_RANK).astype(F32)
    w['mla_kvn'] = p['mla_kv_a_norm'][0].reshape(1, MLA_KV_RANK).astype(F32)
    w['mla_o'] = p['mla_w_o'][0].astype(BF16)

    w['ffn_in'] = p['ffn_w_in'].astype(BF16)
    w['ffn_out'] = p['ffn_w_out'].astype(BF16)
    for name in ('norm_mix_pre', 'norm_mix_post', 'norm_ffn_pre', 'norm_ffn_post'):
        w[name] = p[name].astype(F32)[:, None, :]
    return w


TILES = dict(
    proj_tm=1024, proj_tn=1024,
    gate_tr=512,
    conv_tr=512, conv_tc=1024,
    solve_chunks=16,
    sweep_chunks=4, sweep_heads=16,
    out_tm=256,
    ffn_tm=512, ffn_tf=1024,
    mla_tm=256,
    attn_tq=256, attn_tk=512, attn_items=64,
)


def _trunk(x3, w, tiles):
    B, L, D = x3.shape
    x = x3.reshape(B * L, D)

    pre, gates, gates_t = _gdn_proj(x, w['norm_mix_pre'][0], w['gdn_main'], w['gdn_gate'], w['gdn_gate_t'],
                                    tiles['proj_tm'], tiles['proj_tn'])
    gsum, gsum_t = _gdn_gates(gates, gates_t, w['gdn_neg_a'], w['gdn_dt'], tiles['gate_tr'])
    qkv = _gdn_conv(pre, w['gdn_conv'], L, tiles['conv_tr'], tiles['conv_tc'])
    solved = _gdn_delta_solve(qkv, gsum, gsum_t, tiles['solve_chunks'])
    sweep = functools.partial(_gdn_delta_sweep, gsum_t=gsum_t, batch=B, chunks=tiles['sweep_chunks'],
                              heads=tiles['sweep_heads'])
    o_fwd = sweep(False, *solved[:5])
    o = sweep(True, *solved[5:], extra=(o_fwd, pre, w['gdn_norm']))
    x = _out_proj(o, w['gdn_out'], w['norm_mix_post'][0], x, tiles['out_tm'])
    x = _ffn(x, w['norm_ffn_pre'][0], w['ffn_in'][0], w['ffn_out'][0], w['norm_ffn_post'][0],
             tiles['ffn_tm'], tiles['ffn_tf'])

    cos, sin = _rope_tables(L)
    qt, k, vt = _mla_proj(x, w['norm_mix_pre'][1], w['mla_a'], w['mla_qn'], w['mla_kvn'], w['mla_q_t'],
                          w['mla_kn'], w['mla_v_t'], cos, sin, L, tiles['mla_tm'])
    blocks = max(1, tiles['attn_items'] * tiles['attn_tk'] // L)
    ot = _attention(qt, k, vt, B, tiles['attn_tq'], tiles['attn_tk'], blocks)
    x = _out_proj(ot, w['mla_o'], w['norm_mix_post'][1], x, tiles['out_tm'], transposed=True)
    x = _ffn(x, w['norm_ffn_pre'][1], w['ffn_in'][1], w['ffn_out'][1], w['norm_ffn_post'][1],
             tiles['ffn_tm'], tiles['ffn_tf'])
    return x.reshape(B, L, D)


def kernel(x_prompt, x_sample, norm_mix_pre, norm_mix_post, norm_ffn_pre, norm_ffn_post, gdn_w_in, gdn_conv_w, gdn_a_log, gdn_dt_bias, gdn_norm_w, gdn_w_out, mla_w_a, mla_q_a_norm, mla_w_q_b, mla_kv_a_norm, mla_w_kv_b, mla_w_o, ffn_w_in, ffn_w_out):
    w = _prepare(dict(
        norm_mix_pre=norm_mix_pre, norm_mix_post=norm_mix_post, norm_ffn_pre=norm_ffn_pre,
        norm_ffn_post=norm_ffn_post, gdn_w_in=gdn_w_in, gdn_conv_w=gdn_conv_w, gdn_a_log=gdn_a_log,
        gdn_dt_bias=gdn_dt_bias, gdn_norm_w=gdn_norm_w, gdn_w_out=gdn_w_out, mla_w_a=mla_w_a,
        mla_q_a_norm=mla_q_a_norm, mla_w_q_b=mla_w_q_b, mla_kv_a_norm=mla_kv_a_norm,
        mla_w_kv_b=mla_w_kv_b, mla_w_o=mla_w_o, ffn_w_in=ffn_w_in, ffn_w_out=ffn_w_out))
    return _trunk(x_prompt, w, TILES), _trunk(x_sample, w, TILES)
```

```python
import functools
import math

import jax
import jax.numpy as jnp
from jax import lax
from jax.experimental import pallas as pl
from jax.experimental.pallas import tpu as pltpu

F32 = jnp.float32
BF16 = jnp.bfloat16

RMS_EPS = 1e-6
L2_EPS = 1e-6
LANES = 128
BF16_ROWS = 16

GDN_QK_HEADS = 16
GDN_V_HEADS = 32
GDN_DK = 128
GDN_CONV = 5
GDN_CHUNK = 64
GDN_Q_DIM = GDN_QK_HEADS * GDN_DK
GDN_V_DIM = GDN_V_HEADS * GDN_DK
GDN_CONV_DIM = 2 * GDN_Q_DIM + GDN_V_DIM
GDN_MAIN_DIM = GDN_CONV_DIM + GDN_V_DIM
GDN_GATE_LANES = 4 * GDN_V_HEADS

MLA_HEADS = 16
MLA_Q_RANK = 768
MLA_KV_RANK = 512
MLA_D_NOPE = 128
MLA_D_ROPE = 64
MLA_D_V = 128
MLA_D_QK = MLA_D_NOPE + MLA_D_ROPE
MLA_D_PAD = 2 * LANES
MLA_V_ROWS = MLA_D_V + BF16_ROWS
ROPE_THETA = 10000.0

VMEM_LIMIT = 56 * 1024 * 1024


def _params(sem):
    return pltpu.CompilerParams(dimension_semantics=sem, vmem_limit_bytes=VMEM_LIMIT)


def _resident(shape, index_map):
    return pl.BlockSpec(shape, index_map, pipeline_mode=pl.Buffered(1))


def _rms(x, w):
    return x * lax.rsqrt(jnp.mean(x * x, axis=-1, keepdims=True) + RMS_EPS) * w


def _dot(a, b):
    return jnp.dot(a, b, preferred_element_type=F32)


def _dot_nt(a, b):
    return lax.dot_general(a, b, (((1,), (1,)), ((), ())), preferred_element_type=F32)


def _dot_tn(a, b):
    return lax.dot_general(a, b, (((0,), (0,)), ((), ())), preferred_element_type=F32)


def _split3(x):
    hi = x.astype(BF16)
    r1 = x - hi.astype(F32)
    mid = r1.astype(BF16)
    lo = (r1 - mid.astype(F32)).astype(BF16)
    return hi, mid, lo


def _silu(x):
    return x * (1.0 / (1.0 + jnp.exp(-x)))


def _gdn_proj_kernel(x_ref, nw_ref, w_ref, wg_ref, wgt_ref, o_ref, g_ref, gt_ref, xn_ref):
    @pl.when(pl.program_id(1) == 0)
    def _():
        xn = _rms(x_ref[...], nw_ref[...]).astype(BF16)
        xn_ref[...] = xn
        g_ref[...] = _dot(xn, wg_ref[...])
        gt_ref[...] = _dot_nt(wgt_ref[...], xn)

    o_ref[...] = _dot(xn_ref[...], w_ref[...]).astype(BF16)


def _gdn_proj(x, nw, w, wg, wgt, tm, tn):
    T, D = x.shape
    N = w.shape[1]
    return pl.pallas_call(
        _gdn_proj_kernel,
        grid=(T // tm, N // tn),
        in_specs=[
            pl.BlockSpec((tm, D), lambda i, j: (i, 0)),
            pl.BlockSpec((1, D), lambda i, j: (0, 0)),
            pl.BlockSpec((D, tn), lambda i, j: (0, j)),
            pl.BlockSpec((D, LANES), lambda i, j: (0, 0)),
            pl.BlockSpec((LANES, D), lambda i, j: (0, 0)),
        ],
        out_specs=[
            pl.BlockSpec((tm, tn), lambda i, j: (i, j)),
            pl.BlockSpec((tm, LANES), lambda i, j: (i, 0)),
            pl.BlockSpec((LANES, tm), lambda i, j: (0, i)),
        ],
        out_shape=[
            jax.ShapeDtypeStruct((T, N), BF16),
            jax.ShapeDtypeStruct((T, LANES), F32),
            jax.ShapeDtypeStruct((LANES, T), F32),
        ],
        scratch_shapes=[pltpu.VMEM((tm, D), BF16)],
        compiler_params=_params(("parallel", "arbitrary")),
        name="gdn_proj",
    )(x, nw, w, wg, wgt)


def _softplus(y):
    return jnp.maximum(y, 0.0) + jnp.log1p(jnp.exp(-jnp.abs(y)))


def _gate_values(x, neg_a, dt):
    beta = 1.0 / (1.0 + jnp.exp(-x))
    g = neg_a * _softplus(x + dt)
    return beta, g


def _gdn_gates_kernel(x_ref, xt_ref, na_ref, dt_ref, nat_ref, dtt_ref, o_ref, ot_ref):
    R = x_ref.shape[0]
    ii = lax.broadcasted_iota(jnp.int32, (R, R), 0)
    jj = lax.broadcasted_iota(jnp.int32, (R, R), 1)
    same = (ii // GDN_CHUNK) == (jj // GDN_CHUNK)
    lower = jnp.where(same & (jj <= ii), 1.0, 0.0).astype(BF16)
    upper = jnp.where(same & (jj >= ii), 1.0, 0.0).astype(BF16)

    beta, g = _gate_values(x_ref[...], na_ref[...], dt_ref[...])
    parts = _split3(g)
    cf = sum(_dot(lower, p) for p in parts)
    cb = sum(_dot(upper, p) for p in parts)
    e = lax.broadcasted_iota(jnp.int32, (R, LANES), 1) % 8
    o_ref[...] = jnp.where(e < 4, beta, jnp.where(e < 6, cf, cb))

    beta_t, g_t = _gate_values(xt_ref[...], nat_ref[...], dtt_ref[...])
    parts_t = _split3(g_t)
    cf_t = sum(_dot(p, upper) for p in parts_t)
    cb_t = sum(_dot(p, lower) for p in parts_t)
    e_t = lax.broadcasted_iota(jnp.int32, (LANES, R), 0) % 8
    ot_ref[...] = jnp.where(e_t < 4, beta_t, jnp.where(e_t < 6, cf_t, cb_t))


def _gdn_gates(gates, gates_t, neg_a, dt, tr):
    T = gates.shape[0]
    row = lambda i: (i, 0)
    col = lambda i: (0, i)
    fixed = lambda i: (0, 0)
    return pl.pallas_call(
        _gdn_gates_kernel,
        grid=(T // tr,),
        in_specs=[
            pl.BlockSpec((tr, LANES), row),
            pl.BlockSpec((LANES, tr), col),
            pl.BlockSpec((1, LANES), fixed),
            pl.BlockSpec((1, LANES), fixed),
            pl.BlockSpec((LANES, 1), fixed),
            pl.BlockSpec((LANES, 1), fixed),
        ],
        out_specs=[pl.BlockSpec((tr, LANES), row), pl.BlockSpec((LANES, tr), col)],
        out_shape=[jax.ShapeDtypeStruct((T, LANES), F32), jax.ShapeDtypeStruct((LANES, T), F32)],
        compiler_params=_params(("parallel",)),
        name="gdn_gates",
    )(gates, gates_t, neg_a.reshape(1, LANES), dt.reshape(1, LANES),
      neg_a.reshape(LANES, 1), dt.reshape(LANES, 1))


CONV_HALO = BF16_ROWS
CONV_PAD = 8


def _gdn_conv_kernel(blocks_per_seq, n_q_blocks, n_qk_blocks,
                     prev_ref, main_ref, next_ref, w_ref, o_ref):
    i = pl.program_id(0)
    j = pl.program_id(1)
    tr, tc = main_ref.shape
    r = GDN_CONV // 2
    pos = i % blocks_per_seq
    keep_prev = jnp.where(pos == 0, 0.0, 1.0)
    keep_next = jnp.where(pos == blocks_per_seq - 1, 0.0, 1.0)
    rows = tr + 2 * CONV_PAD

    def conv_silu(cols):
        xs = jnp.concatenate([
            prev_ref[CONV_HALO - CONV_PAD:, cols].astype(F32) * keep_prev,
            main_ref[:, cols].astype(F32),
            next_ref[:CONV_PAD, cols].astype(F32) * keep_next], axis=0)
        w = w_ref[:, cols]
        acc = xs[CONV_PAD:CONV_PAD + tr] * w[r:r + 1, :]
        for t in range(GDN_CONV):
            if t != r:
                shifted = pltpu.roll(xs, (r - t) % rows, axis=0)
                acc = acc + shifted[CONV_PAD:CONV_PAD + tr] * w[t:t + 1, :]
        return _silu(acc)

    @pl.when(j >= n_qk_blocks)
    def _():
        for h in range(tc // LANES):
            cols = slice(h * LANES, (h + 1) * LANES)
            o_ref[:, cols] = conv_silu(cols).astype(BF16)

    @pl.when(j < n_qk_blocks)
    def _():
        q_scale = jnp.where(j < n_q_blocks, GDN_DK ** -0.5, 1.0)
        for h in range(tc // LANES):
            cols = slice(h * LANES, (h + 1) * LANES)
            blk = conv_silu(cols)
            inv = lax.rsqrt(jnp.sum(blk * blk, axis=-1, keepdims=True) + L2_EPS) * q_scale
            o_ref[:, cols] = (blk * inv).astype(BF16)


def _gdn_conv(pre, conv_w, seq_len, tr, tc):
    T = pre.shape[0]
    C = conv_w.shape[1]
    hb = tr // CONV_HALO
    n_halo = T // CONV_HALO
    kern = functools.partial(_gdn_conv_kernel, seq_len // tr, GDN_Q_DIM // tc, 2 * GDN_Q_DIM // tc)
    return pl.pallas_call(
        kern,
        grid=(T // tr, C // tc),
        in_specs=[
            pl.BlockSpec((CONV_HALO, tc), lambda i, j: (jnp.maximum(i * hb - 1, 0), j)),
            pl.BlockSpec((tr, tc), lambda i, j: (i, j)),
            pl.BlockSpec((CONV_HALO, tc), lambda i, j: (jnp.minimum((i + 1) * hb, n_halo - 1), j)),
            pl.BlockSpec((GDN_CONV, tc), lambda i, j: (0, j)),
        ],
        out_specs=pl.BlockSpec((tr, tc), lambda i, j: (i, j)),
        out_shape=jax.ShapeDtypeStruct((T, C), BF16),
        compiler_params=_params(("parallel", "parallel")),
        name="gdn_conv",
    )(pre, pre, pre, conv_w)


NEUMANN_DOUBLINGS = 5


def _gate_lane(direction, v_head, decay):
    return (4 if decay else 0) + 2 * direction + v_head


def _delta_solve_kernel(chunks, q_ref, k_ref, v_ref, g_ref, gt_ref,
                        uf_ref, wf_ref, pf_ref, qf_ref, kf_ref, ub_ref, wb_ref, pb_ref, qb_ref, kb_ref):
    C = GDN_CHUNK
    W = 4 * C
    qh = pl.program_id(1)
    gsel = pltpu.roll(g_ref[...], (LANES - 8 * qh) % LANES, axis=1)
    ii = lax.broadcasted_iota(jnp.int32, (C, W), 0)
    ll = lax.broadcasted_iota(jnp.int32, (C, W), 1)
    jj = ll % C
    group = ll // C
    ahead = jnp.where(group >= 2, jj - ii, ii - jj)
    incl = ahead >= 0
    strict = ahead > 0
    eye = jnp.where(ii == jj, 1.0, 0.0)
    own = [group == p for p in range(4)]
    low_half = lax.broadcasted_iota(jnp.int32, (C, LANES), 1) < C
    outs = ((uf_ref, wf_ref, pf_ref, qf_ref, kf_ref), (ub_ref, wb_ref, pb_ref, qb_ref, kb_ref))
    eye_k = jnp.where(lax.broadcasted_iota(jnp.int32, (GDN_DK, GDN_DK), 0)
                      == lax.broadcasted_iota(jnp.int32, (GDN_DK, GDN_DK), 1), 1.0, 0.0).astype(BF16)
    zeros = jnp.zeros((C, 2 * LANES), BF16)

    def side_by_side(cols):
        return jnp.concatenate([jnp.where(low_half, cols[0], cols[1]),
                                jnp.where(low_half, cols[2], cols[3])], axis=1)

    def block_diag(x):
        return jnp.concatenate([jnp.where(own[p], x, 0.0).astype(BF16) for p in range(4)], axis=0)

    ts, xs, rhs = [], [], []
    for ci in range(chunks):
        rows = slice(ci * C, (ci + 1) * C)
        kc = k_ref[rows, :]
        qc = q_ref[rows, :]
        kq = _dot_nt(jnp.concatenate([kc, qc, eye_k], axis=0), jnp.concatenate([kc] * 4, axis=0))
        betas = [jnp.broadcast_to(gsel[rows, p:p + 1], (C, LANES)) for p in range(4)]
        gccs = [jnp.broadcast_to(gsel[rows, 4 + p:5 + p], (C, LANES)) for p in range(4)]
        g_pair = gt_ref[:, (ci // 2) * 2 * C:(ci // 2 + 1) * 2 * C]
        g_swap = pltpu.roll(g_pair, C, axis=1)
        on_low, on_high = (g_pair, g_swap) if ci % 2 == 0 else (g_swap, g_pair)
        gcrs = [on_low[4 + p:5 + p, :C] for p in range(4)]
        gcr_rows = [jnp.where(low_half[0:1], on_low[4 + p:5 + p], on_high[5 + p:6 + p]) for p in (0, 2)]
        decay = jnp.concatenate([jnp.where(low_half, gccs[p], gccs[p + 1]) - gcr_rows[p // 2]
                                 for p in (0, 2)], axis=1)
        decay = jnp.where(incl, jnp.exp(jnp.where(incl, decay, 0.0)), 0.0)
        a = jnp.where(strict, side_by_side(betas) * kq[:C] * decay, 0.0)
        ts.append(eye - a)
        xs.append(a)
        pm = (kq[C:2 * C] * decay).astype(BF16)
        pf_ref[rows, :] = pm[:, :2 * C]
        pb_ref[rows, :] = pm[:, 2 * C:]
        kcf = kc.astype(F32)
        qcf = qc.astype(F32)
        for p in range(4):
            d, hv = divmod(p, 2)
            cols = slice(hv * LANES, (hv + 1) * LANES)
            gl = gcrs[p][:, 0:1] if d else gcrs[p][:, C - 1:C]
            eg = jnp.exp(gccs[p])
            outs[d][3][rows, cols] = (qcf * eg).astype(BF16)
            outs[d][4][cols, rows] = (kq[2 * C:, :C] * jnp.exp(gl - gcrs[p])).astype(BF16)
            vb = v_ref[rows, cols].astype(F32) * betas[p]
            kb = kcf * (betas[p] * eg)
            rhs.append(jnp.concatenate([vb, kb], axis=1).astype(BF16))

    xs = [_dot(x.astype(BF16), block_diag(x)) for x in xs]
    for step in range(NEUMANN_DOUBLINGS):
        if step + 1 < NEUMANN_DOUBLINGS:
            both = [_dot(jnp.concatenate([t, x], axis=0).astype(BF16), block_diag(x)) for t, x in zip(ts, xs)]
            ts = [t + b[:C] for t, b in zip(ts, both)]
            xs = [b[C:] for b in both]
        else:
            ts = [t + _dot(t.astype(BF16), block_diag(x)) for t, x in zip(ts, xs)]

    for ci in range(chunks):
        rows = slice(ci * C, (ci + 1) * C)
        for p in range(4):
            d, hv = divmod(p, 2)
            cols = slice(hv * LANES, (hv + 1) * LANES)
            lhs = jnp.where(own[p], ts[ci], 0.0).astype(BF16)
            padded = jnp.concatenate([zeros] * p + [rhs[4 * ci + p]] + [zeros] * (3 - p), axis=0)
            uw = _dot(lhs, padded)
            outs[d][0][rows, cols] = uw[:, :LANES].astype(BF16)
            outs[d][1][rows, cols] = uw[:, LANES:].astype(BF16)


def _gdn_delta_solve(qkv, gsum, gsum_t, chunks):
    T = qkv.shape[0]
    R = chunks * GDN_CHUNK
    k_block0 = GDN_Q_DIM // LANES
    v_block0 = 2 * GDN_Q_DIM // (2 * LANES)
    wide = pl.BlockSpec((R, 2 * LANES), lambda i, h: (i, h))
    narrow = pl.BlockSpec((R, LANES), lambda i, h: (i, h))
    tall = pl.BlockSpec((2 * GDN_DK, R), lambda i, h: (h, i))
    uw_shape = jax.ShapeDtypeStruct((T, GDN_V_DIM), BF16)
    p_shape = jax.ShapeDtypeStruct((T, GDN_V_HEADS * GDN_CHUNK), BF16)
    kt_shape = jax.ShapeDtypeStruct((GDN_V_HEADS * GDN_DK, T), BF16)
    return pl.pallas_call(
        functools.partial(_delta_solve_kernel, chunks),
        grid=(T // R, GDN_QK_HEADS),
        in_specs=[
            narrow,
            pl.BlockSpec((R, LANES), lambda i, h: (i, k_block0 + h)),
            pl.BlockSpec((R, 2 * LANES), lambda i, h: (i, v_block0 + h)),
            pl.BlockSpec((R, LANES), lambda i, h: (i, 0)),
            pl.BlockSpec((8, R), lambda i, h: (h, i)),
        ],
        out_specs=[wide, wide, narrow, wide, tall] * 2,
        out_shape=[uw_shape, uw_shape, p_shape, uw_shape, kt_shape] * 2,
        compiler_params=_params(("parallel", "parallel")),
        name="gdn_delta_solve",
    )(qkv, qkv, qkv, gsum, gsum_t)


def _delta_sweep_kernel(rev, chunks, heads, *refs):
    if rev:
        u_ref, w_ref, p_ref, qd_ref, kdt_ref, gt_ref, of_ref, z_ref, nw_ref, o_ref, s_ref = refs
    else:
        u_ref, w_ref, p_ref, qd_ref, kdt_ref, gt_ref, o_ref, s_ref = refs
    C = GDN_CHUNK
    d = 1 if rev else 0

    @pl.when(pl.program_id(2) == 0)
    def _():
        s_ref[...] = jnp.zeros_like(s_ref)

    states = range(2 * heads)
    zc = jnp.zeros((C, GDN_DK), BF16)
    zs = jnp.zeros((GDN_DK, GDN_DK), BF16)
    diag = lambda a, b, z: jnp.concatenate([jnp.concatenate([a, z], axis=1),
                                            jnp.concatenate([z, b], axis=1)], axis=0)
    order = range(chunks - 1, -1, -1) if rev else range(chunks)
    for ci in order:
        rows = slice(ci * C, (ci + 1) * C)
        pair = slice((ci // 2) * 2 * C, (ci // 2 + 1) * 2 * C)
        last = ci * C + (0 if rev else C - 1)
        cols = [slice(h * LANES, (h + 1) * LANES) for h in states]
        both = [slice(2 * j * LANES, (2 * j + 2) * LANES) for j in range(heads)]
        s_old = [s_ref[h] for h in states]
        wqs = [_dot(jnp.concatenate([w_ref[rows, both[j]], qd_ref[rows, both[j]]], axis=0),
                    diag(s_old[2 * j].astype(BF16), s_old[2 * j + 1].astype(BF16), zs)) for j in range(heads)]
        vns = [(u_ref[rows, both[j]].astype(F32) - wqs[j][:C]).astype(BF16) for j in range(heads)]
        vn = [vns[h // 2][:, (h % 2) * LANES:(h % 2 + 1) * LANES] for h in states]
        pvs = [_dot(p_ref[rows, j * LANES:(j + 1) * LANES], diag(vn[2 * j], vn[2 * j + 1], zc))
               for j in range(heads)]
        kvs = [_dot(kdt_ref[cols[h], pair], jnp.concatenate([zc, vn[h]] if ci % 2 else [vn[h], zc], axis=0))
               for h in states]
        for h in states:
            lg = 8 * (h // 2) + _gate_lane(d, h % 2, True)
            s_ref[h] = s_old[h] * jnp.exp(gt_ref[lg:lg + 1, last:last + 1]) + kvs[h]
            half = slice((h % 2) * LANES, (h % 2 + 1) * LANES)
            o = wqs[h // 2][C:, half] + pvs[h // 2][:, half]
            if rev:
                tot = of_ref[rows, cols[h]] + o
                gate = _silu(z_ref[rows, cols[h]].astype(F32))
                o_ref[rows, cols[h]] = (_rms(tot, nw_ref[...]) * gate).astype(BF16)
            else:
                o_ref[rows, cols[h]] = o


def _gdn_delta_sweep(rev, u, w, p, qd, kdt, gsum_t, batch, chunks, heads, extra=None):
    T = u.shape[0]
    R = chunks * GDN_CHUNK
    nb = T // batch // R
    assert chunks % 2 == 0
    rows = (lambda b, n: b * nb + (nb - 1 - n)) if rev else (lambda b, n: b * nb + n)
    qk_w = heads * LANES
    v_w = 2 * heads * LANES
    wide = pl.BlockSpec((R, v_w), lambda b, h, n: (rows(b, n), h))
    narrow = pl.BlockSpec((R, qk_w), lambda b, h, n: (rows(b, n), h))
    in_specs = [
        wide, wide, narrow, wide,
        pl.BlockSpec((v_w, R), lambda b, h, n: (h, rows(b, n))),
        pl.BlockSpec((8 * heads, R), lambda b, h, n: (h, rows(b, n))),
    ]
    args = [u, w, p, qd, kdt, gsum_t]
    if rev:
        o_fwd, pre, norm_w = extra
        z_block0 = GDN_CONV_DIM // v_w
        in_specs += [
            wide,
            pl.BlockSpec((R, v_w), lambda b, h, n: (rows(b, n), z_block0 + h)),
            pl.BlockSpec((1, LANES), lambda b, h, n: (0, 0)),
        ]
        args += [o_fwd, pre, norm_w]
    return pl.pallas_call(
        functools.partial(_delta_sweep_kernel, rev, chunks, heads),
        grid=(batch, GDN_QK_HEADS // heads, nb),
        in_specs=in_specs,
        out_specs=wide,
        out_shape=jax.ShapeDtypeStruct((T, GDN_V_DIM), BF16 if rev else F32),
        scratch_shapes=[pltpu.VMEM((2 * heads, GDN_DK, GDN_DK), F32)],
        compiler_params=_params(("parallel", "parallel", "arbitrary")),
        name="gdn_sweep_bwd" if rev else "gdn_sweep_fwd",
    )(*args)


def _out_proj_kernel(transposed, a_ref, w_ref, nw_ref, x_ref, o_ref):
    m = _dot_tn(a_ref[...], w_ref[...]) if transposed else _dot(a_ref[...], w_ref[...])
    o_ref[...] = x_ref[...] + _rms(m, nw_ref[...])


def _out_proj(a, w, nw, x, tm, transposed=False):
    K, D = w.shape
    T = x.shape[0]
    a_spec = pl.BlockSpec((K, tm), lambda i: (0, i)) if transposed else pl.BlockSpec((tm, K), lambda i: (i, 0))
    return pl.pallas_call(
        functools.partial(_out_proj_kernel, transposed),
        grid=(T // tm,),
        in_specs=[
            a_spec,
            _resident((K, D), lambda i: (0, 0)),
            pl.BlockSpec((1, D), lambda i: (0, 0)),
            pl.BlockSpec((tm, D), lambda i: (i, 0)),
        ],
        out_specs=pl.BlockSpec((tm, D), lambda i: (i, 0)),
        out_shape=jax.ShapeDtypeStruct((T, D), F32),
        compiler_params=_params(("parallel",)),
        name="out_proj_t" if transposed else "out_proj",
    )(a, w, nw, x)


def _ffn_kernel(x_ref, nw1_ref, w1_ref, w2_ref, nw2_ref, o_ref, xn_ref, acc_ref):
    j = pl.program_id(1)

    @pl.when(j == 0)
    def _():
        xn_ref[...] = _rms(x_ref[...], nw1_ref[...]).astype(BF16)
        acc_ref[...] = jnp.zeros_like(acc_ref)

    h = jnp.maximum(_dot(xn_ref[...], w1_ref[...]), 0.0)
    acc_ref[...] += _dot((h * h).astype(BF16), w2_ref[...])

    @pl.when(j == pl.num_programs(1) - 1)
    def _():
        o_ref[...] = x_ref[...] + _rms(acc_ref[...], nw2_ref[...])


def _ffn(x, nw1, w1, w2, nw2, tm, tf):
    T, D = x.shape
    Fd = w1.shape[1]
    return pl.pallas_call(
        _ffn_kernel,
        grid=(T // tm, Fd // tf),
        in_specs=[
            pl.BlockSpec((tm, D), lambda i, j: (i, 0)),
            pl.BlockSpec((1, D), lambda i, j: (0, 0)),
            pl.BlockSpec((D, tf), lambda i, j: (0, j)),
            pl.BlockSpec((tf, D), lambda i, j: (j, 0)),
            pl.BlockSpec((1, D), lambda i, j: (0, 0)),
        ],
        out_specs=pl.BlockSpec((tm, D), lambda i, j: (i, 0)),
        out_shape=jax.ShapeDtypeStruct((T, D), F32),
        scratch_shapes=[pltpu.VMEM((tm, D), BF16), pltpu.VMEM((tm, D), F32)],
        compiler_params=_params(("parallel", "arbitrary")),
        name="ffn",
    )(x, nw1, w1, w2, nw2)


def _mla_proj_kernel(x_ref, nw_ref, wa_ref, qn_ref, kvn_ref, wqt_ref, wkn_ref, wvt_ref,
                     cos_ref, sin_ref, cost_ref, sint_ref, qt_ref, k_ref, vt_ref):
    xn = _rms(x_ref[...], nw_ref[...]).astype(BF16)
    a = _dot(xn, wa_ref[...])
    cq = _rms(a[:, :MLA_Q_RANK], qn_ref[...]).astype(BF16)
    ckv = _rms(a[:, MLA_Q_RANK:MLA_Q_RANK + MLA_KV_RANK], kvn_ref[...]).astype(BF16)

    kr = a[:, MLA_Q_RANK + MLA_KV_RANK:]
    k_rope = (kr * cos_ref[...] + pltpu.roll(kr, LANES // 2, axis=1) * sin_ref[...]).astype(BF16)
    kn = _dot(ckv, wkn_ref[...])
    for h in range(MLA_HEADS):
        c0 = h * MLA_D_PAD
        k_ref[:, c0:c0 + LANES] = kn[:, h * LANES:(h + 1) * LANES].astype(BF16)
        k_ref[:, c0 + LANES:c0 + 2 * LANES] = k_rope

    scale = MLA_D_QK ** -0.5 * math.log2(math.e)
    qt = _dot_nt(wqt_ref[...], cq)
    cost = cost_ref[...]
    sint = sint_ref[...]
    half = LANES // 2
    for h in range(MLA_HEADS):
        r0 = h * MLA_D_PAD
        qt_ref[r0:r0 + LANES, :] = (qt[r0:r0 + LANES] * scale).astype(BF16)
        blk = qt[r0 + LANES:r0 + 2 * LANES]
        swapped = jnp.concatenate([blk[half:], blk[:half]], axis=0)
        qt_ref[r0 + LANES:r0 + 2 * LANES, :] = ((blk * cost + swapped * sint) * scale).astype(BF16)

    vt = _dot_nt(wvt_ref[...], ckv)
    ones = jnp.ones((MLA_V_ROWS - MLA_D_V, vt.shape[1]), BF16)
    for h in range(MLA_HEADS):
        r0 = h * MLA_V_ROWS
        vt_ref[r0:r0 + MLA_D_V, :] = vt[h * MLA_D_V:(h + 1) * MLA_D_V].astype(BF16)
        vt_ref[r0 + MLA_D_V:r0 + MLA_V_ROWS, :] = ones


def _mla_proj(x, nw, wa, qn, kvn, wqt, wkn, wvt, cos, sin, seq_len, tm):
    T, D = x.shape
    pos_blocks = seq_len // tm
    row = lambda i: (i, 0)
    col = lambda i: (0, i)
    fixed = lambda i: (0, 0)
    return pl.pallas_call(
        _mla_proj_kernel,
        grid=(T // tm,),
        in_specs=[
            pl.BlockSpec((tm, D), row),
            pl.BlockSpec((1, D), fixed),
            _resident(wa.shape, fixed),
            pl.BlockSpec((1, MLA_Q_RANK), fixed),
            pl.BlockSpec((1, MLA_KV_RANK), fixed),
            _resident(wqt.shape, fixed),
            _resident(wkn.shape, fixed),
            _resident(wvt.shape, fixed),
            pl.BlockSpec((tm, LANES), lambda i: (i % pos_blocks, 0)),
            pl.BlockSpec((tm, LANES), lambda i: (i % pos_blocks, 0)),
            pl.BlockSpec((LANES, tm), lambda i: (0, i % pos_blocks)),
            pl.BlockSpec((LANES, tm), lambda i: (0, i % pos_blocks)),
        ],
        out_specs=[
            pl.BlockSpec((MLA_HEADS * MLA_D_PAD, tm), col),
            pl.BlockSpec((tm, MLA_HEADS * MLA_D_PAD), row),
            pl.BlockSpec((MLA_HEADS * MLA_V_ROWS, tm), col),
        ],
        out_shape=[
            jax.ShapeDtypeStruct((MLA_HEADS * MLA_D_PAD, T), BF16),
            jax.ShapeDtypeStruct((T, MLA_HEADS * MLA_D_PAD), BF16),
            jax.ShapeDtypeStruct((MLA_HEADS * MLA_V_ROWS, T), BF16),
        ],
        compiler_params=_params(("parallel",)),
        name="mla_proj",
    )(x, nw, wa, qn, kvn, wqt, wkn, wvt, cos, sin, cos.T, sin.T)


ATTN_SOFTMAX_ROWS = 64
ATTN_SCORE_SLOTS = 4


def _attn_kernel(tk, tq, qt_ref, k_ref, vt_ref, ot_ref, s_ref, p_ref, acc_ref):
    n = k_ref.shape[0] // tk
    blocks = qt_ref.shape[1] // tq
    R = ATTN_SOFTMAX_ROWS

    def keys(g):
        return slice((g % n) * tk, (g % n + 1) * tk)

    def queries(g):
        return slice((g // n) * tq, (g // n + 1) * tq)

    def scores(g):
        s_ref[g % ATTN_SCORE_SLOTS] = _dot(k_ref[keys(g), :], qt_ref[:, queries(g)])

    def weighted(g, a):
        acc = a * acc_ref[g // n] + _dot(vt_ref[:, keys(g)], p_ref[g % 2])
        if g % n == n - 1:
            ot_ref[:, queries(g)] = (acc[:MLA_D_V] / acc[MLA_D_V:MLA_D_V + 1]).astype(BF16)
        else:
            acc_ref[g // n] = acc

    def softmax(g, m):
        slot = g % ATTN_SCORE_SLOTS
        if g % n == 0:
            m = jnp.full((1, tq), -jnp.inf, F32)
        mx = s_ref[slot, 0:R, :]
        for r in range(R, tk, R):
            mx = jnp.maximum(mx, s_ref[slot, r:r + R, :])
        m_new = jnp.maximum(m, jnp.max(mx, axis=0, keepdims=True))
        for r in range(0, tk, R):
            p_ref[g % 2, r:r + R, :] = jnp.exp2(s_ref[slot, r:r + R, :] - m_new).astype(BF16)
        return m_new, jnp.exp2(m - m_new)

    total = blocks * n
    acc_ref[...] = jnp.zeros_like(acc_ref)
    scores(0)
    scores(1)
    m, a = softmax(0, None)
    scores(2)
    for g in range(1, total):
        if g + 2 < total:
            scores(g + 2)
        weighted(g - 1, a)
        m, a = softmax(g, m)
    weighted(total - 1, a)


def _attention(qt, k, vt, batch, tq, tk, blocks):
    T = k.shape[0]
    L = T // batch
    nq = L // (tq * blocks)
    assert blocks * (L // tk) >= 3
    return pl.pallas_call(
        functools.partial(_attn_kernel, tk, tq),
        grid=(batch, MLA_HEADS, nq),
        in_specs=[
            pl.BlockSpec((MLA_D_PAD, blocks * tq), lambda b, h, i: (h, b * nq + i)),
            pl.BlockSpec((L, MLA_D_PAD), lambda b, h, i: (b, h)),
            pl.BlockSpec((MLA_V_ROWS, L), lambda b, h, i: (h, b)),
        ],
        out_specs=pl.BlockSpec((MLA_D_V, blocks * tq), lambda b, h, i: (h, b * nq + i)),
        out_shape=jax.ShapeDtypeStruct((MLA_HEADS * MLA_D_V, T), BF16),
        scratch_shapes=[pltpu.VMEM((ATTN_SCORE_SLOTS, tk, tq), F32), pltpu.VMEM((2, tk, tq), BF16),
                        pltpu.VMEM((blocks, MLA_V_ROWS, tq), F32)],
        compiler_params=_params(("parallel", "parallel", "arbitrary")),
        name="attention",
    )(qt, k, vt)


def _gate_lane_perm():
    lanes = jnp.arange(GDN_GATE_LANES)
    q, e = lanes // 8, lanes % 8
    return GDN_V_HEADS * (e // 2) + 2 * q + e % 2


def _gate_lane_params(a_log, dt_bias):
    lanes = jnp.arange(GDN_GATE_LANES)
    q, e = lanes // 8, lanes % 8
    head = 2 * q + e % 2
    direction = jnp.maximum(e // 2 - 2, 0)
    is_decay = e >= 4
    neg_a = jnp.where(is_decay, -jnp.exp(a_log.astype(F32))[direction, head], 0.0)
    dt = jnp.where(is_decay, dt_bias.astype(F32)[direction, head], 0.0)
    return neg_a, dt


def _pad_rope_cols(w):
    half = MLA_D_ROPE // 2
    z = jnp.zeros(w.shape[:-1] + (half,), w.dtype)
    return jnp.concatenate([w[..., :half], z, w[..., half:], z], axis=-1)


def _rope_tables(L):
    half = MLA_D_ROPE // 2
    inv_freq = ROPE_THETA ** (-jnp.arange(half, dtype=F32) / half)
    ang = jnp.arange(L, dtype=F32)[:, None] * inv_freq[None, :]
    c, s = jnp.cos(ang), jnp.sin(ang)
    z = jnp.zeros_like(c)
    return jnp.concatenate([c, z, c, z], axis=-1), jnp.concatenate([-s, z, s, z], axis=-1)


def _prepare(p):
    w = {}
    g_in = p['gdn_w_in'][0]
    w['gdn_main'] = g_in[:, :GDN_MAIN_DIM].astype(BF16)
    wg = g_in[:, GDN_MAIN_DIM:][:, _gate_lane_perm()].astype(BF16)
    w['gdn_gate'] = wg
    w['gdn_gate_t'] = wg.T
    w['gdn_neg_a'], w['gdn_dt'] = _gate_lane_params(p['gdn_a_log'][0], p['gdn_dt_bias'][0])
    w['gdn_conv'] = p['gdn_conv_w'][0].astype(F32)
    w['gdn_norm'] = p['gdn_norm_w'][0].reshape(1, GDN_DK).astype(F32)
    w['gdn_out'] = p['gdn_w_out'][0].astype(BF16)

    wa = p['mla_w_a'][0]
    rank = MLA_Q_RANK + MLA_KV_RANK
    w['mla_a'] = jnp.concatenate([wa[:, :rank], _pad_rope_cols(wa[:, rank:])], axis=-1).astype(BF16)
    wq = p['mla_w_q_b'][0].reshape(MLA_Q_RANK, MLA_HEADS, MLA_D_QK)
    wq = jnp.concatenate([wq[..., :MLA_D_NOPE], _pad_rope_cols(wq[..., MLA_D_NOPE:])], axis=-1)
    w['mla_q_t'] = wq.reshape(MLA_Q_RANK, MLA_HEADS * MLA_D_PAD).astype(BF16).T
    wkv = p['mla_w_kv_b'][0].reshape(MLA_KV_RANK, MLA_HEADS, MLA_D_NOPE + MLA_D_V)
    w['mla_kn'] = wkv[..., :MLA_D_NOPE].reshape(MLA_KV_RANK, MLA_HEADS * MLA_D_NOPE).astype(BF16)
    w['mla_v_t'] = wkv[..., MLA_D_NOPE:].reshape(MLA_KV_RANK, MLA_HEADS * MLA_D_V).astype(BF16).T
    w['mla_qn'] = p['mla_q_a_norm'][0].reshape(1, MLA_Q_RANK).astype(F32)
    w['mla_kvn'] = p['mla_kv_a_norm'][0].reshape(1, MLA_KV_RANK).astype(F32)
    w['mla_o'] = p['mla_w_o'][0].astype(BF16)

    w['ffn_in'] = p['ffn_w_in'].astype(BF16)
    w['ffn_out'] = p['ffn_w_out'].astype(BF16)
    for name in ('norm_mix_pre', 'norm_mix_post', 'norm_ffn_pre', 'norm_ffn_post'):
        w[name] = p[name].astype(F32)[:, None, :]
    return w


TILES = dict(
    proj_tm=1024, proj_tn=2048,
    gate_tr=512,
    conv_tr=512, conv_tc=1024,
    solve_chunks=16,
    sweep_chunks=4, sweep_heads=16,
    out_tm=512,
    ffn_tm=512, ffn_tf=1024,
    mla_tm=512,
    attn_tq=256, attn_tk=512, attn_items=64,
)


def _trunk(x3, w, tiles):
    B, L, D = x3.shape
    x = x3.reshape(B * L, D)

    pre, gates, gates_t = _gdn_proj(x, w['norm_mix_pre'][0], w['gdn_main'], w['gdn_gate'], w['gdn_gate_t'],
                                    tiles['proj_tm'], tiles['proj_tn'])
    gsum, gsum_t = _gdn_gates(gates, gates_t, w['gdn_neg_a'], w['gdn_dt'], tiles['gate_tr'])
    qkv = _gdn_conv(pre, w['gdn_conv'], L, tiles['conv_tr'], tiles['conv_tc'])
    solved = _gdn_delta_solve(qkv, gsum, gsum_t, tiles['solve_chunks'])
    sweep = functools.partial(_gdn_delta_sweep, gsum_t=gsum_t, batch=B, chunks=tiles['sweep_chunks'],
                              heads=tiles['sweep_heads'])
    o_fwd = sweep(False, *solved[:5])
    o = sweep(True, *solved[5:], extra=(o_fwd, pre, w['gdn_norm']))
    x = _out_proj(o, w['gdn_out'], w['norm_mix_post'][0], x, tiles['out_tm'])
    x = _ffn(x, w['norm_ffn_pre'][0], w['ffn_in'][0], w['ffn_out'][0], w['norm_ffn_post'][0],
             tiles['ffn_tm'], tiles['ffn_tf'])

    cos, sin = _rope_tables(L)
    qt, k, vt = _mla_proj(x, w['norm_mix_pre'][1], w['mla_a'], w['mla_qn'], w['mla_kvn'], w['mla_q_t'],
                          w['mla_kn'], w['mla_v_t'], cos, sin, L, tiles['mla_tm'])
    blocks = max(1, tiles['attn_items'] * tiles['attn_tk'] // L)
    ot = _attention(qt, k, vt, B, tiles['attn_tq'], tiles['attn_tk'], blocks)
    x = _out_proj(ot, w['mla_o'], w['norm_mix_post'][1], x, tiles['out_tm'], transposed=True)
    x = _ffn(x, w['norm_ffn_pre'][1], w['ffn_in'][1], w['ffn_out'][1], w['norm_ffn_post'][1],
             tiles['ffn_tm'], tiles['ffn_tf'])
    return x.reshape(B, L, D)


def kernel(x_prompt, x_sample, norm_mix_pre, norm_mix_post, norm_ffn_pre, norm_ffn_post, gdn_w_in, gdn_conv_w, gdn_a_log, gdn_dt_bias, gdn_norm_w, gdn_w_out, mla_w_a, mla_q_a_norm, mla_w_q_b, mla_kv_a_norm, mla_w_kv_b, mla_w_o, ffn_w_in, ffn_w_out):
    w = _prepare(dict(
        norm_mix_pre=norm_mix_pre, norm_mix_post=norm_mix_post, norm_ffn_pre=norm_ffn_pre,
        norm_ffn_post=norm_ffn_post, gdn_w_in=gdn_w_in, gdn_conv_w=gdn_conv_w, gdn_a_log=gdn_a_log,
        gdn_dt_bias=gdn_dt_bias, gdn_norm_w=gdn_norm_w, gdn_w_out=gdn_w_out, mla_w_a=mla_w_a,
        mla_q_a_norm=mla_q_a_norm, mla_w_q_b=mla_w_q_b, mla_kv_a_norm=mla_kv_a_norm,
        mla_w_kv_b=mla_w_kv_b, mla_w_o=mla_w_o, ffn_w_in=ffn_w_in, ffn_w_out=ffn_w_out))
    return _trunk(x_prompt, w, TILES), _trunk(x_sample, w, TILES)
```

```python
import functools
import math

import jax
import jax.numpy as jnp
from jax import lax
from jax.experimental import pallas as pl
from jax.experimental.pallas import tpu as pltpu

F32 = jnp.float32
BF16 = jnp.bfloat16

RMS_EPS = 1e-6
L2_EPS = 1e-6
LANES = 128
BF16_ROWS = 16

GDN_QK_HEADS = 16
GDN_V_HEADS = 32
GDN_DK = 128
GDN_CONV = 5
GDN_CHUNK = 64
GDN_Q_DIM = GDN_QK_HEADS * GDN_DK
GDN_V_DIM = GDN_V_HEADS * GDN_DK
GDN_CONV_DIM = 2 * GDN_Q_DIM + GDN_V_DIM
GDN_MAIN_DIM = GDN_CONV_DIM + GDN_V_DIM
GDN_GATE_LANES = 4 * GDN_V_HEADS

MLA_HEADS = 16
MLA_Q_RANK = 768
MLA_KV_RANK = 512
MLA_D_NOPE = 128
MLA_D_ROPE = 64
MLA_D_V = 128
MLA_D_QK = MLA_D_NOPE + MLA_D_ROPE
MLA_D_PAD = 2 * LANES
MLA_V_ROWS = MLA_D_V + BF16_ROWS
ROPE_THETA = 10000.0

VMEM_LIMIT = 56 * 1024 * 1024


def _params(sem):
    return pltpu.CompilerParams(dimension_semantics=sem, vmem_limit_bytes=VMEM_LIMIT)


def _resident(shape, index_map):
    return pl.BlockSpec(shape, index_map, pipeline_mode=pl.Buffered(1))


def _rms(x, w):
    return x * lax.rsqrt(jnp.mean(x * x, axis=-1, keepdims=True) + RMS_EPS) * w


def _dot(a, b):
    return jnp.dot(a, b, preferred_element_type=F32)


def _dot_nt(a, b):
    return lax.dot_general(a, b, (((1,), (1,)), ((), ())), preferred_element_type=F32)


def _dot_tn(a, b):
    return lax.dot_general(a, b, (((0,), (0,)), ((), ())), preferred_element_type=F32)


def _split3(x):
    hi = x.astype(BF16)
    r1 = x - hi.astype(F32)
    mid = r1.astype(BF16)
    lo = (r1 - mid.astype(F32)).astype(BF16)
    return hi, mid, lo


def _silu(x):
    return x * (1.0 / (1.0 + jnp.exp(-x)))


def _gdn_proj_kernel(x_ref, nw_ref, w_ref, wg_ref, wgt_ref, o_ref, g_ref, gt_ref, xn_ref):
    @pl.when(pl.program_id(1) == 0)
    def _():
        xn = _rms(x_ref[...], nw_ref[...]).astype(BF16)
        xn_ref[...] = xn
        g_ref[...] = _dot(xn, wg_ref[...])
        gt_ref[...] = _dot_nt(wgt_ref[...], xn)

    o_ref[...] = _dot(xn_ref[...], w_ref[...]).astype(BF16)


def _gdn_proj(x, nw, w, wg, wgt, tm, tn):
    T, D = x.shape
    N = w.shape[1]
    return pl.pallas_call(
        _gdn_proj_kernel,
        grid=(T // tm, N // tn),
        in_specs=[
            pl.BlockSpec((tm, D), lambda i, j: (i, 0)),
            pl.BlockSpec((1, D), lambda i, j: (0, 0)),
            pl.BlockSpec((D, tn), lambda i, j: (0, j)),
            pl.BlockSpec((D, LANES), lambda i, j: (0, 0)),
            pl.BlockSpec((LANES, D), lambda i, j: (0, 0)),
        ],
        out_specs=[
            pl.BlockSpec((tm, tn), lambda i, j: (i, j)),
            pl.BlockSpec((tm, LANES), lambda i, j: (i, 0)),
            pl.BlockSpec((LANES, tm), lambda i, j: (0, i)),
        ],
        out_shape=[
            jax.ShapeDtypeStruct((T, N), BF16),
            jax.ShapeDtypeStruct((T, LANES), F32),
            jax.ShapeDtypeStruct((LANES, T), F32),
        ],
        scratch_shapes=[pltpu.VMEM((tm, D), BF16)],
        compiler_params=_params(("parallel", "arbitrary")),
        name="gdn_proj",
    )(x, nw, w, wg, wgt)


def _softplus(y):
    return jnp.maximum(y, 0.0) + jnp.log1p(jnp.exp(-jnp.abs(y)))


def _gate_values(x, neg_a, dt):
    beta = 1.0 / (1.0 + jnp.exp(-x))
    g = neg_a * _softplus(x + dt)
    return beta, g


def _gdn_gates_kernel(x_ref, xt_ref, na_ref, dt_ref, nat_ref, dtt_ref, o_ref, ot_ref):
    R = x_ref.shape[0]
    ii = lax.broadcasted_iota(jnp.int32, (R, R), 0)
    jj = lax.broadcasted_iota(jnp.int32, (R, R), 1)
    same = (ii // GDN_CHUNK) == (jj // GDN_CHUNK)
    lower = jnp.where(same & (jj <= ii), 1.0, 0.0).astype(BF16)
    upper = jnp.where(same & (jj >= ii), 1.0, 0.0).astype(BF16)

    beta, g = _gate_values(x_ref[...], na_ref[...], dt_ref[...])
    parts = _split3(g)
    cf = sum(_dot(lower, p) for p in parts)
    cb = sum(_dot(upper, p) for p in parts)
    e = lax.broadcasted_iota(jnp.int32, (R, LANES), 1) % 8
    o_ref[...] = jnp.where(e < 4, beta, jnp.where(e < 6, cf, cb))

    beta_t, g_t = _gate_values(xt_ref[...], nat_ref[...], dtt_ref[...])
    parts_t = _split3(g_t)
    cf_t = sum(_dot(p, upper) for p in parts_t)
    cb_t = sum(_dot(p, lower) for p in parts_t)
    e_t = lax.broadcasted_iota(jnp.int32, (LANES, R), 0) % 8
    ot_ref[...] = jnp.where(e_t < 4, beta_t, jnp.where(e_t < 6, cf_t, cb_t))


def _gdn_gates(gates, gates_t, neg_a, dt, tr):
    T = gates.shape[0]
    row = lambda i: (i, 0)
    col = lambda i: (0, i)
    fixed = lambda i: (0, 0)
    return pl.pallas_call(
        _gdn_gates_kernel,
        grid=(T // tr,),
        in_specs=[
            pl.BlockSpec((tr, LANES), row),
            pl.BlockSpec((LANES, tr), col),
            pl.BlockSpec((1, LANES), fixed),
            pl.BlockSpec((1, LANES), fixed),
            pl.BlockSpec((LANES, 1), fixed),
            pl.BlockSpec((LANES, 1), fixed),
        ],
        out_specs=[pl.BlockSpec((tr, LANES), row), pl.BlockSpec((LANES, tr), col)],
        out_shape=[jax.ShapeDtypeStruct((T, LANES), F32), jax.ShapeDtypeStruct((LANES, T), F32)],
        compiler_params=_params(("parallel",)),
        name="gdn_gates",
    )(gates, gates_t, neg_a.reshape(1, LANES), dt.reshape(1, LANES),
      neg_a.reshape(LANES, 1), dt.reshape(LANES, 1))


CONV_HALO = BF16_ROWS
CONV_PAD = 8


def _gdn_conv_kernel(blocks_per_seq, n_q_blocks, n_qk_blocks,
                     prev_ref, main_ref, next_ref, w_ref, o_ref):
    i = pl.program_id(0)
    j = pl.program_id(1)
    tr, tc = main_ref.shape
    r = GDN_CONV // 2
    pos = i % blocks_per_seq
    keep_prev = jnp.where(pos == 0, 0.0, 1.0)
    keep_next = jnp.where(pos == blocks_per_seq - 1, 0.0, 1.0)
    rows = tr + 2 * CONV_PAD

    def conv_silu(cols):
        xs = jnp.concatenate([
            prev_ref[CONV_HALO - CONV_PAD:, cols].astype(F32) * keep_prev,
            main_ref[:, cols].astype(F32),
            next_ref[:CONV_PAD, cols].astype(F32) * keep_next], axis=0)
        w = w_ref[:, cols]
        acc = xs[CONV_PAD:CONV_PAD + tr] * w[r:r + 1, :]
        for t in range(GDN_CONV):
            if t != r:
                shifted = pltpu.roll(xs, (r - t) % rows, axis=0)
                acc = acc + shifted[CONV_PAD:CONV_PAD + tr] * w[t:t + 1, :]
        return _silu(acc)

    @pl.when(j >= n_qk_blocks)
    def _():
        for h in range(tc // LANES):
            cols = slice(h * LANES, (h + 1) * LANES)
            o_ref[:, cols] = conv_silu(cols).astype(BF16)

    @pl.when(j < n_qk_blocks)
    def _():
        q_scale = jnp.where(j < n_q_blocks, GDN_DK ** -0.5, 1.0)
        for h in range(tc // LANES):
            cols = slice(h * LANES, (h + 1) * LANES)
            blk = conv_silu(cols)
            inv = lax.rsqrt(jnp.sum(blk * blk, axis=-1, keepdims=True) + L2_EPS) * q_scale
            o_ref[:, cols] = (blk * inv).astype(BF16)


def _gdn_conv(pre, conv_w, seq_len, tr, tc):
    T = pre.shape[0]
    C = conv_w.shape[1]
    hb = tr // CONV_HALO
    n_halo = T // CONV_HALO
    kern = functools.partial(_gdn_conv_kernel, seq_len // tr, GDN_Q_DIM // tc, 2 * GDN_Q_DIM // tc)
    return pl.pallas_call(
        kern,
        grid=(T // tr, C // tc),
        in_specs=[
            pl.BlockSpec((CONV_HALO, tc), lambda i, j: (jnp.maximum(i * hb - 1, 0), j)),
            pl.BlockSpec((tr, tc), lambda i, j: (i, j)),
            pl.BlockSpec((CONV_HALO, tc), lambda i, j: (jnp.minimum((i + 1) * hb, n_halo - 1), j)),
            pl.BlockSpec((GDN_CONV, tc), lambda i, j: (0, j)),
        ],
        out_specs=pl.BlockSpec((tr, tc), lambda i, j: (i, j)),
        out_shape=jax.ShapeDtypeStruct((T, C), BF16),
        compiler_params=_params(("parallel", "parallel")),
        name="gdn_conv",
    )(pre, pre, pre, conv_w)


NEUMANN_DOUBLINGS = 5


def _gate_lane(direction, v_head, decay):
    return (4 if decay else 0) + 2 * direction + v_head


def _delta_solve_kernel(chunks, q_ref, k_ref, v_ref, g_ref, gt_ref,
                        uf_ref, wf_ref, pf_ref, qf_ref, kf_ref, ub_ref, wb_ref, pb_ref, qb_ref, kb_ref):
    C = GDN_CHUNK
    W = 4 * C
    qh = pl.program_id(1)
    gsel = pltpu.roll(g_ref[...], (LANES - 8 * qh) % LANES, axis=1)
    ii = lax.broadcasted_iota(jnp.int32, (C, W), 0)
    ll = lax.broadcasted_iota(jnp.int32, (C, W), 1)
    jj = ll % C
    group = ll // C
    ahead = jnp.where(group >= 2, jj - ii, ii - jj)
    incl = ahead >= 0
    strict = ahead > 0
    eye = jnp.where(ii == jj, 1.0, 0.0)
    own = [group == p for p in range(4)]
    low_half = lax.broadcasted_iota(jnp.int32, (C, LANES), 1) < C
    outs = ((uf_ref, wf_ref, pf_ref, qf_ref, kf_ref), (ub_ref, wb_ref, pb_ref, qb_ref, kb_ref))
    eye_k = jnp.where(lax.broadcasted_iota(jnp.int32, (GDN_DK, GDN_DK), 0)
                      == lax.broadcasted_iota(jnp.int32, (GDN_DK, GDN_DK), 1), 1.0, 0.0).astype(BF16)
    zeros = jnp.zeros((C, 2 * LANES), BF16)

    def side_by_side(cols):
        return jnp.concatenate([jnp.where(low_half, cols[0], cols[1]),
                                jnp.where(low_half, cols[2], cols[3])], axis=1)

    def block_diag(x):
        return jnp.concatenate([jnp.where(own[p], x, 0.0).astype(BF16) for p in range(4)], axis=0)

    ts, xs, rhs = [], [], []
    for ci in range(chunks):
        rows = slice(ci * C, (ci + 1) * C)
        kc = k_ref[rows, :]
        qc = q_ref[rows, :]
        kq = _dot_nt(jnp.concatenate([kc, qc, eye_k], axis=0), jnp.concatenate([kc] * 4, axis=0))
        betas = [jnp.broadcast_to(gsel[rows, p:p + 1], (C, LANES)) for p in range(4)]
        gccs = [jnp.broadcast_to(gsel[rows, 4 + p:5 + p], (C, LANES)) for p in range(4)]
        g_pair = gt_ref[:, (ci // 2) * 2 * C:(ci // 2 + 1) * 2 * C]
        g_swap = pltpu.roll(g_pair, C, axis=1)
        on_low, on_high = (g_pair, g_swap) if ci % 2 == 0 else (g_swap, g_pair)
        gcrs = [on_low[4 + p:5 + p, :C] for p in range(4)]
        gcr_rows = [jnp.where(low_half[0:1], on_low[4 + p:5 + p], on_high[5 + p:6 + p]) for p in (0, 2)]
        decay = jnp.concatenate([jnp.where(low_half, gccs[p], gccs[p + 1]) - gcr_rows[p // 2]
                                 for p in (0, 2)], axis=1)
        decay = jnp.where(incl, jnp.exp(jnp.where(incl, decay, 0.0)), 0.0)
        a = jnp.where(strict, side_by_side(betas) * kq[:C] * decay, 0.0)
        ts.append(eye - a)
        xs.append(a)
        pm = (kq[C:2 * C] * decay).astype(BF16)
        pf_ref[rows, :] = pm[:, :2 * C]
        pb_ref[rows, :] = pm[:, 2 * C:]
        kcf = kc.astype(F32)
        qcf = qc.astype(F32)
        for p in range(4):
            d, hv = divmod(p, 2)
            cols = slice(hv * LANES, (hv + 1) * LANES)
            gl = gcrs[p][:, 0:1] if d else gcrs[p][:, C - 1:C]
            eg = jnp.exp(gccs[p])
            outs[d][3][rows, cols] = (qcf * eg).astype(BF16)
            outs[d][4][ci // 2, cols, (ci % 2) * C:(ci % 2 + 1) * C] = (
                kq[2 * C:, :C] * jnp.exp(gl - gcrs[p])).astype(BF16)
            vb = v_ref[rows, cols].astype(F32) * betas[p]
            kb = kcf * (betas[p] * eg)
            rhs.append(jnp.concatenate([vb, kb], axis=1).astype(BF16))

    xs = [_dot(x.astype(BF16), block_diag(x)) for x in xs]
    for step in range(NEUMANN_DOUBLINGS):
        if step + 1 < NEUMANN_DOUBLINGS:
            both = [_dot(jnp.concatenate([t, x], axis=0).astype(BF16), block_diag(x)) for t, x in zip(ts, xs)]
            ts = [t + b[:C] for t, b in zip(ts, both)]
            xs = [b[C:] for b in both]
        else:
            ts = [t + _dot(t.astype(BF16), block_diag(x)) for t, x in zip(ts, xs)]

    for ci in range(chunks):
        rows = slice(ci * C, (ci + 1) * C)
        for p in range(4):
            d, hv = divmod(p, 2)
            cols = slice(hv * LANES, (hv + 1) * LANES)
            lhs = jnp.where(own[p], ts[ci], 0.0).astype(BF16)
            padded = jnp.concatenate([zeros] * p + [rhs[4 * ci + p]] + [zeros] * (3 - p), axis=0)
            uw = _dot(lhs, padded)
            outs[d][0][rows, cols] = uw[:, :LANES].astype(BF16)
            outs[d][1][rows, cols] = uw[:, LANES:].astype(BF16)


def _gdn_delta_solve(qkv, gsum, gsum_t, chunks):
    T = qkv.shape[0]
    R = chunks * GDN_CHUNK
    k_block0 = GDN_Q_DIM // LANES
    v_block0 = 2 * GDN_Q_DIM // (2 * LANES)
    wide = pl.BlockSpec((R, 2 * LANES), lambda i, h: (i, h))
    narrow = pl.BlockSpec((R, LANES), lambda i, h: (i, h))
    tall = pl.BlockSpec((chunks // 2, 2 * GDN_DK, 2 * GDN_CHUNK), lambda i, h: (i, h, 0))
    uw_shape = jax.ShapeDtypeStruct((T, GDN_V_DIM), BF16)
    p_shape = jax.ShapeDtypeStruct((T, GDN_V_HEADS * GDN_CHUNK), BF16)
    kt_shape = jax.ShapeDtypeStruct((T // (2 * GDN_CHUNK), GDN_V_HEADS * GDN_DK, 2 * GDN_CHUNK), BF16)
    return pl.pallas_call(
        functools.partial(_delta_solve_kernel, chunks),
        grid=(T // R, GDN_QK_HEADS),
        in_specs=[
            narrow,
            pl.BlockSpec((R, LANES), lambda i, h: (i, k_block0 + h)),
            pl.BlockSpec((R, 2 * LANES), lambda i, h: (i, v_block0 + h)),
            pl.BlockSpec((R, LANES), lambda i, h: (i, 0)),
            pl.BlockSpec((8, R), lambda i, h: (h, i)),
        ],
        out_specs=[wide, wide, narrow, wide, tall] * 2,
        out_shape=[uw_shape, uw_shape, p_shape, uw_shape, kt_shape] * 2,
        compiler_params=_params(("parallel", "parallel")),
        name="gdn_delta_solve",
    )(qkv, qkv, qkv, gsum, gsum_t)


def _delta_sweep_kernel(rev, chunks, heads, *refs):
    if rev:
        u_ref, w_ref, p_ref, qd_ref, kdt_ref, gt_ref, of_ref, z_ref, nw_ref, o_ref, s_ref = refs
    else:
        u_ref, w_ref, p_ref, qd_ref, kdt_ref, gt_ref, o_ref, s_ref = refs
    C = GDN_CHUNK
    d = 1 if rev else 0

    @pl.when(pl.program_id(2) == 0)
    def _():
        s_ref[...] = jnp.zeros_like(s_ref)

    states = range(2 * heads)
    zc = jnp.zeros((C, GDN_DK), BF16)
    zs = jnp.zeros((GDN_DK, GDN_DK), BF16)
    diag = lambda a, b, z: jnp.concatenate([jnp.concatenate([a, z], axis=1),
                                            jnp.concatenate([z, b], axis=1)], axis=0)
    order = range(chunks - 1, -1, -1) if rev else range(chunks)
    for ci in order:
        rows = slice(ci * C, (ci + 1) * C)
        last = ci * C + (0 if rev else C - 1)
        cols = [slice(h * LANES, (h + 1) * LANES) for h in states]
        both = [slice(2 * j * LANES, (2 * j + 2) * LANES) for j in range(heads)]
        s_old = [s_ref[h] for h in states]
        wqs = [_dot(jnp.concatenate([w_ref[rows, both[j]], qd_ref[rows, both[j]]], axis=0),
                    diag(s_old[2 * j].astype(BF16), s_old[2 * j + 1].astype(BF16), zs)) for j in range(heads)]
        vns = [(u_ref[rows, both[j]].astype(F32) - wqs[j][:C]).astype(BF16) for j in range(heads)]
        vn = [vns[h // 2][:, (h % 2) * LANES:(h % 2 + 1) * LANES] for h in states]
        pvs = [_dot(p_ref[rows, j * LANES:(j + 1) * LANES], diag(vn[2 * j], vn[2 * j + 1], zc))
               for j in range(heads)]
        kvs = [_dot(kdt_ref[ci // 2, cols[h], :], jnp.concatenate([zc, vn[h]] if ci % 2 else [vn[h], zc], axis=0))
               for h in states]
        for h in states:
            lg = 8 * (h // 2) + _gate_lane(d, h % 2, True)
            s_ref[h] = s_old[h] * jnp.exp(gt_ref[lg:lg + 1, last:last + 1]) + kvs[h]
            half = slice((h % 2) * LANES, (h % 2 + 1) * LANES)
            o = wqs[h // 2][C:, half] + pvs[h // 2][:, half]
            if rev:
                tot = of_ref[rows, cols[h]].astype(F32) + o
                gate = _silu(z_ref[rows, cols[h]].astype(F32))
                o_ref[rows, cols[h]] = (_rms(tot, nw_ref[...]) * gate).astype(BF16)
            else:
                o_ref[rows, cols[h]] = o.astype(BF16)


def _gdn_delta_sweep(rev, u, w, p, qd, kdt, gsum_t, batch, chunks, heads, extra=None):
    T = u.shape[0]
    R = chunks * GDN_CHUNK
    nb = T // batch // R
    assert chunks % 2 == 0
    rows = (lambda b, n: b * nb + (nb - 1 - n)) if rev else (lambda b, n: b * nb + n)
    qk_w = heads * LANES
    v_w = 2 * heads * LANES
    wide = pl.BlockSpec((R, v_w), lambda b, h, n: (rows(b, n), h))
    narrow = pl.BlockSpec((R, qk_w), lambda b, h, n: (rows(b, n), h))
    in_specs = [
        wide, wide, narrow, wide,
        pl.BlockSpec((chunks // 2, v_w, 2 * GDN_CHUNK), lambda b, h, n: (rows(b, n), h, 0)),
        pl.BlockSpec((8 * heads, R), lambda b, h, n: (h, rows(b, n))),
    ]
    args = [u, w, p, qd, kdt, gsum_t]
    if rev:
        o_fwd, pre, norm_w = extra
        z_block0 = GDN_CONV_DIM // v_w
        in_specs += [
            wide,
            pl.BlockSpec((R, v_w), lambda b, h, n: (rows(b, n), z_block0 + h)),
            pl.BlockSpec((1, LANES), lambda b, h, n: (0, 0)),
        ]
        args += [o_fwd, pre, norm_w]
    return pl.pallas_call(
        functools.partial(_delta_sweep_kernel, rev, chunks, heads),
        grid=(batch, GDN_QK_HEADS // heads, nb),
        in_specs=in_specs,
        out_specs=wide,
        out_shape=jax.ShapeDtypeStruct((T, GDN_V_DIM), BF16),
        scratch_shapes=[pltpu.VMEM((2 * heads, GDN_DK, GDN_DK), F32)],
        compiler_params=_params(("parallel", "parallel", "arbitrary")),
        name="gdn_sweep_bwd" if rev else "gdn_sweep_fwd",
    )(*args)


def _out_proj_kernel(transposed, a_ref, w_ref, nw_ref, x_ref, o_ref):
    m = _dot_tn(a_ref[...], w_ref[...]) if transposed else _dot(a_ref[...], w_ref[...])
    o_ref[...] = x_ref[...] + _rms(m, nw_ref[...])


def _out_proj(a, w, nw, x, tm, transposed=False):
    K, D = w.shape
    T = x.shape[0]
    a_spec = pl.BlockSpec((K, tm), lambda i: (0, i)) if transposed else pl.BlockSpec((tm, K), lambda i: (i, 0))
    return pl.pallas_call(
        functools.partial(_out_proj_kernel, transposed),
        grid=(T // tm,),
        in_specs=[
            a_spec,
            _resident((K, D), lambda i: (0, 0)),
            pl.BlockSpec((1, D), lambda i: (0, 0)),
            pl.BlockSpec((tm, D), lambda i: (i, 0)),
        ],
        out_specs=pl.BlockSpec((tm, D), lambda i: (i, 0)),
        out_shape=jax.ShapeDtypeStruct((T, D), F32),
        compiler_params=_params(("parallel",)),
        name="out_proj_t" if transposed else "out_proj",
    )(a, w, nw, x)


def _ffn_kernel(x_ref, nw1_ref, w1_ref, w2_ref, nw2_ref, o_ref, xn_ref, acc_ref):
    j = pl.program_id(1)

    @pl.when(j == 0)
    def _():
        xn_ref[...] = _rms(x_ref[...], nw1_ref[...]).astype(BF16)
        acc_ref[...] = jnp.zeros_like(acc_ref)

    h = jnp.maximum(_dot(xn_ref[...], w1_ref[...]), 0.0)
    acc_ref[...] += _dot((h * h).astype(BF16), w2_ref[...])

    @pl.when(j == pl.num_programs(1) - 1)
    def _():
        o_ref[...] = x_ref[...] + _rms(acc_ref[...], nw2_ref[...])


def _ffn(x, nw1, w1, w2, nw2, tm, tf):
    T, D = x.shape
    Fd = w1.shape[1]
    return pl.pallas_call(
        _ffn_kernel,
        grid=(T // tm, Fd // tf),
        in_specs=[
            pl.BlockSpec((tm, D), lambda i, j: (i, 0)),
            pl.BlockSpec((1, D), lambda i, j: (0, 0)),
            pl.BlockSpec((D, tf), lambda i, j: (0, j)),
            pl.BlockSpec((tf, D), lambda i, j: (j, 0)),
            pl.BlockSpec((1, D), lambda i, j: (0, 0)),
        ],
        out_specs=pl.BlockSpec((tm, D), lambda i, j: (i, 0)),
        out_shape=jax.ShapeDtypeStruct((T, D), F32),
        scratch_shapes=[pltpu.VMEM((tm, D), BF16), pltpu.VMEM((tm, D), F32)],
        compiler_params=_params(("parallel", "arbitrary")),
        name="ffn",
    )(x, nw1, w1, w2, nw2)


def _mla_proj_kernel(x_ref, nw_ref, wa_ref, qn_ref, kvn_ref, wqt_ref, wkn_ref, wvt_ref,
                     cos_ref, sin_ref, cost_ref, sint_ref, qt_ref, k_ref, vt_ref):
    xn = _rms(x_ref[...], nw_ref[...]).astype(BF16)
    a = _dot(xn, wa_ref[...])
    cq = _rms(a[:, :MLA_Q_RANK], qn_ref[...]).astype(BF16)
    ckv = _rms(a[:, MLA_Q_RANK:MLA_Q_RANK + MLA_KV_RANK], kvn_ref[...]).astype(BF16)

    kr = a[:, MLA_Q_RANK + MLA_KV_RANK:]
    k_rope = (kr * cos_ref[...] + pltpu.roll(kr, LANES // 2, axis=1) * sin_ref[...]).astype(BF16)
    kn = _dot(ckv, wkn_ref[...])
    for h in range(MLA_HEADS):
        c0 = h * MLA_D_PAD
        k_ref[:, c0:c0 + LANES] = kn[:, h * LANES:(h + 1) * LANES].astype(BF16)
        k_ref[:, c0 + LANES:c0 + 2 * LANES] = k_rope

    scale = MLA_D_QK ** -0.5 * math.log2(math.e)
    qt = _dot_nt(wqt_ref[...], cq)
    cost = cost_ref[...]
    sint = sint_ref[...]
    half = LANES // 2
    for h in range(MLA_HEADS):
        r0 = h * MLA_D_PAD
        qt_ref[r0:r0 + LANES, :] = (qt[r0:r0 + LANES] * scale).astype(BF16)
        blk = qt[r0 + LANES:r0 + 2 * LANES]
        swapped = jnp.concatenate([blk[half:], blk[:half]], axis=0)
        qt_ref[r0 + LANES:r0 + 2 * LANES, :] = ((blk * cost + swapped * sint) * scale).astype(BF16)

    vt = _dot_nt(wvt_ref[...], ckv)
    ones = jnp.ones((MLA_V_ROWS - MLA_D_V, vt.shape[1]), BF16)
    for h in range(MLA_HEADS):
        r0 = h * MLA_V_ROWS
        vt_ref[r0:r0 + MLA_D_V, :] = vt[h * MLA_D_V:(h + 1) * MLA_D_V].astype(BF16)
        vt_ref[r0 + MLA_D_V:r0 + MLA_V_ROWS, :] = ones


def _mla_proj(x, nw, wa, qn, kvn, wqt, wkn, wvt, cos, sin, seq_len, tm):
    T, D = x.shape
    pos_blocks = seq_len // tm
    row = lambda i: (i, 0)
    col = lambda i: (0, i)
    fixed = lambda i: (0, 0)
    return pl.pallas_call(
        _mla_proj_kernel,
        grid=(T // tm,),
        in_specs=[
            pl.BlockSpec((tm, D), row),
            pl.BlockSpec((1, D), fixed),
            _resident(wa.shape, fixed),
            pl.BlockSpec((1, MLA_Q_RANK), fixed),
            pl.BlockSpec((1, MLA_KV_RANK), fixed),
            _resident(wqt.shape, fixed),
            _resident(wkn.shape, fixed),
            _resident(wvt.shape, fixed),
            pl.BlockSpec((tm, LANES), lambda i: (i % pos_blocks, 0)),
            pl.BlockSpec((tm, LANES), lambda i: (i % pos_blocks, 0)),
            pl.BlockSpec((LANES, tm), lambda i: (0, i % pos_blocks)),
            pl.BlockSpec((LANES, tm), lambda i: (0, i % pos_blocks)),
        ],
        out_specs=[
            pl.BlockSpec((MLA_HEADS * MLA_D_PAD, tm), col),
            pl.BlockSpec((tm, MLA_HEADS * MLA_D_PAD), row),
            pl.BlockSpec((MLA_HEADS * MLA_V_ROWS, tm), col),
        ],
        out_shape=[
            jax.ShapeDtypeStruct((MLA_HEADS * MLA_D_PAD, T), BF16),
            jax.ShapeDtypeStruct((T, MLA_HEADS * MLA_D_PAD), BF16),
            jax.ShapeDtypeStruct((MLA_HEADS * MLA_V_ROWS, T), BF16),
        ],
        compiler_params=_params(("parallel",)),
        name="mla_proj",
    )(x, nw, wa, qn, kvn, wqt, wkn, wvt, cos, sin, cos.T, sin.T)


ATTN_SOFTMAX_ROWS = 64
ATTN_SCORE_SLOTS = 4


def _attn_kernel(tk, tq, qt_ref, k_ref, vt_ref, ot_ref, s_ref, p_ref, acc_ref):
    n = k_ref.shape[0] // tk
    blocks = qt_ref.shape[1] // tq
    R = ATTN_SOFTMAX_ROWS

    def keys(g):
        return slice((g % n) * tk, (g % n + 1) * tk)

    def queries(g):
        return slice((g // n) * tq, (g // n + 1) * tq)

    def scores(g):
        s_ref[g % ATTN_SCORE_SLOTS] = _dot(k_ref[keys(g), :], qt_ref[:, queries(g)])

    def weighted(g, a):
        acc = a * acc_ref[g // n] + _dot(vt_ref[:, keys(g)], p_ref[g % 2])
        if g % n == n - 1:
            ot_ref[:, queries(g)] = (acc[:MLA_D_V] / acc[MLA_D_V:MLA_D_V + 1]).astype(BF16)
        else:
            acc_ref[g // n] = acc

    def softmax(g, m):
        slot = g % ATTN_SCORE_SLOTS
        if g % n == 0:
            m = jnp.full((1, tq), -jnp.inf, F32)
        mx = s_ref[slot, 0:R, :]
        for r in range(R, tk, R):
            mx = jnp.maximum(mx, s_ref[slot, r:r + R, :])
        m_new = jnp.maximum(m, jnp.max(mx, axis=0, keepdims=True))
        for r in range(0, tk, R):
            p_ref[g % 2, r:r + R, :] = jnp.exp2(s_ref[slot, r:r + R, :] - m_new).astype(BF16)
        return m_new, jnp.exp2(m - m_new)

    total = blocks * n
    acc_ref[...] = jnp.zeros_like(acc_ref)
    scores(0)
    scores(1)
    m, a = softmax(0, None)
    scores(2)
    for g in range(1, total):
        if g + 2 < total:
            scores(g + 2)
        weighted(g - 1, a)
        m, a = softmax(g, m)
    weighted(total - 1, a)


def _attention(qt, k, vt, batch, tq, tk, blocks):
    T = k.shape[0]
    L = T // batch
    nq = L // (tq * blocks)
    assert blocks * (L // tk) >= 3
    return pl.pallas_call(
        functools.partial(_attn_kernel, tk, tq),
        grid=(batch, MLA_HEADS, nq),
        in_specs=[
            pl.BlockSpec((MLA_D_PAD, blocks * tq), lambda b, h, i: (h, b * nq + i)),
            pl.BlockSpec((L, MLA_D_PAD), lambda b, h, i: (b, h)),
            pl.BlockSpec((MLA_V_ROWS, L), lambda b, h, i: (h, b)),
        ],
        out_specs=pl.BlockSpec((MLA_D_V, blocks * tq), lambda b, h, i: (h, b * nq + i)),
        out_shape=jax.ShapeDtypeStruct((MLA_HEADS * MLA_D_V, T), BF16),
        scratch_shapes=[pltpu.VMEM((ATTN_SCORE_SLOTS, tk, tq), F32), pltpu.VMEM((2, tk, tq), BF16),
                        pltpu.VMEM((blocks, MLA_V_ROWS, tq), F32)],
        compiler_params=_params(("parallel", "parallel", "arbitrary")),
        name="attention",
    )(qt, k, vt)


def _gate_lane_perm():
    lanes = jnp.arange(GDN_GATE_LANES)
    q, e = lanes // 8, lanes % 8
    return GDN_V_HEADS * (e // 2) + 2 * q + e % 2


def _gate_lane_params(a_log, dt_bias):
    lanes = jnp.arange(GDN_GATE_LANES)
    q, e = lanes // 8, lanes % 8
    head = 2 * q + e % 2
    direction = jnp.maximum(e // 2 - 2, 0)
    is_decay = e >= 4
    neg_a = jnp.where(is_decay, -jnp.exp(a_log.astype(F32))[direction, head], 0.0)
    dt = jnp.where(is_decay, dt_bias.astype(F32)[direction, head], 0.0)
    return neg_a, dt


def _pad_rope_cols(w):
    half = MLA_D_ROPE // 2
    z = jnp.zeros(w.shape[:-1] + (half,), w.dtype)
    return jnp.concatenate([w[..., :half], z, w[..., half:], z], axis=-1)


def _rope_tables(L):
    half = MLA_D_ROPE // 2
    inv_freq = ROPE_THETA ** (-jnp.arange(half, dtype=F32) / half)
    ang = jnp.arange(L, dtype=F32)[:, None] * inv_freq[None, :]
    c, s = jnp.cos(ang), jnp.sin(ang)
    z = jnp.zeros_like(c)
    return jnp.concatenate([c, z, c, z], axis=-1), jnp.concatenate([-s, z, s, z], axis=-1)


def _prepare(p):
    w = {}
    g_in = p['gdn_w_in'][0]
    w['gdn_main'] = g_in[:, :GDN_MAIN_DIM].astype(BF16)
    wg = g_in[:, GDN_MAIN_DIM:][:, _gate_lane_perm()].astype(BF16)
    w['gdn_gate'] = wg
    w['gdn_gate_t'] = wg.T
    w['gdn_neg_a'], w['gdn_dt'] = _gate_lane_params(p['gdn_a_log'][0], p['gdn_dt_bias'][0])
    w['gdn_conv'] = p['gdn_conv_w'][0].astype(F32)
    w['gdn_norm'] = p['gdn_norm_w'][0].reshape(1, GDN_DK).astype(F32)
    w['gdn_out'] = p['gdn_w_out'][0].astype(BF16)

    wa = p['mla_w_a'][0]
    rank = MLA_Q_RANK + MLA_KV_RANK
    w['mla_a'] = jnp.concatenate([wa[:, :rank], _pad_rope_cols(wa[:, rank:])], axis=-1).astype(BF16)
    wq = p['mla_w_q_b'][0].reshape(MLA_Q_RANK, MLA_HEADS, MLA_D_QK)
    wq = jnp.concatenate([wq[..., :MLA_D_NOPE], _pad_rope_cols(wq[..., MLA_D_NOPE:])], axis=-1)
    w['mla_q_t'] = wq.reshape(MLA_Q_RANK, MLA_HEADS * MLA_D_PAD).astype(BF16).T
    wkv = p['mla_w_kv_b'][0].reshape(MLA_KV_RANK, MLA_HEADS, MLA_D_NOPE + MLA_D_V)
    w['mla_kn'] = wkv[..., :MLA_D_NOPE].reshape(MLA_KV_RANK, MLA_HEADS * MLA_D_NOPE).astype(BF16)
    w['mla_v_t'] = wkv[..., MLA_D_NOPE:].reshape(MLA_KV_RANK, MLA_HEADS * MLA_D_V).astype(BF16).T
    w['mla_qn'] = p['mla_q_a_norm'][0].reshape(1, MLA_Q_RANK).astype(F32)
    w['mla_kvn'] = p['mla_kv_a_norm'][0].reshape(1, MLA_KV_RANK).astype(F32)
    w['mla_o'] = p['mla_w_o'][0].astype(BF16)

    w['ffn_in'] = p['ffn_w_in'].astype(BF16)
    w['ffn_out'] = p['ffn_w_out'].astype(BF16)
    for name in ('norm_mix_pre', 'norm_mix_post', 'norm_ffn_pre', 'norm_ffn_post'):
        w[name] = p[name].astype(F32)[:, None, :]
    return w


TILES = dict(
    proj_tm=1024, proj_tn=2048,
    gate_tr=512,
    conv_tr=512, conv_tc=1024,
    solve_chunks=16,
    sweep_chunks=4, sweep_heads=16,
    out_tm=512,
    ffn_tm=512, ffn_tf=1024,
    mla_tm=512,
    attn_tq=256, attn_tk=512, attn_items=64,
)


def _trunk(x3, w, tiles):
    B, L, D = x3.shape
    x = x3.reshape(B * L, D)

    pre, gates, gates_t = _gdn_proj(x, w['norm_mix_pre'][0], w['gdn_main'], w['gdn_gate'], w['gdn_gate_t'],
                                    tiles['proj_tm'], tiles['proj_tn'])
    gsum, gsum_t = _gdn_gates(gates, gates_t, w['gdn_neg_a'], w['gdn_dt'], tiles['gate_tr'])
    qkv = _gdn_conv(pre, w['gdn_conv'], L, tiles['conv_tr'], tiles['conv_tc'])
    solved = _gdn_delta_solve(qkv, gsum, gsum_t, tiles['solve_chunks'])
    sweep = functools.partial(_gdn_delta_sweep, gsum_t=gsum_t, batch=B, chunks=tiles['sweep_chunks'],
                              heads=tiles['sweep_heads'])
    o_fwd = sweep(False, *solved[:5])
    o = sweep(True, *solved[5:], extra=(o_fwd, pre, w['gdn_norm']))
    x = _out_proj(o, w['gdn_out'], w['norm_mix_post'][0], x, tiles['out_tm'])
    x = _ffn(x, w['norm_ffn_pre'][0], w['ffn_in'][0], w['ffn_out'][0], w['norm_ffn_post'][0],
             tiles['ffn_tm'], tiles['ffn_tf'])

    cos, sin = _rope_tables(L)
    qt, k, vt = _mla_proj(x, w['norm_mix_pre'][1], w['mla_a'], w['mla_qn'], w['mla_kvn'], w['mla_q_t'],
                          w['mla_kn'], w['mla_v_t'], cos, sin, L, tiles['mla_tm'])
    blocks = max(1, tiles['attn_items'] * tiles['attn_tk'] // L)
    ot = _attention(qt, k, vt, B, tiles['attn_tq'], tiles['attn_tk'], blocks)
    x = _out_proj(ot, w['mla_o'], w['norm_mix_post'][1], x, tiles['out_tm'], transposed=True)
    x = _ffn(x, w['norm_ffn_pre'][1], w['ffn_in'][1], w['ffn_out'][1], w['norm_ffn_post'][1],
             tiles['ffn_tm'], tiles['ffn_tf'])
    return x.reshape(B, L, D)


def kernel(x_prompt, x_sample, norm_mix_pre, norm_mix_post, norm_ffn_pre, norm_ffn_post, gdn_w_in, gdn_conv_w, gdn_a_log, gdn_dt_bias, gdn_norm_w, gdn_w_out, mla_w_a, mla_q_a_norm, mla_w_q_b, mla_kv_a_norm, mla_w_kv_b, mla_w_o, ffn_w_in, ffn_w_out):
    w = _prepare(dict(
        norm_mix_pre=norm_mix_pre, norm_mix_post=norm_mix_post, norm_ffn_pre=norm_ffn_pre,
        norm_ffn_post=norm_ffn_post, gdn_w_in=gdn_w_in, gdn_conv_w=gdn_conv_w, gdn_a_log=gdn_a_log,
        gdn_dt_bias=gdn_dt_bias, gdn_norm_w=gdn_norm_w, gdn_w_out=gdn_w_out, mla_w_a=mla_w_a,
        mla_q_a_norm=mla_q_a_norm, mla_w_q_b=mla_w_q_b, mla_kv_a_norm=mla_kv_a_norm,
        mla_w_kv_b=mla_w_kv_b, mla_w_o=mla_w_o, ffn_w_in=ffn_w_in, ffn_w_out=ffn_w_out))
    return _trunk(x_prompt, w, TILES), _trunk(x_sample, w, TILES)
```

```python
import functools
import math

import jax
import jax.numpy as jnp
from jax import lax
from jax.experimental import pallas as pl
from jax.experimental.pallas import tpu as pltpu

F32 = jnp.float32
BF16 = jnp.bfloat16

RMS_EPS = 1e-6
L2_EPS = 1e-6
LANES = 128
BF16_ROWS = 16

GDN_QK_HEADS = 16
GDN_V_HEADS = 32
GDN_DK = 128
GDN_CONV = 5
GDN_CHUNK = 64
GDN_Q_DIM = GDN_QK_HEADS * GDN_DK
GDN_V_DIM = GDN_V_HEADS * GDN_DK
GDN_CONV_DIM = 2 * GDN_Q_DIM + GDN_V_DIM
GDN_MAIN_DIM = GDN_CONV_DIM + GDN_V_DIM
GDN_GATE_LANES = 4 * GDN_V_HEADS

MLA_HEADS = 16
MLA_Q_RANK = 768
MLA_KV_RANK = 512
MLA_D_NOPE = 128
MLA_D_ROPE = 64
MLA_D_V = 128
MLA_D_QK = MLA_D_NOPE + MLA_D_ROPE
MLA_D_PAD = 2 * LANES
MLA_V_ROWS = MLA_D_V + BF16_ROWS
ROPE_THETA = 10000.0

VMEM_LIMIT = 56 * 1024 * 1024


def _params(sem):
    return pltpu.CompilerParams(dimension_semantics=sem, vmem_limit_bytes=VMEM_LIMIT)


def _resident(shape, index_map):
    return pl.BlockSpec(shape, index_map, pipeline_mode=pl.Buffered(1))


def _rms(x, w):
    return x * lax.rsqrt(jnp.mean(x * x, axis=-1, keepdims=True) + RMS_EPS) * w


def _dot(a, b):
    return jnp.dot(a, b, preferred_element_type=F32)


def _dot_nt(a, b):
    return lax.dot_general(a, b, (((1,), (1,)), ((), ())), preferred_element_type=F32)


def _dot_tn(a, b):
    return lax.dot_general(a, b, (((0,), (0,)), ((), ())), preferred_element_type=F32)


def _split3(x):
    hi = x.astype(BF16)
    r1 = x - hi.astype(F32)
    mid = r1.astype(BF16)
    lo = (r1 - mid.astype(F32)).astype(BF16)
    return hi, mid, lo


def _silu(x):
    return x * (1.0 / (1.0 + jnp.exp(-x)))


def _gdn_proj_kernel(x_ref, nw_ref, w_ref, wg_ref, wgt_ref, o_ref, g_ref, gt_ref, xn_ref):
    @pl.when(pl.program_id(1) == 0)
    def _():
        xn = _rms(x_ref[...], nw_ref[...]).astype(BF16)
        xn_ref[...] = xn
        g_ref[...] = _dot(xn, wg_ref[...])
        gt_ref[...] = _dot_nt(wgt_ref[...], xn)

    o_ref[...] = _dot(xn_ref[...], w_ref[...]).astype(BF16)


def _gdn_proj(x, nw, w, wg, wgt, tm, tn):
    T, D = x.shape
    N = w.shape[1]
    return pl.pallas_call(
        _gdn_proj_kernel,
        grid=(T // tm, N // tn),
        in_specs=[
            pl.BlockSpec((tm, D), lambda i, j: (i, 0)),
            pl.BlockSpec((1, D), lambda i, j: (0, 0)),
            pl.BlockSpec((D, tn), lambda i, j: (0, j)),
            pl.BlockSpec((D, LANES), lambda i, j: (0, 0)),
            pl.BlockSpec((LANES, D), lambda i, j: (0, 0)),
        ],
        out_specs=[
            pl.BlockSpec((tm, tn), lambda i, j: (i, j)),
            pl.BlockSpec((tm, LANES), lambda i, j: (i, 0)),
            pl.BlockSpec((LANES, tm), lambda i, j: (0, i)),
        ],
        out_shape=[
            jax.ShapeDtypeStruct((T, N), BF16),
            jax.ShapeDtypeStruct((T, LANES), F32),
            jax.ShapeDtypeStruct((LANES, T), F32),
        ],
        scratch_shapes=[pltpu.VMEM((tm, D), BF16)],
        compiler_params=_params(("parallel", "arbitrary")),
        name="gdn_proj",
    )(x, nw, w, wg, wgt)


def _softplus(y):
    return jnp.maximum(y, 0.0) + jnp.log1p(jnp.exp(-jnp.abs(y)))


def _gate_values(x, neg_a, dt):
    beta = 1.0 / (1.0 + jnp.exp(-x))
    g = neg_a * _softplus(x + dt)
    return beta, g


def _gdn_gates_kernel(x_ref, xt_ref, na_ref, dt_ref, nat_ref, dtt_ref, o_ref, ot_ref):
    R = x_ref.shape[0]
    ii = lax.broadcasted_iota(jnp.int32, (R, R), 0)
    jj = lax.broadcasted_iota(jnp.int32, (R, R), 1)
    same = (ii // GDN_CHUNK) == (jj // GDN_CHUNK)
    lower = jnp.where(same & (jj <= ii), 1.0, 0.0).astype(BF16)
    upper = jnp.where(same & (jj >= ii), 1.0, 0.0).astype(BF16)

    beta, g = _gate_values(x_ref[...], na_ref[...], dt_ref[...])
    parts = _split3(g)
    cf = sum(_dot(lower, p) for p in parts)
    cb = sum(_dot(upper, p) for p in parts)
    e = lax.broadcasted_iota(jnp.int32, (R, LANES), 1) % 8
    o_ref[...] = jnp.where(e < 4, beta, jnp.where(e < 6, cf, cb))

    beta_t, g_t = _gate_values(xt_ref[...], nat_ref[...], dtt_ref[...])
    parts_t = _split3(g_t)
    cf_t = sum(_dot(p, upper) for p in parts_t)
    cb_t = sum(_dot(p, lower) for p in parts_t)
    e_t = lax.broadcasted_iota(jnp.int32, (LANES, R), 0) % 8
    ot_ref[...] = jnp.where(e_t < 4, beta_t, jnp.where(e_t < 6, cf_t, cb_t))


def _gdn_gates(gates, gates_t, neg_a, dt, tr):
    T = gates.shape[0]
    row = lambda i: (i, 0)
    col = lambda i: (0, i)
    fixed = lambda i: (0, 0)
    return pl.pallas_call(
        _gdn_gates_kernel,
        grid=(T // tr,),
        in_specs=[
            pl.BlockSpec((tr, LANES), row),
            pl.BlockSpec((LANES, tr), col),
            pl.BlockSpec((1, LANES), fixed),
            pl.BlockSpec((1, LANES), fixed),
            pl.BlockSpec((LANES, 1), fixed),
            pl.BlockSpec((LANES, 1), fixed),
        ],
        out_specs=[pl.BlockSpec((tr, LANES), row), pl.BlockSpec((LANES, tr), col)],
        out_shape=[jax.ShapeDtypeStruct((T, LANES), F32), jax.ShapeDtypeStruct((LANES, T), F32)],
        compiler_params=_params(("parallel",)),
        name="gdn_gates",
    )(gates, gates_t, neg_a.reshape(1, LANES), dt.reshape(1, LANES),
      neg_a.reshape(LANES, 1), dt.reshape(LANES, 1))


CONV_HALO = BF16_ROWS
CONV_PAD = 8


def _gdn_conv_kernel(blocks_per_seq, n_q_blocks, n_qk_blocks,
                     prev_ref, main_ref, next_ref, w_ref, o_ref):
    i = pl.program_id(0)
    j = pl.program_id(1)
    tr, tc = main_ref.shape
    r = GDN_CONV // 2
    pos = i % blocks_per_seq
    keep_prev = jnp.where(pos == 0, 0.0, 1.0)
    keep_next = jnp.where(pos == blocks_per_seq - 1, 0.0, 1.0)
    rows = tr + 2 * CONV_PAD

    def conv_silu(cols):
        xs = jnp.concatenate([
            prev_ref[CONV_HALO - CONV_PAD:, cols].astype(F32) * keep_prev,
            main_ref[:, cols].astype(F32),
            next_ref[:CONV_PAD, cols].astype(F32) * keep_next], axis=0)
        w = w_ref[:, cols]
        acc = xs[CONV_PAD:CONV_PAD + tr] * w[r:r + 1, :]
        for t in range(GDN_CONV):
            if t != r:
                shifted = pltpu.roll(xs, (r - t) % rows, axis=0)
                acc = acc + shifted[CONV_PAD:CONV_PAD + tr] * w[t:t + 1, :]
        return _silu(acc)

    @pl.when(j >= n_qk_blocks)
    def _():
        for h in range(tc // LANES):
            cols = slice(h * LANES, (h + 1) * LANES)
            o_ref[:, cols] = conv_silu(cols).astype(BF16)

    @pl.when(j < n_qk_blocks)
    def _():
        q_scale = jnp.where(j < n_q_blocks, GDN_DK ** -0.5, 1.0)
        for h in range(tc // LANES):
            cols = slice(h * LANES, (h + 1) * LANES)
            blk = conv_silu(cols)
            inv = lax.rsqrt(jnp.sum(blk * blk, axis=-1, keepdims=True) + L2_EPS) * q_scale
            o_ref[:, cols] = (blk * inv).astype(BF16)


def _gdn_conv(pre, conv_w, seq_len, tr, tc):
    T = pre.shape[0]
    C = conv_w.shape[1]
    hb = tr // CONV_HALO
    n_halo = T // CONV_HALO
    kern = functools.partial(_gdn_conv_kernel, seq_len // tr, GDN_Q_DIM // tc, 2 * GDN_Q_DIM // tc)
    return pl.pallas_call(
        kern,
        grid=(T // tr, C // tc),
        in_specs=[
            pl.BlockSpec((CONV_HALO, tc), lambda i, j: (jnp.maximum(i * hb - 1, 0), j)),
            pl.BlockSpec((tr, tc), lambda i, j: (i, j)),
            pl.BlockSpec((CONV_HALO, tc), lambda i, j: (jnp.minimum((i + 1) * hb, n_halo - 1), j)),
            pl.BlockSpec((GDN_CONV, tc), lambda i, j: (0, j)),
        ],
        out_specs=pl.BlockSpec((tr, tc), lambda i, j: (i, j)),
        out_shape=jax.ShapeDtypeStruct((T, C), BF16),
        compiler_params=_params(("parallel", "parallel")),
        name="gdn_conv",
    )(pre, pre, pre, conv_w)


NEUMANN_DOUBLINGS = 5


def _gate_lane(direction, v_head, decay):
    return (4 if decay else 0) + 2 * direction + v_head


def _delta_solve_kernel(chunks, q_ref, k_ref, v_ref, g_ref, gt_ref,
                        uf_ref, wf_ref, pf_ref, qf_ref, kf_ref, ub_ref, wb_ref, pb_ref, qb_ref, kb_ref):
    C = GDN_CHUNK
    W = 4 * C
    qh = pl.program_id(1)
    gsel = pltpu.roll(g_ref[...], (LANES - 8 * qh) % LANES, axis=1)
    ii = lax.broadcasted_iota(jnp.int32, (C, W), 0)
    ll = lax.broadcasted_iota(jnp.int32, (C, W), 1)
    jj = ll % C
    group = ll // C
    ahead = jnp.where(group >= 2, jj - ii, ii - jj)
    incl = ahead >= 0
    strict = ahead > 0
    eye = jnp.where(ii == jj, 1.0, 0.0)
    own = [group == p for p in range(4)]
    low_half = lax.broadcasted_iota(jnp.int32, (C, LANES), 1) < C
    outs = ((uf_ref, wf_ref, pf_ref, qf_ref, kf_ref), (ub_ref, wb_ref, pb_ref, qb_ref, kb_ref))
    eye_k = jnp.where(lax.broadcasted_iota(jnp.int32, (GDN_DK, GDN_DK), 0)
                      == lax.broadcasted_iota(jnp.int32, (GDN_DK, GDN_DK), 1), 1.0, 0.0).astype(BF16)
    zeros = jnp.zeros((C, 2 * LANES), BF16)

    def side_by_side(cols):
        return jnp.concatenate([jnp.where(low_half, cols[0], cols[1]),
                                jnp.where(low_half, cols[2], cols[3])], axis=1)

    def block_diag(x):
        return jnp.concatenate([jnp.where(own[p], x, 0.0).astype(BF16) for p in range(4)], axis=0)

    ts, xs, rhs = [], [], []
    for ci in range(chunks):
        rows = slice(ci * C, (ci + 1) * C)
        kc = k_ref[rows, :]
        qc = q_ref[rows, :]
        kq = _dot_nt(jnp.concatenate([kc, qc, eye_k], axis=0), jnp.concatenate([kc] * 4, axis=0))
        betas = [jnp.broadcast_to(gsel[rows, p:p + 1], (C, LANES)) for p in range(4)]
        gccs = [jnp.broadcast_to(gsel[rows, 4 + p:5 + p], (C, LANES)) for p in range(4)]
        g_pair = gt_ref[:, (ci // 2) * 2 * C:(ci // 2 + 1) * 2 * C]
        g_swap = pltpu.roll(g_pair, C, axis=1)
        on_low, on_high = (g_pair, g_swap) if ci % 2 == 0 else (g_swap, g_pair)
        gcrs = [on_low[4 + p:5 + p, :C] for p in range(4)]
        gcr_rows = [jnp.where(low_half[0:1], on_low[4 + p:5 + p], on_high[5 + p:6 + p]) for p in (0, 2)]
        decay = jnp.concatenate([jnp.where(low_half, gccs[p], gccs[p + 1]) - gcr_rows[p // 2]
                                 for p in (0, 2)], axis=1)
        decay = jnp.where(incl, jnp.exp(jnp.where(incl, decay, 0.0)), 0.0)
        a = jnp.where(strict, side_by_side(betas) * kq[:C] * decay, 0.0)
        ts.append(eye - a)
        xs.append(a)
        pm = (kq[C:2 * C] * decay).astype(BF16)
        pf_ref[rows, :] = pm[:, :2 * C]
        pb_ref[rows, :] = pm[:, 2 * C:]
        kcf = kc.astype(F32)
        qcf = qc.astype(F32)
        for p in range(4):
            d, hv = divmod(p, 2)
            cols = slice(hv * LANES, (hv + 1) * LANES)
            gl = gcrs[p][:, 0:1] if d else gcrs[p][:, C - 1:C]
            eg = jnp.exp(gccs[p])
            outs[d][3][rows, cols] = (qcf * eg).astype(BF16)
            outs[d][4][ci // 2, cols, (ci % 2) * C:(ci % 2 + 1) * C] = (
                kq[2 * C:, :C] * jnp.exp(gl - gcrs[p])).astype(BF16)
            vb = v_ref[rows, cols].astype(F32) * betas[p]
            kb = kcf * (betas[p] * eg)
            rhs.append(jnp.concatenate([vb, kb], axis=1).astype(BF16))

    xs = [_dot(x.astype(BF16), block_diag(x)) for x in xs]
    for step in range(NEUMANN_DOUBLINGS):
        if step + 1 < NEUMANN_DOUBLINGS:
            both = [_dot(jnp.concatenate([t, x], axis=0).astype(BF16), block_diag(x)) for t, x in zip(ts, xs)]
            ts = [t + b[:C] for t, b in zip(ts, both)]
            xs = [b[C:] for b in both]
        else:
            ts = [t + _dot(t.astype(BF16), block_diag(x)) for t, x in zip(ts, xs)]

    for ci in range(chunks):
        rows = slice(ci * C, (ci + 1) * C)
        for p in range(4):
            d, hv = divmod(p, 2)
            cols = slice(hv * LANES, (hv + 1) * LANES)
            lhs = jnp.where(own[p], ts[ci], 0.0).astype(BF16)
            padded = jnp.concatenate([zeros] * p + [rhs[4 * ci + p]] + [zeros] * (3 - p), axis=0)
            uw = _dot(lhs, padded)
            outs[d][0][rows, cols] = uw[:, :LANES].astype(BF16)
            outs[d][1][rows, cols] = uw[:, LANES:].astype(BF16)


def _gdn_delta_solve(qkv, gsum, gsum_t, chunks):
    T = qkv.shape[0]
    R = chunks * GDN_CHUNK
    k_block0 = GDN_Q_DIM // LANES
    v_block0 = 2 * GDN_Q_DIM // (2 * LANES)
    wide = pl.BlockSpec((R, 2 * LANES), lambda i, h: (i, h))
    narrow = pl.BlockSpec((R, LANES), lambda i, h: (i, h))
    tall = pl.BlockSpec((chunks // 2, 2 * GDN_DK, 2 * GDN_CHUNK), lambda i, h: (i, h, 0))
    uw_shape = jax.ShapeDtypeStruct((T, GDN_V_DIM), BF16)
    p_shape = jax.ShapeDtypeStruct((T, GDN_V_HEADS * GDN_CHUNK), BF16)
    kt_shape = jax.ShapeDtypeStruct((T // (2 * GDN_CHUNK), GDN_V_HEADS * GDN_DK, 2 * GDN_CHUNK), BF16)
    return pl.pallas_call(
        functools.partial(_delta_solve_kernel, chunks),
        grid=(T // R, GDN_QK_HEADS),
        in_specs=[
            narrow,
            pl.BlockSpec((R, LANES), lambda i, h: (i, k_block0 + h)),
            pl.BlockSpec((R, 2 * LANES), lambda i, h: (i, v_block0 + h)),
            pl.BlockSpec((R, LANES), lambda i, h: (i, 0)),
            pl.BlockSpec((8, R), lambda i, h: (h, i)),
        ],
        out_specs=[wide, wide, narrow, wide, tall] * 2,
        out_shape=[uw_shape, uw_shape, p_shape, uw_shape, kt_shape] * 2,
        compiler_params=_params(("parallel", "parallel")),
        name="gdn_delta_solve",
    )(qkv, qkv, qkv, gsum, gsum_t)


def _delta_sweep_kernel(rev, chunks, heads, *refs):
    if rev:
        u_ref, w_ref, p_ref, qd_ref, kdt_ref, gt_ref, of_ref, z_ref, nw_ref, o_ref, s_ref = refs
    else:
        u_ref, w_ref, p_ref, qd_ref, kdt_ref, gt_ref, o_ref, s_ref = refs
    C = GDN_CHUNK
    d = 1 if rev else 0

    @pl.when(pl.program_id(2) == 0)
    def _():
        s_ref[...] = jnp.zeros_like(s_ref)

    states = range(2 * heads)
    zc = jnp.zeros((C, GDN_DK), BF16)
    zs = jnp.zeros((GDN_DK, GDN_DK), BF16)
    diag = lambda a, b, z: jnp.concatenate([jnp.concatenate([a, z], axis=1),
                                            jnp.concatenate([z, b], axis=1)], axis=0)
    order = range(chunks - 1, -1, -1) if rev else range(chunks)
    for ci in order:
        rows = slice(ci * C, (ci + 1) * C)
        last = ci * C + (0 if rev else C - 1)
        cols = [slice(h * LANES, (h + 1) * LANES) for h in states]
        both = [slice(2 * j * LANES, (2 * j + 2) * LANES) for j in range(heads)]
        s_old = [s_ref[h] for h in states]
        wqs = [_dot(jnp.concatenate([w_ref[rows, both[j]], qd_ref[rows, both[j]]], axis=0),
                    diag(s_old[2 * j].astype(BF16), s_old[2 * j + 1].astype(BF16), zs)) for j in range(heads)]
        vns = [(u_ref[rows, both[j]].astype(F32) - wqs[j][:C]).astype(BF16) for j in range(heads)]
        vn = [vns[h // 2][:, (h % 2) * LANES:(h % 2 + 1) * LANES] for h in states]
        pvs = [_dot(p_ref[rows, j * LANES:(j + 1) * LANES], diag(vn[2 * j], vn[2 * j + 1], zc))
               for j in range(heads)]
        kvs = [_dot(kdt_ref[ci // 2, cols[h], :], jnp.concatenate([zc, vn[h]] if ci % 2 else [vn[h], zc], axis=0))
               for h in states]
        for h in states:
            lg = 8 * (h // 2) + _gate_lane(d, h % 2, True)
            s_ref[h] = s_old[h] * jnp.exp(gt_ref[lg:lg + 1, last:last + 1]) + kvs[h]
            half = slice((h % 2) * LANES, (h % 2 + 1) * LANES)
            o = wqs[h // 2][C:, half] + pvs[h // 2][:, half]
            if rev:
                tot = of_ref[rows, cols[h]].astype(F32) + o
                gate = _silu(z_ref[rows, cols[h]].astype(F32))
                o_ref[rows, cols[h]] = (_rms(tot, nw_ref[...]) * gate).astype(BF16)
            else:
                o_ref[rows, cols[h]] = o.astype(BF16)


def _gdn_delta_sweep(rev, u, w, p, qd, kdt, gsum_t, batch, chunks, heads, extra=None):
    T = u.shape[0]
    R = chunks * GDN_CHUNK
    nb = T // batch // R
    assert chunks % 2 == 0
    rows = (lambda b, n: b * nb + (nb - 1 - n)) if rev else (lambda b, n: b * nb + n)
    qk_w = heads * LANES
    v_w = 2 * heads * LANES
    wide = pl.BlockSpec((R, v_w), lambda b, h, n: (rows(b, n), h))
    narrow = pl.BlockSpec((R, qk_w), lambda b, h, n: (rows(b, n), h))
    in_specs = [
        wide, wide, narrow, wide,
        pl.BlockSpec((chunks // 2, v_w, 2 * GDN_CHUNK), lambda b, h, n: (rows(b, n), h, 0)),
        pl.BlockSpec((8 * heads, R), lambda b, h, n: (h, rows(b, n))),
    ]
    args = [u, w, p, qd, kdt, gsum_t]
    if rev:
        o_fwd, pre, norm_w = extra
        z_block0 = GDN_CONV_DIM // v_w
        in_specs += [
            wide,
            pl.BlockSpec((R, v_w), lambda b, h, n: (rows(b, n), z_block0 + h)),
            pl.BlockSpec((1, LANES), lambda b, h, n: (0, 0)),
        ]
        args += [o_fwd, pre, norm_w]
    return pl.pallas_call(
        functools.partial(_delta_sweep_kernel, rev, chunks, heads),
        grid=(batch, GDN_QK_HEADS // heads, nb),
        in_specs=in_specs,
        out_specs=wide,
        out_shape=jax.ShapeDtypeStruct((T, GDN_V_DIM), BF16),
        scratch_shapes=[pltpu.VMEM((2 * heads, GDN_DK, GDN_DK), F32)],
        compiler_params=_params(("parallel", "parallel", "arbitrary")),
        name="gdn_sweep_bwd" if rev else "gdn_sweep_fwd",
    )(*args)


def _out_proj_kernel(transposed, a_ref, w_ref, nw_ref, x_ref, o_ref):
    m = _dot_tn(a_ref[...], w_ref[...]) if transposed else _dot(a_ref[...], w_ref[...])
    o_ref[...] = x_ref[...] + _rms(m, nw_ref[...])


def _out_proj(a, w, nw, x, tm, transposed=False):
    K, D = w.shape
    T = x.shape[0]
    a_spec = pl.BlockSpec((K, tm), lambda i: (0, i)) if transposed else pl.BlockSpec((tm, K), lambda i: (i, 0))
    return pl.pallas_call(
        functools.partial(_out_proj_kernel, transposed),
        grid=(T // tm,),
        in_specs=[
            a_spec,
            _resident((K, D), lambda i: (0, 0)),
            pl.BlockSpec((1, D), lambda i: (0, 0)),
            pl.BlockSpec((tm, D), lambda i: (i, 0)),
        ],
        out_specs=pl.BlockSpec((tm, D), lambda i: (i, 0)),
        out_shape=jax.ShapeDtypeStruct((T, D), F32),
        compiler_params=_params(("parallel",)),
        name="out_proj_t" if transposed else "out_proj",
    )(a, w, nw, x)


def _ffn_kernel(x_ref, nw1_ref, w1_ref, w2_ref, nw2_ref, o_ref, xn_ref, acc_ref):
    j = pl.program_id(1)

    @pl.when(j == 0)
    def _():
        xn_ref[...] = _rms(x_ref[...], nw1_ref[...]).astype(BF16)
        acc_ref[...] = jnp.zeros_like(acc_ref)

    h = jnp.maximum(_dot(xn_ref[...], w1_ref[...]), 0.0)
    acc_ref[...] += _dot((h * h).astype(BF16), w2_ref[...])

    @pl.when(j == pl.num_programs(1) - 1)
    def _():
        o_ref[...] = x_ref[...] + _rms(acc_ref[...], nw2_ref[...])


def _ffn(x, nw1, w1, w2, nw2, layer, tm, tf):
    T, D = x.shape
    Fd = w1.shape[2]
    return pl.pallas_call(
        _ffn_kernel,
        grid=(T // tm, Fd // tf),
        in_specs=[
            pl.BlockSpec((tm, D), lambda i, j: (i, 0)),
            pl.BlockSpec((1, D), lambda i, j: (0, 0)),
            pl.BlockSpec((pl.Squeezed(), D, tf), lambda i, j: (layer, 0, j)),
            pl.BlockSpec((pl.Squeezed(), tf, D), lambda i, j: (layer, j, 0)),
            pl.BlockSpec((1, D), lambda i, j: (0, 0)),
        ],
        out_specs=pl.BlockSpec((tm, D), lambda i, j: (i, 0)),
        out_shape=jax.ShapeDtypeStruct((T, D), F32),
        scratch_shapes=[pltpu.VMEM((tm, D), BF16), pltpu.VMEM((tm, D), F32)],
        compiler_params=_params(("parallel", "arbitrary")),
        name="ffn",
    )(x, nw1, w1, w2, nw2)


def _mla_proj_kernel(x_ref, nw_ref, wa_ref, qn_ref, kvn_ref, wqt_ref, wkn_ref, wvt_ref,
                     cos_ref, sin_ref, cost_ref, sint_ref, qt_ref, k_ref, vt_ref):
    xn = _rms(x_ref[...], nw_ref[...]).astype(BF16)
    a = _dot(xn, wa_ref[...])
    cq = _rms(a[:, :MLA_Q_RANK], qn_ref[...]).astype(BF16)
    ckv = _rms(a[:, MLA_Q_RANK:MLA_Q_RANK + MLA_KV_RANK], kvn_ref[...]).astype(BF16)

    kr = a[:, MLA_Q_RANK + MLA_KV_RANK:]
    k_rope = (kr * cos_ref[...] + pltpu.roll(kr, LANES // 2, axis=1) * sin_ref[...]).astype(BF16)
    kn = _dot(ckv, wkn_ref[...])
    for h in range(MLA_HEADS):
        c0 = h * MLA_D_PAD
        k_ref[:, c0:c0 + LANES] = kn[:, h * LANES:(h + 1) * LANES].astype(BF16)
        k_ref[:, c0 + LANES:c0 + 2 * LANES] = k_rope

    scale = MLA_D_QK ** -0.5 * math.log2(math.e)
    qt = _dot_nt(wqt_ref[...], cq)
    cost = cost_ref[...]
    sint = sint_ref[...]
    half = LANES // 2
    for h in range(MLA_HEADS):
        r0 = h * MLA_D_PAD
        qt_ref[r0:r0 + LANES, :] = (qt[r0:r0 + LANES] * scale).astype(BF16)
        blk = qt[r0 + LANES:r0 + 2 * LANES]
        swapped = jnp.concatenate([blk[half:], blk[:half]], axis=0)
        qt_ref[r0 + LANES:r0 + 2 * LANES, :] = ((blk * cost + swapped * sint) * scale).astype(BF16)

    vt = _dot_nt(wvt_ref[...], ckv)
    ones = jnp.ones((MLA_V_ROWS - MLA_D_V, vt.shape[1]), BF16)
    for h in range(MLA_HEADS):
        r0 = h * MLA_V_ROWS
        vt_ref[r0:r0 + MLA_D_V, :] = vt[h * MLA_D_V:(h + 1) * MLA_D_V].astype(BF16)
        vt_ref[r0 + MLA_D_V:r0 + MLA_V_ROWS, :] = ones


def _mla_proj(x, nw, wa, qn, kvn, wqt, wkn, wvt, cos, sin, seq_len, tm):
    T, D = x.shape
    pos_blocks = seq_len // tm
    row = lambda i: (i, 0)
    col = lambda i: (0, i)
    fixed = lambda i: (0, 0)
    return pl.pallas_call(
        _mla_proj_kernel,
        grid=(T // tm,),
        in_specs=[
            pl.BlockSpec((tm, D), row),
            pl.BlockSpec((1, D), fixed),
            _resident(wa.shape, fixed),
            pl.BlockSpec((1, MLA_Q_RANK), fixed),
            pl.BlockSpec((1, MLA_KV_RANK), fixed),
            _resident(wqt.shape, fixed),
            _resident(wkn.shape, fixed),
            _resident(wvt.shape, fixed),
            pl.BlockSpec((tm, LANES), lambda i: (i % pos_blocks, 0)),
            pl.BlockSpec((tm, LANES), lambda i: (i % pos_blocks, 0)),
            pl.BlockSpec((LANES, tm), lambda i: (0, i % pos_blocks)),
            pl.BlockSpec((LANES, tm), lambda i: (0, i % pos_blocks)),
        ],
        out_specs=[
            pl.BlockSpec((MLA_HEADS * MLA_D_PAD, tm), col),
            pl.BlockSpec((tm, MLA_HEADS * MLA_D_PAD), row),
            pl.BlockSpec((MLA_HEADS * MLA_V_ROWS, tm), col),
        ],
        out_shape=[
            jax.ShapeDtypeStruct((MLA_HEADS * MLA_D_PAD, T), BF16),
            jax.ShapeDtypeStruct((T, MLA_HEADS * MLA_D_PAD), BF16),
            jax.ShapeDtypeStruct((MLA_HEADS * MLA_V_ROWS, T), BF16),
        ],
        compiler_params=_params(("parallel",)),
        name="mla_proj",
    )(x, nw, wa, qn, kvn, wqt, wkn, wvt, cos, sin, cos.T, sin.T)


ATTN_SOFTMAX_ROWS = 64
ATTN_SCORE_SLOTS = 4


def _attn_kernel(tk, tq, qt_ref, k_ref, vt_ref, ot_ref, s_ref, p_ref, acc_ref):
    n = k_ref.shape[0] // tk
    blocks = qt_ref.shape[1] // tq
    R = ATTN_SOFTMAX_ROWS

    def keys(g):
        return slice((g % n) * tk, (g % n + 1) * tk)

    def queries(g):
        return slice((g // n) * tq, (g // n + 1) * tq)

    def scores(g):
        s_ref[g % ATTN_SCORE_SLOTS] = _dot(k_ref[keys(g), :], qt_ref[:, queries(g)])

    def weighted(g, a):
        acc = a * acc_ref[g // n] + _dot(vt_ref[:, keys(g)], p_ref[g % 2])
        if g % n == n - 1:
            ot_ref[:, queries(g)] = (acc[:MLA_D_V] / acc[MLA_D_V:MLA_D_V + 1]).astype(BF16)
        else:
            acc_ref[g // n] = acc

    def softmax(g, m):
        slot = g % ATTN_SCORE_SLOTS
        if g % n == 0:
            m = jnp.full((1, tq), -jnp.inf, F32)
        mx = s_ref[slot, 0:R, :]
        for r in range(R, tk, R):
            mx = jnp.maximum(mx, s_ref[slot, r:r + R, :])
        m_new = jnp.maximum(m, jnp.max(mx, axis=0, keepdims=True))
        for r in range(0, tk, R):
            p_ref[g % 2, r:r + R, :] = jnp.exp2(s_ref[slot, r:r + R, :] - m_new).astype(BF16)
        return m_new, jnp.exp2(m - m_new)

    total = blocks * n
    acc_ref[...] = jnp.zeros_like(acc_ref)
    scores(0)
    scores(1)
    m, a = softmax(0, None)
    scores(2)
    for g in range(1, total):
        if g + 2 < total:
            scores(g + 2)
        weighted(g - 1, a)
        m, a = softmax(g, m)
    weighted(total - 1, a)


def _attention(qt, k, vt, batch, tq, tk, blocks):
    T = k.shape[0]
    L = T // batch
    nq = L // (tq * blocks)
    assert blocks * (L // tk) >= 3
    return pl.pallas_call(
        functools.partial(_attn_kernel, tk, tq),
        grid=(batch, MLA_HEADS, nq),
        in_specs=[
            pl.BlockSpec((MLA_D_PAD, blocks * tq), lambda b, h, i: (h, b * nq + i)),
            pl.BlockSpec((L, MLA_D_PAD), lambda b, h, i: (b, h)),
            pl.BlockSpec((MLA_V_ROWS, L), lambda b, h, i: (h, b)),
        ],
        out_specs=pl.BlockSpec((MLA_D_V, blocks * tq), lambda b, h, i: (h, b * nq + i)),
        out_shape=jax.ShapeDtypeStruct((MLA_HEADS * MLA_D_V, T), BF16),
        scratch_shapes=[pltpu.VMEM((ATTN_SCORE_SLOTS, tk, tq), F32), pltpu.VMEM((2, tk, tq), BF16),
                        pltpu.VMEM((blocks, MLA_V_ROWS, tq), F32)],
        compiler_params=_params(("parallel", "parallel", "arbitrary")),
        name="attention",
    )(qt, k, vt)


def _gate_lane_perm():
    lanes = jnp.arange(GDN_GATE_LANES)
    q, e = lanes // 8, lanes % 8
    return GDN_V_HEADS * (e // 2) + 2 * q + e % 2


def _gate_lane_params(a_log, dt_bias):
    lanes = jnp.arange(GDN_GATE_LANES)
    q, e = lanes // 8, lanes % 8
    head = 2 * q + e % 2
    direction = jnp.maximum(e // 2 - 2, 0)
    is_decay = e >= 4
    neg_a = jnp.where(is_decay, -jnp.exp(a_log.astype(F32))[direction, head], 0.0)
    dt = jnp.where(is_decay, dt_bias.astype(F32)[direction, head], 0.0)
    return neg_a, dt


def _pad_rope_cols(w):
    half = MLA_D_ROPE // 2
    z = jnp.zeros(w.shape[:-1] + (half,), w.dtype)
    return jnp.concatenate([w[..., :half], z, w[..., half:], z], axis=-1)


def _rope_tables(L):
    half = MLA_D_ROPE // 2
    inv_freq = ROPE_THETA ** (-jnp.arange(half, dtype=F32) / half)
    ang = jnp.arange(L, dtype=F32)[:, None] * inv_freq[None, :]
    c, s = jnp.cos(ang), jnp.sin(ang)
    z = jnp.zeros_like(c)
    return jnp.concatenate([c, z, c, z], axis=-1), jnp.concatenate([-s, z, s, z], axis=-1)


def _prepare(p):
    w = {}
    g_in = p['gdn_w_in'][0]
    w['gdn_main'] = g_in[:, :GDN_MAIN_DIM].astype(BF16)
    wg = g_in[:, GDN_MAIN_DIM:][:, _gate_lane_perm()].astype(BF16)
    w['gdn_gate'] = wg
    w['gdn_gate_t'] = wg.T
    w['gdn_neg_a'], w['gdn_dt'] = _gate_lane_params(p['gdn_a_log'][0], p['gdn_dt_bias'][0])
    w['gdn_conv'] = p['gdn_conv_w'][0].astype(F32)
    w['gdn_norm'] = p['gdn_norm_w'][0].reshape(1, GDN_DK).astype(F32)
    w['gdn_out'] = p['gdn_w_out'][0].astype(BF16)

    wa = p['mla_w_a'][0]
    rank = MLA_Q_RANK + MLA_KV_RANK
    w['mla_a'] = jnp.concatenate([wa[:, :rank], _pad_rope_cols(wa[:, rank:])], axis=-1).astype(BF16)
    wq = p['mla_w_q_b'][0].reshape(MLA_Q_RANK, MLA_HEADS, MLA_D_QK)
    wq = jnp.concatenate([wq[..., :MLA_D_NOPE], _pad_rope_cols(wq[..., MLA_D_NOPE:])], axis=-1)
    w['mla_q_t'] = wq.reshape(MLA_Q_RANK, MLA_HEADS * MLA_D_PAD).astype(BF16).T
    wkv = p['mla_w_kv_b'][0].reshape(MLA_KV_RANK, MLA_HEADS, MLA_D_NOPE + MLA_D_V)
    w['mla_kn'] = wkv[..., :MLA_D_NOPE].reshape(MLA_KV_RANK, MLA_HEADS * MLA_D_NOPE).astype(BF16)
    w['mla_v_t'] = wkv[..., MLA_D_NOPE:].reshape(MLA_KV_RANK, MLA_HEADS * MLA_D_V).astype(BF16).T
    w['mla_qn'] = p['mla_q_a_norm'][0].reshape(1, MLA_Q_RANK).astype(F32)
    w['mla_kvn'] = p['mla_kv_a_norm'][0].reshape(1, MLA_KV_RANK).astype(F32)
    w['mla_o'] = p['mla_w_o'][0].astype(BF16)

    w['ffn_in'] = p['ffn_w_in'].astype(BF16)
    w['ffn_out'] = p['ffn_w_out'].astype(BF16)
    for name in ('norm_mix_pre', 'norm_mix_post', 'norm_ffn_pre', 'norm_ffn_post'):
        w[name] = p[name].astype(F32)[:, None, :]
    return w


TILES = dict(
    proj_tm=1024, proj_tn=2048,
    gate_tr=512,
    conv_tr=512, conv_tc=1024,
    solve_chunks=16,
    sweep_chunks=4, sweep_heads=16,
    out_tm=512,
    ffn_tm=512, ffn_tf=1024,
    mla_tm=512,
    attn_tq=256, attn_tk=512, attn_items=64,
)


def _trunk(x3, w, tiles):
    B, L, D = x3.shape
    x = x3.reshape(B * L, D)

    pre, gates, gates_t = _gdn_proj(x, w['norm_mix_pre'][0], w['gdn_main'], w['gdn_gate'], w['gdn_gate_t'],
                                    tiles['proj_tm'], tiles['proj_tn'])
    gsum, gsum_t = _gdn_gates(gates, gates_t, w['gdn_neg_a'], w['gdn_dt'], tiles['gate_tr'])
    qkv = _gdn_conv(pre, w['gdn_conv'], L, tiles['conv_tr'], tiles['conv_tc'])
    solved = _gdn_delta_solve(qkv, gsum, gsum_t, tiles['solve_chunks'])
    sweep = functools.partial(_gdn_delta_sweep, gsum_t=gsum_t, batch=B, chunks=tiles['sweep_chunks'],
                              heads=tiles['sweep_heads'])
    o_fwd = sweep(False, *solved[:5])
    o = sweep(True, *solved[5:], extra=(o_fwd, pre, w['gdn_norm']))
    x = _out_proj(o, w['gdn_out'], w['norm_mix_post'][0], x, tiles['out_tm'])
    x = _ffn(x, w['norm_ffn_pre'][0], w['ffn_in'], w['ffn_out'], w['norm_ffn_post'][0], 0,
             tiles['ffn_tm'], tiles['ffn_tf'])

    cos, sin = _rope_tables(L)
    qt, k, vt = _mla_proj(x, w['norm_mix_pre'][1], w['mla_a'], w['mla_qn'], w['mla_kvn'], w['mla_q_t'],
                          w['mla_kn'], w['mla_v_t'], cos, sin, L, tiles['mla_tm'])
    blocks = max(1, tiles['attn_items'] * tiles['attn_tk'] // L)
    ot = _attention(qt, k, vt, B, tiles['attn_tq'], tiles['attn_tk'], blocks)
    x = _out_proj(ot, w['mla_o'], w['norm_mix_post'][1], x, tiles['out_tm'], transposed=True)
    x = _ffn(x, w['norm_ffn_pre'][1], w['ffn_in'], w['ffn_out'], w['norm_ffn_post'][1], 1,
             tiles['ffn_tm'], tiles['ffn_tf'])
    return x.reshape(B, L, D)


def kernel(x_prompt, x_sample, norm_mix_pre, norm_mix_post, norm_ffn_pre, norm_ffn_post, gdn_w_in, gdn_conv_w, gdn_a_log, gdn_dt_bias, gdn_norm_w, gdn_w_out, mla_w_a, mla_q_a_norm, mla_w_q_b, mla_kv_a_norm, mla_w_kv_b, mla_w_o, ffn_w_in, ffn_w_out):
    w = _prepare(dict(
        norm_mix_pre=norm_mix_pre, norm_mix_post=norm_mix_post, norm_ffn_pre=norm_ffn_pre,
        norm_ffn_post=norm_ffn_post, gdn_w_in=gdn_w_in, gdn_conv_w=gdn_conv_w, gdn_a_log=gdn_a_log,
        gdn_dt_bias=gdn_dt_bias, gdn_norm_w=gdn_norm_w, gdn_w_out=gdn_w_out, mla_w_a=mla_w_a,
        mla_q_a_norm=mla_q_a_norm, mla_w_q_b=mla_w_q_b, mla_kv_a_norm=mla_kv_a_norm,
        mla_w_kv_b=mla_w_kv_b, mla_w_o=mla_w_o, ffn_w_in=ffn_w_in, ffn_w_out=ffn_w_out))
    return _trunk(x_prompt, w, TILES), _trunk(x_sample, w, TILES)
```

```python
import functools
import math

import jax
import jax.numpy as jnp
from jax import lax
from jax.experimental import pallas as pl
from jax.experimental.pallas import tpu as pltpu

F32 = jnp.float32
BF16 = jnp.bfloat16

RMS_EPS = 1e-6
L2_EPS = 1e-6
LANES = 128
BF16_ROWS = 16

GDN_QK_HEADS = 16
GDN_V_HEADS = 32
GDN_DK = 128
GDN_CONV = 5
GDN_CHUNK = 64
GDN_Q_DIM = GDN_QK_HEADS * GDN_DK
GDN_V_DIM = GDN_V_HEADS * GDN_DK
GDN_CONV_DIM = 2 * GDN_Q_DIM + GDN_V_DIM
GDN_MAIN_DIM = GDN_CONV_DIM + GDN_V_DIM
GDN_GATE_LANES = 4 * GDN_V_HEADS

MLA_HEADS = 16
MLA_Q_RANK = 768
MLA_KV_RANK = 512
MLA_D_NOPE = 128
MLA_D_ROPE = 64
MLA_D_V = 128
MLA_D_QK = MLA_D_NOPE + MLA_D_ROPE
MLA_D_PAD = 2 * LANES
MLA_V_ROWS = MLA_D_V + BF16_ROWS
ROPE_THETA = 10000.0

VMEM_LIMIT = 56 * 1024 * 1024


def _params(sem):
    return pltpu.CompilerParams(dimension_semantics=sem, vmem_limit_bytes=VMEM_LIMIT)


def _resident(shape, index_map):
    return pl.BlockSpec(shape, index_map, pipeline_mode=pl.Buffered(1))


def _rms(x, w):
    return x * lax.rsqrt(jnp.mean(x * x, axis=-1, keepdims=True) + RMS_EPS) * w


def _dot(a, b):
    return jnp.dot(a, b, preferred_element_type=F32)


def _dot_nt(a, b):
    return lax.dot_general(a, b, (((1,), (1,)), ((), ())), preferred_element_type=F32)


def _dot_tn(a, b):
    return lax.dot_general(a, b, (((0,), (0,)), ((), ())), preferred_element_type=F32)


def _split3(x):
    hi = x.astype(BF16)
    r1 = x - hi.astype(F32)
    mid = r1.astype(BF16)
    lo = (r1 - mid.astype(F32)).astype(BF16)
    return hi, mid, lo


def _silu(x):
    return x * (1.0 / (1.0 + jnp.exp(-x)))


def _gdn_proj_kernel(x_ref, nw_ref, w_ref, wg_ref, wgt_ref, o_ref, g_ref, gt_ref, xn_ref):
    @pl.when(pl.program_id(1) == 0)
    def _():
        xn = _rms(x_ref[...], nw_ref[...]).astype(BF16)
        xn_ref[...] = xn
        g_ref[...] = _dot(xn, wg_ref[...])
        gt_ref[...] = _dot_nt(wgt_ref[...], xn)

    o_ref[...] = _dot(xn_ref[...], w_ref[...]).astype(BF16)


def _gdn_proj(x, nw, w, wg, wgt, tm, tn):
    T, D = x.shape
    N = w.shape[1]
    return pl.pallas_call(
        _gdn_proj_kernel,
        grid=(T // tm, N // tn),
        in_specs=[
            pl.BlockSpec((tm, D), lambda i, j: (i, 0)),
            pl.BlockSpec((1, D), lambda i, j: (0, 0)),
            pl.BlockSpec((D, tn), lambda i, j: (0, j)),
            pl.BlockSpec((D, LANES), lambda i, j: (0, 0)),
            pl.BlockSpec((LANES, D), lambda i, j: (0, 0)),
        ],
        out_specs=[
            pl.BlockSpec((tm, tn), lambda i, j: (i, j)),
            pl.BlockSpec((tm, LANES), lambda i, j: (i, 0)),
            pl.BlockSpec((LANES, tm), lambda i, j: (0, i)),
        ],
        out_shape=[
            jax.ShapeDtypeStruct((T, N), BF16),
            jax.ShapeDtypeStruct((T, LANES), F32),
            jax.ShapeDtypeStruct((LANES, T), F32),
        ],
        scratch_shapes=[pltpu.VMEM((tm, D), BF16)],
        compiler_params=_params(("parallel", "arbitrary")),
        name="gdn_proj",
    )(x, nw, w, wg, wgt)


def _softplus(y):
    return jnp.maximum(y, 0.0) + jnp.log1p(jnp.exp(-jnp.abs(y)))


def _gate_values(x, neg_a, dt):
    beta = 1.0 / (1.0 + jnp.exp(-x))
    g = neg_a * _softplus(x + dt)
    return beta, g


def _gdn_gates_kernel(x_ref, xt_ref, na_ref, dt_ref, nat_ref, dtt_ref, o_ref, ot_ref):
    R = x_ref.shape[0]
    ii = lax.broadcasted_iota(jnp.int32, (R, R), 0)
    jj = lax.broadcasted_iota(jnp.int32, (R, R), 1)
    same = (ii // GDN_CHUNK) == (jj // GDN_CHUNK)
    lower = jnp.where(same & (jj <= ii), 1.0, 0.0).astype(BF16)
    upper = jnp.where(same & (jj >= ii), 1.0, 0.0).astype(BF16)

    beta, g = _gate_values(x_ref[...], na_ref[...], dt_ref[...])
    parts = _split3(g)
    cf = sum(_dot(lower, p) for p in parts)
    cb = sum(_dot(upper, p) for p in parts)
    e = lax.broadcasted_iota(jnp.int32, (R, LANES), 1) % 8
    o_ref[...] = jnp.where(e < 4, beta, jnp.where(e < 6, cf, cb))

    beta_t, g_t = _gate_values(xt_ref[...], nat_ref[...], dtt_ref[...])
    parts_t = _split3(g_t)
    cf_t = sum(_dot(p, upper) for p in parts_t)
    cb_t = sum(_dot(p, lower) for p in parts_t)
    e_t = lax.broadcasted_iota(jnp.int32, (LANES, R), 0) % 8
    ot_ref[...] = jnp.where(e_t < 4, beta_t, jnp.where(e_t < 6, cf_t, cb_t))


def _gdn_gates(gates, gates_t, neg_a, dt, tr):
    T = gates.shape[0]
    row = lambda i: (i, 0)
    col = lambda i: (0, i)
    fixed = lambda i: (0, 0)
    return pl.pallas_call(
        _gdn_gates_kernel,
        grid=(T // tr,),
        in_specs=[
            pl.BlockSpec((tr, LANES), row),
            pl.BlockSpec((LANES, tr), col),
            pl.BlockSpec((1, LANES), fixed),
            pl.BlockSpec((1, LANES), fixed),
            pl.BlockSpec((LANES, 1), fixed),
            pl.BlockSpec((LANES, 1), fixed),
        ],
        out_specs=[pl.BlockSpec((tr, LANES), row), pl.BlockSpec((LANES, tr), col)],
        out_shape=[jax.ShapeDtypeStruct((T, LANES), F32), jax.ShapeDtypeStruct((LANES, T), F32)],
        compiler_params=_params(("parallel",)),
        name="gdn_gates",
    )(gates, gates_t, neg_a.reshape(1, LANES), dt.reshape(1, LANES),
      neg_a.reshape(LANES, 1), dt.reshape(LANES, 1))


CONV_HALO = BF16_ROWS
CONV_PAD = 8


def _gdn_conv_kernel(blocks_per_seq, n_q_blocks, n_qk_blocks,
                     prev_ref, main_ref, next_ref, w_ref, o_ref):
    i = pl.program_id(0)
    j = pl.program_id(1)
    tr, tc = main_ref.shape
    r = GDN_CONV // 2
    pos = i % blocks_per_seq
    keep_prev = jnp.where(pos == 0, 0.0, 1.0)
    keep_next = jnp.where(pos == blocks_per_seq - 1, 0.0, 1.0)
    rows = tr + 2 * CONV_PAD

    def conv_silu(cols):
        xs = jnp.concatenate([
            prev_ref[CONV_HALO - CONV_PAD:, cols].astype(F32) * keep_prev,
            main_ref[:, cols].astype(F32),
            next_ref[:CONV_PAD, cols].astype(F32) * keep_next], axis=0)
        w = w_ref[:, cols]
        acc = xs[CONV_PAD:CONV_PAD + tr] * w[r:r + 1, :]
        for t in range(GDN_CONV):
            if t != r:
                shifted = pltpu.roll(xs, (r - t) % rows, axis=0)
                acc = acc + shifted[CONV_PAD:CONV_PAD + tr] * w[t:t + 1, :]
        return _silu(acc)

    @pl.when(j >= n_qk_blocks)
    def _():
        for h in range(tc // LANES):
            cols = slice(h * LANES, (h + 1) * LANES)
            o_ref[:, cols] = conv_silu(cols).astype(BF16)

    @pl.when(j < n_qk_blocks)
    def _():
        q_scale = jnp.where(j < n_q_blocks, GDN_DK ** -0.5, 1.0)
        for h in range(tc // LANES):
            cols = slice(h * LANES, (h + 1) * LANES)
            blk = conv_silu(cols)
            inv = lax.rsqrt(jnp.sum(blk * blk, axis=-1, keepdims=True) + L2_EPS) * q_scale
            o_ref[:, cols] = (blk * inv).astype(BF16)


def _gdn_conv(pre, conv_w, seq_len, tr, tc):
    T = pre.shape[0]
    C = conv_w.shape[1]
    hb = tr // CONV_HALO
    n_halo = T // CONV_HALO
    kern = functools.partial(_gdn_conv_kernel, seq_len // tr, GDN_Q_DIM // tc, 2 * GDN_Q_DIM // tc)
    return pl.pallas_call(
        kern,
        grid=(T // tr, C // tc),
        in_specs=[
            pl.BlockSpec((CONV_HALO, tc), lambda i, j: (jnp.maximum(i * hb - 1, 0), j)),
            pl.BlockSpec((tr, tc), lambda i, j: (i, j)),
            pl.BlockSpec((CONV_HALO, tc), lambda i, j: (jnp.minimum((i + 1) * hb, n_halo - 1), j)),
            pl.BlockSpec((GDN_CONV, tc), lambda i, j: (0, j)),
        ],
        out_specs=pl.BlockSpec((tr, tc), lambda i, j: (i, j)),
        out_shape=jax.ShapeDtypeStruct((T, C), BF16),
        compiler_params=_params(("parallel", "parallel")),
        name="gdn_conv",
    )(pre, pre, pre, conv_w)


NEUMANN_DOUBLINGS = 5


def _gate_lane(direction, v_head, decay):
    return (4 if decay else 0) + 2 * direction + v_head


def _delta_solve_kernel(chunks, q_ref, k_ref, v_ref, g_ref, gt_ref,
                        uf_ref, wf_ref, pf_ref, qf_ref, kf_ref, ub_ref, wb_ref, pb_ref, qb_ref, kb_ref):
    C = GDN_CHUNK
    W = 4 * C
    qh = pl.program_id(1)
    gsel = pltpu.roll(g_ref[...], (LANES - 8 * qh) % LANES, axis=1)
    ii = lax.broadcasted_iota(jnp.int32, (C, W), 0)
    ll = lax.broadcasted_iota(jnp.int32, (C, W), 1)
    jj = ll % C
    group = ll // C
    ahead = jnp.where(group >= 2, jj - ii, ii - jj)
    incl = ahead >= 0
    strict = ahead > 0
    eye = jnp.where(ii == jj, 1.0, 0.0)
    own = [group == p for p in range(4)]
    low_half = lax.broadcasted_iota(jnp.int32, (C, LANES), 1) < C
    outs = ((uf_ref, wf_ref, pf_ref, qf_ref, kf_ref), (ub_ref, wb_ref, pb_ref, qb_ref, kb_ref))
    eye_k = jnp.where(lax.broadcasted_iota(jnp.int32, (GDN_DK, GDN_DK), 0)
                      == lax.broadcasted_iota(jnp.int32, (GDN_DK, GDN_DK), 1), 1.0, 0.0).astype(BF16)
    zeros = jnp.zeros((C, 2 * LANES), BF16)

    def side_by_side(cols):
        return jnp.concatenate([jnp.where(low_half, cols[0], cols[1]),
                                jnp.where(low_half, cols[2], cols[3])], axis=1)

    def block_diag(x):
        return jnp.concatenate([jnp.where(own[p], x, 0.0).astype(BF16) for p in range(4)], axis=0)

    ts, xs, rhs = [], [], []
    for ci in range(chunks):
        rows = slice(ci * C, (ci + 1) * C)
        kc = k_ref[rows, :]
        qc = q_ref[rows, :]
        kq = _dot_nt(jnp.concatenate([kc, qc, eye_k], axis=0), jnp.concatenate([kc] * 4, axis=0))
        betas = [jnp.broadcast_to(gsel[rows, p:p + 1], (C, LANES)) for p in range(4)]
        gccs = [jnp.broadcast_to(gsel[rows, 4 + p:5 + p], (C, LANES)) for p in range(4)]
        g_pair = gt_ref[:, (ci // 2) * 2 * C:(ci // 2 + 1) * 2 * C]
        g_swap = pltpu.roll(g_pair, C, axis=1)
        on_low, on_high = (g_pair, g_swap) if ci % 2 == 0 else (g_swap, g_pair)
        gcrs = [on_low[4 + p:5 + p, :C] for p in range(4)]
        gcr_rows = [jnp.where(low_half[0:1], on_low[4 + p:5 + p], on_high[5 + p:6 + p]) for p in (0, 2)]
        decay = jnp.concatenate([jnp.where(low_half, gccs[p], gccs[p + 1]) - gcr_rows[p // 2]
                                 for p in (0, 2)], axis=1)
        decay = jnp.where(incl, jnp.exp(jnp.where(incl, decay, 0.0)), 0.0)
        a = jnp.where(strict, side_by_side(betas) * kq[:C] * decay, 0.0)
        ts.append(eye - a)
        xs.append(a)
        pm = (kq[C:2 * C] * decay).astype(BF16)
        pf_ref[rows, :] = pm[:, :2 * C]
        pb_ref[rows, :] = pm[:, 2 * C:]
        kcf = kc.astype(F32)
        qcf = qc.astype(F32)
        for p in range(4):
            d, hv = divmod(p, 2)
            cols = slice(hv * LANES, (hv + 1) * LANES)
            gl = gcrs[p][:, 0:1] if d else gcrs[p][:, C - 1:C]
            eg = jnp.exp(gccs[p])
            outs[d][3][rows, cols] = (qcf * eg).astype(BF16)
            outs[d][4][ci // 2, cols, (ci % 2) * C:(ci % 2 + 1) * C] = (
                kq[2 * C:, :C] * jnp.exp(gl - gcrs[p])).astype(BF16)
            vb = v_ref[rows, cols].astype(F32) * betas[p]
            kb = kcf * (betas[p] * eg)
            rhs.append(jnp.concatenate([vb, kb], axis=1).astype(BF16))

    xs = [_dot(x.astype(BF16), block_diag(x)) for x in xs]
    for step in range(NEUMANN_DOUBLINGS):
        if step + 1 < NEUMANN_DOUBLINGS:
            both = [_dot(jnp.concatenate([t, x], axis=0).astype(BF16), block_diag(x)) for t, x in zip(ts, xs)]
            ts = [t + b[:C] for t, b in zip(ts, both)]
            xs = [b[C:] for b in both]
        else:
            ts = [t + _dot(t.astype(BF16), block_diag(x)) for t, x in zip(ts, xs)]

    for ci in range(chunks):
        rows = slice(ci * C, (ci + 1) * C)
        for p in range(4):
            d, hv = divmod(p, 2)
            cols = slice(hv * LANES, (hv + 1) * LANES)
            lhs = jnp.where(own[p], ts[ci], 0.0).astype(BF16)
            padded = jnp.concatenate([zeros] * p + [rhs[4 * ci + p]] + [zeros] * (3 - p), axis=0)
            uw = _dot(lhs, padded)
            outs[d][0][rows, cols] = uw[:, :LANES].astype(BF16)
            outs[d][1][rows, cols] = uw[:, LANES:].astype(BF16)


def _gdn_delta_solve(qkv, gsum, gsum_t, chunks):
    T = qkv.shape[0]
    R = chunks * GDN_CHUNK
    k_block0 = GDN_Q_DIM // LANES
    v_block0 = 2 * GDN_Q_DIM // (2 * LANES)
    wide = pl.BlockSpec((R, 2 * LANES), lambda i, h: (i, h))
    narrow = pl.BlockSpec((R, LANES), lambda i, h: (i, h))
    tall = pl.BlockSpec((chunks // 2, 2 * GDN_DK, 2 * GDN_CHUNK), lambda i, h: (i, h, 0))
    uw_shape = jax.ShapeDtypeStruct((T, GDN_V_DIM), BF16)
    p_shape = jax.ShapeDtypeStruct((T, GDN_V_HEADS * GDN_CHUNK), BF16)
    kt_shape = jax.ShapeDtypeStruct((T // (2 * GDN_CHUNK), GDN_V_HEADS * GDN_DK, 2 * GDN_CHUNK), BF16)
    return pl.pallas_call(
        functools.partial(_delta_solve_kernel, chunks),
        grid=(T // R, GDN_QK_HEADS),
        in_specs=[
            narrow,
            pl.BlockSpec((R, LANES), lambda i, h: (i, k_block0 + h)),
            pl.BlockSpec((R, 2 * LANES), lambda i, h: (i, v_block0 + h)),
            pl.BlockSpec((R, LANES), lambda i, h: (i, 0)),
            pl.BlockSpec((8, R), lambda i, h: (h, i)),
        ],
        out_specs=[wide, wide, narrow, wide, tall] * 2,
        out_shape=[uw_shape, uw_shape, p_shape, uw_shape, kt_shape] * 2,
        compiler_params=_params(("parallel", "parallel")),
        name="gdn_delta_solve",
    )(qkv, qkv, qkv, gsum, gsum_t)


def _delta_sweep_kernel(rev, chunks, heads, *refs):
    if rev:
        u_ref, w_ref, p_ref, qd_ref, kdt_ref, gt_ref, of_ref, z_ref, nw_ref, o_ref, s_ref = refs
    else:
        u_ref, w_ref, p_ref, qd_ref, kdt_ref, gt_ref, o_ref, s_ref = refs
    C = GDN_CHUNK
    d = 1 if rev else 0

    @pl.when(pl.program_id(2) == 0)
    def _():
        s_ref[...] = jnp.zeros_like(s_ref)

    states = range(2 * heads)
    zc = jnp.zeros((C, GDN_DK), BF16)
    zs = jnp.zeros((GDN_DK, GDN_DK), BF16)
    diag = lambda a, b, z: jnp.concatenate([jnp.concatenate([a, z], axis=1),
                                            jnp.concatenate([z, b], axis=1)], axis=0)
    order = range(chunks - 1, -1, -1) if rev else range(chunks)
    for ci in order:
        rows = slice(ci * C, (ci + 1) * C)
        last = ci * C + (0 if rev else C - 1)
        cols = [slice(h * LANES, (h + 1) * LANES) for h in states]
        both = [slice(2 * j * LANES, (2 * j + 2) * LANES) for j in range(heads)]
        s_old = [s_ref[h] for h in states]
        wqs = [_dot(jnp.concatenate([w_ref[rows, both[j]], qd_ref[rows, both[j]]], axis=0),
                    diag(s_old[2 * j].astype(BF16), s_old[2 * j + 1].astype(BF16), zs)) for j in range(heads)]
        vns = [(u_ref[rows, both[j]].astype(F32) - wqs[j][:C]).astype(BF16) for j in range(heads)]
        vn = [vns[h // 2][:, (h % 2) * LANES:(h % 2 + 1) * LANES] for h in states]
        pvs = [_dot(p_ref[rows, j * LANES:(j + 1) * LANES], diag(vn[2 * j], vn[2 * j + 1], zc))
               for j in range(heads)]
        kvs = [_dot(kdt_ref[ci // 2, cols[h], :], jnp.concatenate([zc, vn[h]] if ci % 2 else [vn[h], zc], axis=0))
               for h in states]
        for h in states:
            lg = 8 * (h // 2) + _gate_lane(d, h % 2, True)
            s_ref[h] = s_old[h] * jnp.exp(gt_ref[lg:lg + 1, last:last + 1]) + kvs[h]
            half = slice((h % 2) * LANES, (h % 2 + 1) * LANES)
            o = wqs[h // 2][C:, half] + pvs[h // 2][:, half]
            if rev:
                tot = of_ref[rows, cols[h]].astype(F32) + o
                gate = _silu(z_ref[rows, cols[h]].astype(F32))
                o_ref[rows, cols[h]] = (_rms(tot, nw_ref[...]) * gate).astype(BF16)
            else:
                o_ref[rows, cols[h]] = o.astype(BF16)


def _gdn_delta_sweep(rev, u, w, p, qd, kdt, gsum_t, batch, chunks, heads, extra=None):
    T = u.shape[0]
    R = chunks * GDN_CHUNK
    nb = T // batch // R
    assert chunks % 2 == 0
    rows = (lambda b, n: b * nb + (nb - 1 - n)) if rev else (lambda b, n: b * nb + n)
    qk_w = heads * LANES
    v_w = 2 * heads * LANES
    wide = pl.BlockSpec((R, v_w), lambda b, h, n: (rows(b, n), h))
    narrow = pl.BlockSpec((R, qk_w), lambda b, h, n: (rows(b, n), h))
    in_specs = [
        wide, wide, narrow, wide,
        pl.BlockSpec((chunks // 2, v_w, 2 * GDN_CHUNK), lambda b, h, n: (rows(b, n), h, 0)),
        pl.BlockSpec((8 * heads, R), lambda b, h, n: (h, rows(b, n))),
    ]
    args = [u, w, p, qd, kdt, gsum_t]
    if rev:
        o_fwd, pre, norm_w = extra
        z_block0 = GDN_CONV_DIM // v_w
        in_specs += [
            wide,
            pl.BlockSpec((R, v_w), lambda b, h, n: (rows(b, n), z_block0 + h)),
            pl.BlockSpec((1, LANES), lambda b, h, n: (0, 0)),
        ]
        args += [o_fwd, pre, norm_w]
    return pl.pallas_call(
        functools.partial(_delta_sweep_kernel, rev, chunks, heads),
        grid=(batch, GDN_QK_HEADS // heads, nb),
        in_specs=in_specs,
        out_specs=wide,
        out_shape=jax.ShapeDtypeStruct((T, GDN_V_DIM), BF16),
        scratch_shapes=[pltpu.VMEM((2 * heads, GDN_DK, GDN_DK), F32)],
        compiler_params=_params(("parallel", "parallel", "arbitrary")),
        name="gdn_sweep_bwd" if rev else "gdn_sweep_fwd",
    )(*args)


def _out_proj_kernel(transposed, a_ref, w_ref, nw_ref, x_ref, o_ref):
    m = _dot_tn(a_ref[...], w_ref[...]) if transposed else _dot(a_ref[...], w_ref[...])
    o_ref[...] = x_ref[...] + _rms(m, nw_ref[...])


def _out_proj(a, w, nw, x, tm, transposed=False):
    K, D = w.shape
    T = x.shape[0]
    a_spec = pl.BlockSpec((K, tm), lambda i: (0, i)) if transposed else pl.BlockSpec((tm, K), lambda i: (i, 0))
    return pl.pallas_call(
        functools.partial(_out_proj_kernel, transposed),
        grid=(T // tm,),
        in_specs=[
            a_spec,
            _resident((K, D), lambda i: (0, 0)),
            pl.BlockSpec((1, D), lambda i: (0, 0)),
            pl.BlockSpec((tm, D), lambda i: (i, 0)),
        ],
        out_specs=pl.BlockSpec((tm, D), lambda i: (i, 0)),
        out_shape=jax.ShapeDtypeStruct((T, D), F32),
        compiler_params=_params(("parallel",)),
        name="out_proj_t" if transposed else "out_proj",
    )(a, w, nw, x)


def _ffn_kernel(x_ref, nw1_ref, w1_ref, w2_ref, nw2_ref, o_ref, xn_ref, acc_ref):
    j = pl.program_id(1)

    @pl.when(j == 0)
    def _():
        xn_ref[...] = _rms(x_ref[...], nw1_ref[...]).astype(BF16)
        acc_ref[...] = jnp.zeros_like(acc_ref)

    h = jnp.maximum(_dot(xn_ref[...], w1_ref[...]), 0.0)
    acc_ref[...] += _dot((h * h).astype(BF16), w2_ref[...])

    @pl.when(j == pl.num_programs(1) - 1)
    def _():
        o_ref[...] = x_ref[...] + _rms(acc_ref[...], nw2_ref[...])


def _ffn_streamed_kernel(layer, tf, x_ref, nw1_ref, w1_hbm, w2_hbm, nw2_ref, o_ref,
                         w1_buf, w2_buf, sem, xn_ref, acc_ref):
    i = pl.program_id(0)
    nj = w1_hbm.shape[2] // tf

    def copies(j, slot):
        return (pltpu.make_async_copy(w1_hbm.at[layer, :, pl.ds(j * tf, tf)], w1_buf.at[slot], sem.at[0, slot]),
                pltpu.make_async_copy(w2_hbm.at[layer, pl.ds(j * tf, tf), :], w2_buf.at[slot], sem.at[1, slot]))

    def start(j, slot):
        for c in copies(j, slot):
            c.start()

    @pl.when(i == 0)
    def _():
        start(0, 0)

    xn_ref[...] = _rms(x_ref[...], nw1_ref[...]).astype(BF16)
    for j in range(nj):
        slot = j % 2
        for c in copies(j, slot):
            c.wait()
        if j + 1 < nj:
            start(j + 1, 1 - slot)
        else:
            @pl.when(i + 1 < pl.num_programs(0))
            def _():
                start(0, 1 - slot)
        h = jnp.maximum(_dot(xn_ref[...], w1_buf[slot]), 0.0)
        part = _dot((h * h).astype(BF16), w2_buf[slot])
        if j == 0:
            acc_ref[...] = part
        else:
            acc_ref[...] += part
    o_ref[...] = x_ref[...] + _rms(acc_ref[...], nw2_ref[...])


def _ffn_streamed(x, nw1, w1, w2, nw2, layer, tm, tf):
    T, D = x.shape
    assert (w1.shape[2] // tf) % 2 == 0
    return pl.pallas_call(
        functools.partial(_ffn_streamed_kernel, layer, tf),
        grid=(T // tm,),
        in_specs=[
            pl.BlockSpec((tm, D), lambda i: (i, 0)),
            pl.BlockSpec((1, D), lambda i: (0, 0)),
            pl.BlockSpec(memory_space=pl.ANY),
            pl.BlockSpec(memory_space=pl.ANY),
            pl.BlockSpec((1, D), lambda i: (0, 0)),
        ],
        out_specs=pl.BlockSpec((tm, D), lambda i: (i, 0)),
        out_shape=jax.ShapeDtypeStruct((T, D), F32),
        scratch_shapes=[pltpu.VMEM((2, D, tf), BF16), pltpu.VMEM((2, tf, D), BF16),
                        pltpu.SemaphoreType.DMA((2, 2)),
                        pltpu.VMEM((tm, D), BF16), pltpu.VMEM((tm, D), F32)],
        compiler_params=_params(("arbitrary",)),
        name="ffn_streamed",
    )(x, nw1, w1, w2, nw2)


def _ffn(x, nw1, w1, w2, nw2, layer, tm, tf):
    T, D = x.shape
    Fd = w1.shape[2]
    return pl.pallas_call(
        _ffn_kernel,
        grid=(T // tm, Fd // tf),
        in_specs=[
            pl.BlockSpec((tm, D), lambda i, j: (i, 0)),
            pl.BlockSpec((1, D), lambda i, j: (0, 0)),
            pl.BlockSpec((pl.Squeezed(), D, tf), lambda i, j: (layer, 0, j)),
            pl.BlockSpec((pl.Squeezed(), tf, D), lambda i, j: (layer, j, 0)),
            pl.BlockSpec((1, D), lambda i, j: (0, 0)),
        ],
        out_specs=pl.BlockSpec((tm, D), lambda i, j: (i, 0)),
        out_shape=jax.ShapeDtypeStruct((T, D), F32),
        scratch_shapes=[pltpu.VMEM((tm, D), BF16), pltpu.VMEM((tm, D), F32)],
        compiler_params=_params(("parallel", "arbitrary")),
        name="ffn",
    )(x, nw1, w1, w2, nw2)


def _mla_proj_kernel(x_ref, nw_ref, wa_ref, qn_ref, kvn_ref, wqt_ref, wkn_ref, wvt_ref,
                     cos_ref, sin_ref, cost_ref, sint_ref, qt_ref, k_ref, vt_ref):
    xn = _rms(x_ref[...], nw_ref[...]).astype(BF16)
    a = _dot(xn, wa_ref[...])
    cq = _rms(a[:, :MLA_Q_RANK], qn_ref[...]).astype(BF16)
    ckv = _rms(a[:, MLA_Q_RANK:MLA_Q_RANK + MLA_KV_RANK], kvn_ref[...]).astype(BF16)

    kr = a[:, MLA_Q_RANK + MLA_KV_RANK:]
    k_rope = (kr * cos_ref[...] + pltpu.roll(kr, LANES // 2, axis=1) * sin_ref[...]).astype(BF16)
    kn = _dot(ckv, wkn_ref[...])
    for h in range(MLA_HEADS):
        c0 = h * MLA_D_PAD
        k_ref[:, c0:c0 + LANES] = kn[:, h * LANES:(h + 1) * LANES].astype(BF16)
        k_ref[:, c0 + LANES:c0 + 2 * LANES] = k_rope

    scale = MLA_D_QK ** -0.5 * math.log2(math.e)
    qt = _dot_nt(wqt_ref[...], cq)
    cost = cost_ref[...]
    sint = sint_ref[...]
    half = LANES // 2
    for h in range(MLA_HEADS):
        r0 = h * MLA_D_PAD
        qt_ref[r0:r0 + LANES, :] = (qt[r0:r0 + LANES] * scale).astype(BF16)
        blk = qt[r0 + LANES:r0 + 2 * LANES]
        swapped = jnp.concatenate([blk[half:], blk[:half]], axis=0)
        qt_ref[r0 + LANES:r0 + 2 * LANES, :] = ((blk * cost + swapped * sint) * scale).astype(BF16)

    vt = _dot_nt(wvt_ref[...], ckv)
    ones = jnp.ones((MLA_V_ROWS - MLA_D_V, vt.shape[1]), BF16)
    for h in range(MLA_HEADS):
        r0 = h * MLA_V_ROWS
        vt_ref[r0:r0 + MLA_D_V, :] = vt[h * MLA_D_V:(h + 1) * MLA_D_V].astype(BF16)
        vt_ref[r0 + MLA_D_V:r0 + MLA_V_ROWS, :] = ones


def _mla_proj(x, nw, wa, qn, kvn, wqt, wkn, wvt, cos, sin, seq_len, tm):
    T, D = x.shape
    pos_blocks = seq_len // tm
    row = lambda i: (i, 0)
    col = lambda i: (0, i)
    fixed = lambda i: (0, 0)
    return pl.pallas_call(
        _mla_proj_kernel,
        grid=(T // tm,),
        in_specs=[
            pl.BlockSpec((tm, D), row),
            pl.BlockSpec((1, D), fixed),
            _resident(wa.shape, fixed),
            pl.BlockSpec((1, MLA_Q_RANK), fixed),
            pl.BlockSpec((1, MLA_KV_RANK), fixed),
            _resident(wqt.shape, fixed),
            _resident(wkn.shape, fixed),
            _resident(wvt.shape, fixed),
            pl.BlockSpec((tm, LANES), lambda i: (i % pos_blocks, 0)),
            pl.BlockSpec((tm, LANES), lambda i: (i % pos_blocks, 0)),
            pl.BlockSpec((LANES, tm), lambda i: (0, i % pos_blocks)),
            pl.BlockSpec((LANES, tm), lambda i: (0, i % pos_blocks)),
        ],
        out_specs=[
            pl.BlockSpec((MLA_HEADS * MLA_D_PAD, tm), col),
            pl.BlockSpec((tm, MLA_HEADS * MLA_D_PAD), row),
            pl.BlockSpec((MLA_HEADS * MLA_V_ROWS, tm), col),
        ],
        out_shape=[
            jax.ShapeDtypeStruct((MLA_HEADS * MLA_D_PAD, T), BF16),
            jax.ShapeDtypeStruct((T, MLA_HEADS * MLA_D_PAD), BF16),
            jax.ShapeDtypeStruct((MLA_HEADS * MLA_V_ROWS, T), BF16),
        ],
        compiler_params=_params(("parallel",)),
        name="mla_proj",
    )(x, nw, wa, qn, kvn, wqt, wkn, wvt, cos, sin, cos.T, sin.T)


ATTN_SOFTMAX_ROWS = 64
ATTN_SCORE_SLOTS = 4


def _attn_kernel(tk, tq, qt_ref, k_ref, vt_ref, ot_ref, s_ref, p_ref, acc_ref):
    n = k_ref.shape[0] // tk
    blocks = qt_ref.shape[1] // tq
    R = ATTN_SOFTMAX_ROWS

    def keys(g):
        return slice((g % n) * tk, (g % n + 1) * tk)

    def queries(g):
        return slice((g // n) * tq, (g // n + 1) * tq)

    def scores(g):
        s_ref[g % ATTN_SCORE_SLOTS] = _dot(k_ref[keys(g), :], qt_ref[:, queries(g)])

    def weighted(g, a):
        acc = a * acc_ref[g // n] + _dot(vt_ref[:, keys(g)], p_ref[g % 2])
        if g % n == n - 1:
            ot_ref[:, queries(g)] = (acc[:MLA_D_V] / acc[MLA_D_V:MLA_D_V + 1]).astype(BF16)
        else:
            acc_ref[g // n] = acc

    def softmax(g, m):
        slot = g % ATTN_SCORE_SLOTS
        if g % n == 0:
            m = jnp.full((1, tq), -jnp.inf, F32)
        mx = s_ref[slot, 0:R, :]
        for r in range(R, tk, R):
            mx = jnp.maximum(mx, s_ref[slot, r:r + R, :])
        m_new = jnp.maximum(m, jnp.max(mx, axis=0, keepdims=True))
        for r in range(0, tk, R):
            p_ref[g % 2, r:r + R, :] = jnp.exp2(s_ref[slot, r:r + R, :] - m_new).astype(BF16)
        return m_new, jnp.exp2(m - m_new)

    total = blocks * n
    acc_ref[...] = jnp.zeros_like(acc_ref)
    scores(0)
    scores(1)
    m, a = softmax(0, None)
    scores(2)
    for g in range(1, total):
        if g + 2 < total:
            scores(g + 2)
        weighted(g - 1, a)
        m, a = softmax(g, m)
    weighted(total - 1, a)


def _attention(qt, k, vt, batch, tq, tk, blocks):
    T = k.shape[0]
    L = T // batch
    nq = L // (tq * blocks)
    assert blocks * (L // tk) >= 3
    return pl.pallas_call(
        functools.partial(_attn_kernel, tk, tq),
        grid=(batch, MLA_HEADS, nq),
        in_specs=[
            pl.BlockSpec((MLA_D_PAD, blocks * tq), lambda b, h, i: (h, b * nq + i)),
            pl.BlockSpec((L, MLA_D_PAD), lambda b, h, i: (b, h)),
            pl.BlockSpec((MLA_V_ROWS, L), lambda b, h, i: (h, b)),
        ],
        out_specs=pl.BlockSpec((MLA_D_V, blocks * tq), lambda b, h, i: (h, b * nq + i)),
        out_shape=jax.ShapeDtypeStruct((MLA_HEADS * MLA_D_V, T), BF16),
        scratch_shapes=[pltpu.VMEM((ATTN_SCORE_SLOTS, tk, tq), F32), pltpu.VMEM((2, tk, tq), BF16),
                        pltpu.VMEM((blocks, MLA_V_ROWS, tq), F32)],
        compiler_params=_params(("parallel", "parallel", "arbitrary")),
        name="attention",
    )(qt, k, vt)


def _gate_lane_perm():
    lanes = jnp.arange(GDN_GATE_LANES)
    q, e = lanes // 8, lanes % 8
    return GDN_V_HEADS * (e // 2) + 2 * q + e % 2


def _gate_lane_params(a_log, dt_bias):
    lanes = jnp.arange(GDN_GATE_LANES)
    q, e = lanes // 8, lanes % 8
    head = 2 * q + e % 2
    direction = jnp.maximum(e // 2 - 2, 0)
    is_decay = e >= 4
    neg_a = jnp.where(is_decay, -jnp.exp(a_log.astype(F32))[direction, head], 0.0)
    dt = jnp.where(is_decay, dt_bias.astype(F32)[direction, head], 0.0)
    return neg_a, dt


def _pad_rope_cols(w):
    half = MLA_D_ROPE // 2
    z = jnp.zeros(w.shape[:-1] + (half,), w.dtype)
    return jnp.concatenate([w[..., :half], z, w[..., half:], z], axis=-1)


def _rope_tables(L):
    half = MLA_D_ROPE // 2
    inv_freq = ROPE_THETA ** (-jnp.arange(half, dtype=F32) / half)
    ang = jnp.arange(L, dtype=F32)[:, None] * inv_freq[None, :]
    c, s = jnp.cos(ang), jnp.sin(ang)
    z = jnp.zeros_like(c)
    return jnp.concatenate([c, z, c, z], axis=-1), jnp.concatenate([-s, z, s, z], axis=-1)


def _prepare(p):
    w = {}
    g_in = p['gdn_w_in'][0]
    w['gdn_main'] = g_in[:, :GDN_MAIN_DIM].astype(BF16)
    wg = g_in[:, GDN_MAIN_DIM:][:, _gate_lane_perm()].astype(BF16)
    w['gdn_gate'] = wg
    w['gdn_gate_t'] = wg.T
    w['gdn_neg_a'], w['gdn_dt'] = _gate_lane_params(p['gdn_a_log'][0], p['gdn_dt_bias'][0])
    w['gdn_conv'] = p['gdn_conv_w'][0].astype(F32)
    w['gdn_norm'] = p['gdn_norm_w'][0].reshape(1, GDN_DK).astype(F32)
    w['gdn_out'] = p['gdn_w_out'][0].astype(BF16)

    wa = p['mla_w_a'][0]
    rank = MLA_Q_RANK + MLA_KV_RANK
    w['mla_a'] = jnp.concatenate([wa[:, :rank], _pad_rope_cols(wa[:, rank:])], axis=-1).astype(BF16)
    wq = p['mla_w_q_b'][0].reshape(MLA_Q_RANK, MLA_HEADS, MLA_D_QK)
    wq = jnp.concatenate([wq[..., :MLA_D_NOPE], _pad_rope_cols(wq[..., MLA_D_NOPE:])], axis=-1)
    w['mla_q_t'] = wq.reshape(MLA_Q_RANK, MLA_HEADS * MLA_D_PAD).astype(BF16).T
    wkv = p['mla_w_kv_b'][0].reshape(MLA_KV_RANK, MLA_HEADS, MLA_D_NOPE + MLA_D_V)
    w['mla_kn'] = wkv[..., :MLA_D_NOPE].reshape(MLA_KV_RANK, MLA_HEADS * MLA_D_NOPE).astype(BF16)
    w['mla_v_t'] = wkv[..., MLA_D_NOPE:].reshape(MLA_KV_RANK, MLA_HEADS * MLA_D_V).astype(BF16).T
    w['mla_qn'] = p['mla_q_a_norm'][0].reshape(1, MLA_Q_RANK).astype(F32)
    w['mla_kvn'] = p['mla_kv_a_norm'][0].reshape(1, MLA_KV_RANK).astype(F32)
    w['mla_o'] = p['mla_w_o'][0].astype(BF16)

    w['ffn_in'] = p['ffn_w_in'].astype(BF16)
    w['ffn_out'] = p['ffn_w_out'].astype(BF16)
    for name in ('norm_mix_pre', 'norm_mix_post', 'norm_ffn_pre', 'norm_ffn_post'):
        w[name] = p[name].astype(F32)[:, None, :]
    return w


TILES = dict(
    proj_tm=1024, proj_tn=2048,
    gate_tr=512,
    conv_tr=512, conv_tc=1024,
    solve_chunks=16,
    sweep_chunks=4, sweep_heads=16,
    out_tm=512,
    ffn_tm=512, ffn_tf=1024,
    mla_tm=512,
    attn_tq=256, attn_tk=512, attn_items=64,
)


def _trunk(x3, w, tiles):
    B, L, D = x3.shape
    x = x3.reshape(B * L, D)

    pre, gates, gates_t = _gdn_proj(x, w['norm_mix_pre'][0], w['gdn_main'], w['gdn_gate'], w['gdn_gate_t'],
                                    tiles['proj_tm'], tiles['proj_tn'])
    gsum, gsum_t = _gdn_gates(gates, gates_t, w['gdn_neg_a'], w['gdn_dt'], tiles['gate_tr'])
    qkv = _gdn_conv(pre, w['gdn_conv'], L, tiles['conv_tr'], tiles['conv_tc'])
    solved = _gdn_delta_solve(qkv, gsum, gsum_t, tiles['solve_chunks'])
    sweep = functools.partial(_gdn_delta_sweep, gsum_t=gsum_t, batch=B, chunks=tiles['sweep_chunks'],
                              heads=tiles['sweep_heads'])
    o_fwd = sweep(False, *solved[:5])
    o = sweep(True, *solved[5:], extra=(o_fwd, pre, w['gdn_norm']))
    x = _out_proj(o, w['gdn_out'], w['norm_mix_post'][0], x, tiles['out_tm'])
    x = _ffn_streamed(x, w['norm_ffn_pre'][0], w['ffn_in'], w['ffn_out'], w['norm_ffn_post'][0], 0,
             tiles['ffn_tm'], tiles['ffn_tf'])

    cos, sin = _rope_tables(L)
    qt, k, vt = _mla_proj(x, w['norm_mix_pre'][1], w['mla_a'], w['mla_qn'], w['mla_kvn'], w['mla_q_t'],
                          w['mla_kn'], w['mla_v_t'], cos, sin, L, tiles['mla_tm'])
    blocks = max(1, tiles['attn_items'] * tiles['attn_tk'] // L)
    ot = _attention(qt, k, vt, B, tiles['attn_tq'], tiles['attn_tk'], blocks)
    x = _out_proj(ot, w['mla_o'], w['norm_mix_post'][1], x, tiles['out_tm'], transposed=True)
    x = _ffn_streamed(x, w['norm_ffn_pre'][1], w['ffn_in'], w['ffn_out'], w['norm_ffn_post'][1], 1,
             tiles['ffn_tm'], tiles['ffn_tf'])
    return x.reshape(B, L, D)


def kernel(x_prompt, x_sample, norm_mix_pre, norm_mix_post, norm_ffn_pre, norm_ffn_post, gdn_w_in, gdn_conv_w, gdn_a_log, gdn_dt_bias, gdn_norm_w, gdn_w_out, mla_w_a, mla_q_a_norm, mla_w_q_b, mla_kv_a_norm, mla_w_kv_b, mla_w_o, ffn_w_in, ffn_w_out):
    w = _prepare(dict(
        norm_mix_pre=norm_mix_pre, norm_mix_post=norm_mix_post, norm_ffn_pre=norm_ffn_pre,
        norm_ffn_post=norm_ffn_post, gdn_w_in=gdn_w_in, gdn_conv_w=gdn_conv_w, gdn_a_log=gdn_a_log,
        gdn_dt_bias=gdn_dt_bias, gdn_norm_w=gdn_norm_w, gdn_w_out=gdn_w_out, mla_w_a=mla_w_a,
        mla_q_a_norm=mla_q_a_norm, mla_w_q_b=mla_w_q_b, mla_kv_a_norm=mla_kv_a_norm,
        mla_w_kv_b=mla_w_kv_b, mla_w_o=mla_w_o, ffn_w_in=ffn_w_in, ffn_w_out=ffn_w_out))
    return _trunk(x_prompt, w, TILES), _trunk(x_sample, w, TILES)
```

```python
import functools
import math

import jax
import jax.numpy as jnp
from jax import lax
from jax.experimental import pallas as pl
from jax.experimental.pallas import tpu as pltpu

F32 = jnp.float32
BF16 = jnp.bfloat16

RMS_EPS = 1e-6
L2_EPS = 1e-6
LANES = 128
BF16_ROWS = 16

GDN_QK_HEADS = 16
GDN_V_HEADS = 32
GDN_DK = 128
GDN_CONV = 5
GDN_CHUNK = 64
GDN_Q_DIM = GDN_QK_HEADS * GDN_DK
GDN_V_DIM = GDN_V_HEADS * GDN_DK
GDN_CONV_DIM = 2 * GDN_Q_DIM + GDN_V_DIM
GDN_MAIN_DIM = GDN_CONV_DIM + GDN_V_DIM
GDN_GATE_LANES = 4 * GDN_V_HEADS

MLA_HEADS = 16
MLA_Q_RANK = 768
MLA_KV_RANK = 512
MLA_D_NOPE = 128
MLA_D_ROPE = 64
MLA_D_V = 128
MLA_D_QK = MLA_D_NOPE + MLA_D_ROPE
MLA_D_PAD = 2 * LANES
MLA_V_ROWS = MLA_D_V + BF16_ROWS
ROPE_THETA = 10000.0

VMEM_LIMIT = 56 * 1024 * 1024


def _params(sem):
    return pltpu.CompilerParams(dimension_semantics=sem, vmem_limit_bytes=VMEM_LIMIT)


def _resident(shape, index_map):
    return pl.BlockSpec(shape, index_map, pipeline_mode=pl.Buffered(1))


def _rms(x, w):
    return x * lax.rsqrt(jnp.mean(x * x, axis=-1, keepdims=True) + RMS_EPS) * w


def _dot(a, b):
    return jnp.dot(a, b, preferred_element_type=F32)


def _dot_nt(a, b):
    return lax.dot_general(a, b, (((1,), (1,)), ((), ())), preferred_element_type=F32)


def _dot_tn(a, b):
    return lax.dot_general(a, b, (((0,), (0,)), ((), ())), preferred_element_type=F32)


def _split3(x):
    hi = x.astype(BF16)
    r1 = x - hi.astype(F32)
    mid = r1.astype(BF16)
    lo = (r1 - mid.astype(F32)).astype(BF16)
    return hi, mid, lo


def _silu(x):
    return x * (1.0 / (1.0 + jnp.exp(-x)))


def _gdn_proj_kernel(x_ref, nw_ref, w_ref, wg_ref, wgt_ref, o_ref, g_ref, gt_ref, xn_ref):
    @pl.when(pl.program_id(1) == 0)
    def _():
        xn = _rms(x_ref[...], nw_ref[...]).astype(BF16)
        xn_ref[...] = xn
        g_ref[...] = _dot(xn, wg_ref[...])
        gt_ref[...] = _dot_nt(wgt_ref[...], xn)

    o_ref[...] = _dot(xn_ref[...], w_ref[...]).astype(BF16)


def _gdn_proj(x, nw, w, wg, wgt, tm, tn):
    T, D = x.shape
    N = w.shape[1]
    return pl.pallas_call(
        _gdn_proj_kernel,
        grid=(T // tm, N // tn),
        in_specs=[
            pl.BlockSpec((tm, D), lambda i, j: (i, 0)),
            pl.BlockSpec((1, D), lambda i, j: (0, 0)),
            pl.BlockSpec((D, tn), lambda i, j: (0, j)),
            pl.BlockSpec((D, LANES), lambda i, j: (0, 0)),
            pl.BlockSpec((LANES, D), lambda i, j: (0, 0)),
        ],
        out_specs=[
            pl.BlockSpec((tm, tn), lambda i, j: (i, j)),
            pl.BlockSpec((tm, LANES), lambda i, j: (i, 0)),
            pl.BlockSpec((LANES, tm), lambda i, j: (0, i)),
        ],
        out_shape=[
            jax.ShapeDtypeStruct((T, N), BF16),
            jax.ShapeDtypeStruct((T, LANES), F32),
            jax.ShapeDtypeStruct((LANES, T), F32),
        ],
        scratch_shapes=[pltpu.VMEM((tm, D), BF16)],
        compiler_params=_params(("parallel", "arbitrary")),
        name="gdn_proj",
    )(x, nw, w, wg, wgt)


def _softplus(y):
    return jnp.maximum(y, 0.0) + jnp.log1p(jnp.exp(-jnp.abs(y)))


def _gate_values(x, neg_a, dt):
    beta = 1.0 / (1.0 + jnp.exp(-x))
    g = neg_a * _softplus(x + dt)
    return beta, g


def _gdn_gates_kernel(x_ref, xt_ref, na_ref, dt_ref, nat_ref, dtt_ref, o_ref, ot_ref):
    R = x_ref.shape[0]
    ii = lax.broadcasted_iota(jnp.int32, (R, R), 0)
    jj = lax.broadcasted_iota(jnp.int32, (R, R), 1)
    same = (ii // GDN_CHUNK) == (jj // GDN_CHUNK)
    lower = jnp.where(same & (jj <= ii), 1.0, 0.0).astype(BF16)
    upper = jnp.where(same & (jj >= ii), 1.0, 0.0).astype(BF16)

    beta, g = _gate_values(x_ref[...], na_ref[...], dt_ref[...])
    parts = _split3(g)
    cf = sum(_dot(lower, p) for p in parts)
    cb = sum(_dot(upper, p) for p in parts)
    e = lax.broadcasted_iota(jnp.int32, (R, LANES), 1) % 8
    o_ref[...] = jnp.where(e < 4, beta, jnp.where(e < 6, cf, cb))

    beta_t, g_t = _gate_values(xt_ref[...], nat_ref[...], dtt_ref[...])
    parts_t = _split3(g_t)
    cf_t = sum(_dot(p, upper) for p in parts_t)
    cb_t = sum(_dot(p, lower) for p in parts_t)
    e_t = lax.broadcasted_iota(jnp.int32, (LANES, R), 0) % 8
    ot_ref[...] = jnp.where(e_t < 4, beta_t, jnp.where(e_t < 6, cf_t, cb_t))


def _gdn_gates(gates, gates_t, neg_a, dt, tr):
    T = gates.shape[0]
    row = lambda i: (i, 0)
    col = lambda i: (0, i)
    fixed = lambda i: (0, 0)
    return pl.pallas_call(
        _gdn_gates_kernel,
        grid=(T // tr,),
        in_specs=[
            pl.BlockSpec((tr, LANES), row),
            pl.BlockSpec((LANES, tr), col),
            pl.BlockSpec((1, LANES), fixed),
            pl.BlockSpec((1, LANES), fixed),
            pl.BlockSpec((LANES, 1), fixed),
            pl.BlockSpec((LANES, 1), fixed),
        ],
        out_specs=[pl.BlockSpec((tr, LANES), row), pl.BlockSpec((LANES, tr), col)],
        out_shape=[jax.ShapeDtypeStruct((T, LANES), F32), jax.ShapeDtypeStruct((LANES, T), F32)],
        compiler_params=_params(("parallel",)),
        name="gdn_gates",
    )(gates, gates_t, neg_a.reshape(1, LANES), dt.reshape(1, LANES),
      neg_a.reshape(LANES, 1), dt.reshape(LANES, 1))


CONV_HALO = BF16_ROWS
CONV_PAD = 8


def _gdn_conv_kernel(blocks_per_seq, n_q_blocks, n_qk_blocks,
                     prev_ref, main_ref, next_ref, w_ref, o_ref):
    i = pl.program_id(0)
    j = pl.program_id(1)
    tr, tc = main_ref.shape
    r = GDN_CONV // 2
    pos = i % blocks_per_seq
    keep_prev = jnp.where(pos == 0, 0.0, 1.0)
    keep_next = jnp.where(pos == blocks_per_seq - 1, 0.0, 1.0)
    rows = tr + 2 * CONV_PAD

    def conv_silu(cols):
        xs = jnp.concatenate([
            prev_ref[CONV_HALO - CONV_PAD:, cols].astype(F32) * keep_prev,
            main_ref[:, cols].astype(F32),
            next_ref[:CONV_PAD, cols].astype(F32) * keep_next], axis=0)
        w = w_ref[:, cols]
        acc = xs[CONV_PAD:CONV_PAD + tr] * w[r:r + 1, :]
        for t in range(GDN_CONV):
            if t != r:
                shifted = pltpu.roll(xs, (r - t) % rows, axis=0)
                acc = acc + shifted[CONV_PAD:CONV_PAD + tr] * w[t:t + 1, :]
        return _silu(acc)

    @pl.when(j >= n_qk_blocks)
    def _():
        for h in range(tc // LANES):
            cols = slice(h * LANES, (h + 1) * LANES)
            o_ref[:, cols] = conv_silu(cols).astype(BF16)

    @pl.when(j < n_qk_blocks)
    def _():
        q_scale = jnp.where(j < n_q_blocks, GDN_DK ** -0.5, 1.0)
        for h in range(tc // LANES):
            cols = slice(h * LANES, (h + 1) * LANES)
            blk = conv_silu(cols)
            inv = lax.rsqrt(jnp.sum(blk * blk, axis=-1, keepdims=True) + L2_EPS) * q_scale
            o_ref[:, cols] = (blk * inv).astype(BF16)


def _gdn_conv(pre, conv_w, seq_len, tr, tc):
    T = pre.shape[0]
    C = conv_w.shape[1]
    hb = tr // CONV_HALO
    n_halo = T // CONV_HALO
    kern = functools.partial(_gdn_conv_kernel, seq_len // tr, GDN_Q_DIM // tc, 2 * GDN_Q_DIM // tc)
    return pl.pallas_call(
        kern,
        grid=(T // tr, C // tc),
        in_specs=[
            pl.BlockSpec((CONV_HALO, tc), lambda i, j: (jnp.maximum(i * hb - 1, 0), j)),
            pl.BlockSpec((tr, tc), lambda i, j: (i, j)),
            pl.BlockSpec((CONV_HALO, tc), lambda i, j: (jnp.minimum((i + 1) * hb, n_halo - 1), j)),
            pl.BlockSpec((GDN_CONV, tc), lambda i, j: (0, j)),
        ],
        out_specs=pl.BlockSpec((tr, tc), lambda i, j: (i, j)),
        out_shape=jax.ShapeDtypeStruct((T, C), BF16),
        compiler_params=_params(("parallel", "parallel")),
        name="gdn_conv",
    )(pre, pre, pre, conv_w)


NEUMANN_DOUBLINGS = 5


def _gate_lane(direction, v_head, decay):
    return (4 if decay else 0) + 2 * direction + v_head


def _delta_solve_kernel(chunks, q_ref, k_ref, v_ref, g_ref, gt_ref,
                        uf_ref, wf_ref, pf_ref, qf_ref, kf_ref, ub_ref, wb_ref, pb_ref, qb_ref, kb_ref):
    C = GDN_CHUNK
    W = 4 * C
    qh = pl.program_id(1)
    gsel = pltpu.roll(g_ref[...], (LANES - 8 * qh) % LANES, axis=1)
    ii = lax.broadcasted_iota(jnp.int32, (C, W), 0)
    ll = lax.broadcasted_iota(jnp.int32, (C, W), 1)
    jj = ll % C
    group = ll // C
    ahead = jnp.where(group >= 2, jj - ii, ii - jj)
    incl = ahead >= 0
    strict = ahead > 0
    eye = jnp.where(ii == jj, 1.0, 0.0)
    own = [group == p for p in range(4)]
    low_half = lax.broadcasted_iota(jnp.int32, (C, LANES), 1) < C
    outs = ((uf_ref, wf_ref, pf_ref, qf_ref, kf_ref), (ub_ref, wb_ref, pb_ref, qb_ref, kb_ref))
    eye_k = jnp.where(lax.broadcasted_iota(jnp.int32, (GDN_DK, GDN_DK), 0)
                      == lax.broadcasted_iota(jnp.int32, (GDN_DK, GDN_DK), 1), 1.0, 0.0).astype(BF16)
    zeros = jnp.zeros((C, 2 * LANES), BF16)

    def side_by_side(cols):
        return jnp.concatenate([jnp.where(low_half, cols[0], cols[1]),
                                jnp.where(low_half, cols[2], cols[3])], axis=1)

    def block_diag(x):
        return jnp.concatenate([jnp.where(own[p], x, 0.0).astype(BF16) for p in range(4)], axis=0)

    ts, xs, rhs = [], [], []
    for ci in range(chunks):
        rows = slice(ci * C, (ci + 1) * C)
        kc = k_ref[rows, :]
        qc = q_ref[rows, :]
        kq = _dot_nt(jnp.concatenate([kc, qc, eye_k], axis=0), jnp.concatenate([kc] * 4, axis=0))
        betas = [jnp.broadcast_to(gsel[rows, p:p + 1], (C, LANES)) for p in range(4)]
        gccs = [jnp.broadcast_to(gsel[rows, 4 + p:5 + p], (C, LANES)) for p in range(4)]
        g_pair = gt_ref[:, (ci // 2) * 2 * C:(ci // 2 + 1) * 2 * C]
        g_swap = pltpu.roll(g_pair, C, axis=1)
        on_low, on_high = (g_pair, g_swap) if ci % 2 == 0 else (g_swap, g_pair)
        gcrs = [on_low[4 + p:5 + p, :C] for p in range(4)]
        gcr_rows = [jnp.where(low_half[0:1], on_low[4 + p:5 + p], on_high[5 + p:6 + p]) for p in (0, 2)]
        decay = jnp.concatenate([jnp.where(low_half, gccs[p], gccs[p + 1]) - gcr_rows[p // 2]
                                 for p in (0, 2)], axis=1)
        decay = jnp.where(incl, jnp.exp(jnp.where(incl, decay, 0.0)), 0.0)
        a = jnp.where(strict, side_by_side(betas) * kq[:C] * decay, 0.0)
        ts.append(eye - a)
        xs.append(a)
        pm = (kq[C:2 * C] * decay).astype(BF16)
        pf_ref[rows, :] = pm[:, :2 * C]
        pb_ref[rows, :] = pm[:, 2 * C:]
        kcf = kc.astype(F32)
        qcf = qc.astype(F32)
        for p in range(4):
            d, hv = divmod(p, 2)
            cols = slice(hv * LANES, (hv + 1) * LANES)
            gl = gcrs[p][:, 0:1] if d else gcrs[p][:, C - 1:C]
            eg = jnp.exp(gccs[p])
            outs[d][3][rows, cols] = (qcf * eg).astype(BF16)
            outs[d][4][ci // 2, cols, (ci % 2) * C:(ci % 2 + 1) * C] = (
                kq[2 * C:, :C] * jnp.exp(gl - gcrs[p])).astype(BF16)
            vb = v_ref[rows, cols].astype(F32) * betas[p]
            kb = kcf * (betas[p] * eg)
            rhs.append(jnp.concatenate([vb, kb], axis=1).astype(BF16))

    xs = [_dot(x.astype(BF16), block_diag(x)) for x in xs]
    for step in range(NEUMANN_DOUBLINGS):
        if step + 1 < NEUMANN_DOUBLINGS:
            both = [_dot(jnp.concatenate([t, x], axis=0).astype(BF16), block_diag(x)) for t, x in zip(ts, xs)]
            ts = [t + b[:C] for t, b in zip(ts, both)]
            xs = [b[C:] for b in both]
        else:
            ts = [t + _dot(t.astype(BF16), block_diag(x)) for t, x in zip(ts, xs)]

    for ci in range(chunks):
        rows = slice(ci * C, (ci + 1) * C)
        for p in range(4):
            d, hv = divmod(p, 2)
            cols = slice(hv * LANES, (hv + 1) * LANES)
            lhs = jnp.where(own[p], ts[ci], 0.0).astype(BF16)
            padded = jnp.concatenate([zeros] * p + [rhs[4 * ci + p]] + [zeros] * (3 - p), axis=0)
            uw = _dot(lhs, padded)
            outs[d][0][rows, cols] = uw[:, :LANES].astype(BF16)
            outs[d][1][rows, cols] = uw[:, LANES:].astype(BF16)


def _gdn_delta_solve(qkv, gsum, gsum_t, chunks):
    T = qkv.shape[0]
    R = chunks * GDN_CHUNK
    k_block0 = GDN_Q_DIM // LANES
    v_block0 = 2 * GDN_Q_DIM // (2 * LANES)
    wide = pl.BlockSpec((R, 2 * LANES), lambda i, h: (i, h))
    narrow = pl.BlockSpec((R, LANES), lambda i, h: (i, h))
    tall = pl.BlockSpec((chunks // 2, 2 * GDN_DK, 2 * GDN_CHUNK), lambda i, h: (i, h, 0))
    uw_shape = jax.ShapeDtypeStruct((T, GDN_V_DIM), BF16)
    p_shape = jax.ShapeDtypeStruct((T, GDN_V_HEADS * GDN_CHUNK), BF16)
    kt_shape = jax.ShapeDtypeStruct((T // (2 * GDN_CHUNK), GDN_V_HEADS * GDN_DK, 2 * GDN_CHUNK), BF16)
    return pl.pallas_call(
        functools.partial(_delta_solve_kernel, chunks),
        grid=(T // R, GDN_QK_HEADS),
        in_specs=[
            narrow,
            pl.BlockSpec((R, LANES), lambda i, h: (i, k_block0 + h)),
            pl.BlockSpec((R, 2 * LANES), lambda i, h: (i, v_block0 + h)),
            pl.BlockSpec((R, LANES), lambda i, h: (i, 0)),
            pl.BlockSpec((8, R), lambda i, h: (h, i)),
        ],
        out_specs=[wide, wide, narrow, wide, tall] * 2,
        out_shape=[uw_shape, uw_shape, p_shape, uw_shape, kt_shape] * 2,
        compiler_params=_params(("parallel", "parallel")),
        name="gdn_delta_solve",
    )(qkv, qkv, qkv, gsum, gsum_t)


def _delta_sweep_kernel(rev, chunks, heads, *refs):
    if rev:
        u_ref, w_ref, p_ref, qd_ref, kdt_ref, gt_ref, of_ref, z_ref, nw_ref, o_ref, s_ref = refs
    else:
        u_ref, w_ref, p_ref, qd_ref, kdt_ref, gt_ref, o_ref, s_ref = refs
    C = GDN_CHUNK
    d = 1 if rev else 0

    @pl.when(pl.program_id(2) == 0)
    def _():
        s_ref[...] = jnp.zeros_like(s_ref)

    states = range(2 * heads)
    zc = jnp.zeros((C, GDN_DK), BF16)
    zs = jnp.zeros((GDN_DK, GDN_DK), BF16)
    diag = lambda a, b, z: jnp.concatenate([jnp.concatenate([a, z], axis=1),
                                            jnp.concatenate([z, b], axis=1)], axis=0)
    order = range(chunks - 1, -1, -1) if rev else range(chunks)
    for ci in order:
        rows = slice(ci * C, (ci + 1) * C)
        last = ci * C + (0 if rev else C - 1)
        cols = [slice(h * LANES, (h + 1) * LANES) for h in states]
        both = [slice(2 * j * LANES, (2 * j + 2) * LANES) for j in range(heads)]
        s_old = [s_ref[h] for h in states]
        wqs = [_dot(jnp.concatenate([w_ref[rows, both[j]], qd_ref[rows, both[j]]], axis=0),
                    diag(s_old[2 * j].astype(BF16), s_old[2 * j + 1].astype(BF16), zs)) for j in range(heads)]
        vns = [(u_ref[rows, both[j]].astype(F32) - wqs[j][:C]).astype(BF16) for j in range(heads)]
        vn = [vns[h // 2][:, (h % 2) * LANES:(h % 2 + 1) * LANES] for h in states]
        pvs = [_dot(p_ref[rows, j * LANES:(j + 1) * LANES], diag(vn[2 * j], vn[2 * j + 1], zc))
               for j in range(heads)]
        kvs = [_dot(kdt_ref[ci // 2, cols[h], :], jnp.concatenate([zc, vn[h]] if ci % 2 else [vn[h], zc], axis=0))
               for h in states]
        for h in states:
            lg = 8 * (h // 2) + _gate_lane(d, h % 2, True)
            s_ref[h] = s_old[h] * jnp.exp(gt_ref[lg:lg + 1, last:last + 1]) + kvs[h]
            half = slice((h % 2) * LANES, (h % 2 + 1) * LANES)
            o = wqs[h // 2][C:, half] + pvs[h // 2][:, half]
            if rev:
                tot = of_ref[rows, cols[h]].astype(F32) + o
                gate = _silu(z_ref[rows, cols[h]].astype(F32))
                o_ref[rows, cols[h]] = (_rms(tot, nw_ref[...]) * gate).astype(BF16)
            else:
                o_ref[rows, cols[h]] = o.astype(BF16)


def _gdn_delta_sweep(rev, u, w, p, qd, kdt, gsum_t, batch, chunks, heads, extra=None):
    T = u.shape[0]
    R = chunks * GDN_CHUNK
    nb = T // batch // R
    assert chunks % 2 == 0
    rows = (lambda b, n: b * nb + (nb - 1 - n)) if rev else (lambda b, n: b * nb + n)
    qk_w = heads * LANES
    v_w = 2 * heads * LANES
    wide = pl.BlockSpec((R, v_w), lambda b, h, n: (rows(b, n), h))
    narrow = pl.BlockSpec((R, qk_w), lambda b, h, n: (rows(b, n), h))
    in_specs = [
        wide, wide, narrow, wide,
        pl.BlockSpec((chunks // 2, v_w, 2 * GDN_CHUNK), lambda b, h, n: (rows(b, n), h, 0)),
        pl.BlockSpec((8 * heads, R), lambda b, h, n: (h, rows(b, n))),
    ]
    args = [u, w, p, qd, kdt, gsum_t]
    if rev:
        o_fwd, pre, norm_w = extra
        z_block0 = GDN_CONV_DIM // v_w
        in_specs += [
            wide,
            pl.BlockSpec((R, v_w), lambda b, h, n: (rows(b, n), z_block0 + h)),
            pl.BlockSpec((1, LANES), lambda b, h, n: (0, 0)),
        ]
        args += [o_fwd, pre, norm_w]
    return pl.pallas_call(
        functools.partial(_delta_sweep_kernel, rev, chunks, heads),
        grid=(batch, GDN_QK_HEADS // heads, nb),
        in_specs=in_specs,
        out_specs=wide,
        out_shape=jax.ShapeDtypeStruct((T, GDN_V_DIM), BF16),
        scratch_shapes=[pltpu.VMEM((2 * heads, GDN_DK, GDN_DK), F32)],
        compiler_params=_params(("parallel", "parallel", "arbitrary")),
        name="gdn_sweep_bwd" if rev else "gdn_sweep_fwd",
    )(*args)


def _out_proj_kernel(transposed, a_ref, w_ref, nw_ref, x_ref, o_ref):
    m = _dot_tn(a_ref[...], w_ref[...]) if transposed else _dot(a_ref[...], w_ref[...])
    o_ref[...] = x_ref[...] + _rms(m, nw_ref[...])


def _out_proj(a, w, nw, x, tm, transposed=False):
    K, D = w.shape
    T = x.shape[0]
    a_spec = pl.BlockSpec((K, tm), lambda i: (0, i)) if transposed else pl.BlockSpec((tm, K), lambda i: (i, 0))
    return pl.pallas_call(
        functools.partial(_out_proj_kernel, transposed),
        grid=(T // tm,),
        in_specs=[
            a_spec,
            _resident((K, D), lambda i: (0, 0)),
            pl.BlockSpec((1, D), lambda i: (0, 0)),
            pl.BlockSpec((tm, D), lambda i: (i, 0)),
        ],
        out_specs=pl.BlockSpec((tm, D), lambda i: (i, 0)),
        out_shape=jax.ShapeDtypeStruct((T, D), F32),
        compiler_params=_params(("parallel",)),
        name="out_proj_t" if transposed else "out_proj",
    )(a, w, nw, x)


def _ffn_streamed_kernel(layer, tf, x_ref, nw1_ref, w1_hbm, w2_hbm, nw2_ref, o_ref,
                         w1_buf, w2_buf, sem, xn_ref, acc_ref):
    i = pl.program_id(0)
    nj = w1_hbm.shape[2] // tf

    def copies(j, slot):
        return (pltpu.make_async_copy(w1_hbm.at[layer, :, pl.ds(j * tf, tf)], w1_buf.at[slot], sem.at[0, slot]),
                pltpu.make_async_copy(w2_hbm.at[layer, pl.ds(j * tf, tf), :], w2_buf.at[slot], sem.at[1, slot]))

    def start(j, slot):
        for c in copies(j, slot):
            c.start()

    @pl.when(i == 0)
    def _():
        start(0, 0)

    xn_ref[...] = _rms(x_ref[...], nw1_ref[...]).astype(BF16)
    for j in range(nj):
        slot = j % 2
        for c in copies(j, slot):
            c.wait()
        if j + 1 < nj:
            start(j + 1, 1 - slot)
        else:
            @pl.when(i + 1 < pl.num_programs(0))
            def _():
                start(0, 1 - slot)
        h = jnp.maximum(_dot(xn_ref[...], w1_buf[slot]), 0.0)
        part = _dot((h * h).astype(BF16), w2_buf[slot])
        if j == 0:
            acc_ref[...] = part
        else:
            acc_ref[...] += part
    o_ref[...] = x_ref[...] + _rms(acc_ref[...], nw2_ref[...])


def _ffn_streamed(x, nw1, w1, w2, nw2, layer, tm, tf):
    T, D = x.shape
    assert (w1.shape[2] // tf) % 2 == 0
    return pl.pallas_call(
        functools.partial(_ffn_streamed_kernel, layer, tf),
        grid=(T // tm,),
        in_specs=[
            pl.BlockSpec((tm, D), lambda i: (i, 0)),
            pl.BlockSpec((1, D), lambda i: (0, 0)),
            pl.BlockSpec(memory_space=pl.ANY),
            pl.BlockSpec(memory_space=pl.ANY),
            pl.BlockSpec((1, D), lambda i: (0, 0)),
        ],
        out_specs=pl.BlockSpec((tm, D), lambda i: (i, 0)),
        out_shape=jax.ShapeDtypeStruct((T, D), F32),
        scratch_shapes=[pltpu.VMEM((2, D, tf), BF16), pltpu.VMEM((2, tf, D), BF16),
                        pltpu.SemaphoreType.DMA((2, 2)),
                        pltpu.VMEM((tm, D), BF16), pltpu.VMEM((tm, D), F32)],
        compiler_params=_params(("arbitrary",)),
        name="ffn_streamed",
    )(x, nw1, w1, w2, nw2)


def _mla_proj_kernel(x_ref, nw_ref, wa_ref, qn_ref, kvn_ref, wqt_ref, wkn_ref, wvt_ref,
                     cos_ref, sin_ref, cost_ref, sint_ref, qt_ref, k_ref, vt_ref):
    xn = _rms(x_ref[...], nw_ref[...]).astype(BF16)
    a = _dot(xn, wa_ref[...])
    cq = _rms(a[:, :MLA_Q_RANK], qn_ref[...]).astype(BF16)
    ckv = _rms(a[:, MLA_Q_RANK:MLA_Q_RANK + MLA_KV_RANK], kvn_ref[...]).astype(BF16)

    kr = a[:, MLA_Q_RANK + MLA_KV_RANK:]
    k_rope = (kr * cos_ref[...] + pltpu.roll(kr, LANES // 2, axis=1) * sin_ref[...]).astype(BF16)
    kn = _dot(ckv, wkn_ref[...])
    for h in range(MLA_HEADS):
        c0 = h * MLA_D_PAD
        k_ref[:, c0:c0 + LANES] = kn[:, h * LANES:(h + 1) * LANES].astype(BF16)
        k_ref[:, c0 + LANES:c0 + 2 * LANES] = k_rope

    scale = MLA_D_QK ** -0.5 * math.log2(math.e)
    qt = _dot_nt(wqt_ref[...], cq)
    cost = cost_ref[...]
    sint = sint_ref[...]
    half = LANES // 2
    for h in range(MLA_HEADS):
        r0 = h * MLA_D_PAD
        qt_ref[r0:r0 + LANES, :] = (qt[r0:r0 + LANES] * scale).astype(BF16)
        blk = qt[r0 + LANES:r0 + 2 * LANES]
        swapped = jnp.concatenate([blk[half:], blk[:half]], axis=0)
        qt_ref[r0 + LANES:r0 + 2 * LANES, :] = ((blk * cost + swapped * sint) * scale).astype(BF16)

    vt = _dot_nt(wvt_ref[...], ckv)
    ones = jnp.ones((MLA_V_ROWS - MLA_D_V, vt.shape[1]), BF16)
    for h in range(MLA_HEADS):
        r0 = h * MLA_V_ROWS
        vt_ref[r0:r0 + MLA_D_V, :] = vt[h * MLA_D_V:(h + 1) * MLA_D_V].astype(BF16)
        vt_ref[r0 + MLA_D_V:r0 + MLA_V_ROWS, :] = ones


def _mla_proj(x, nw, wa, qn, kvn, wqt, wkn, wvt, cos, sin, seq_len, tm):
    T, D = x.shape
    pos_blocks = seq_len // tm
    row = lambda i: (i, 0)
    col = lambda i: (0, i)
    fixed = lambda i: (0, 0)
    return pl.pallas_call(
        _mla_proj_kernel,
        grid=(T // tm,),
        in_specs=[
            pl.BlockSpec((tm, D), row),
            pl.BlockSpec((1, D), fixed),
            _resident(wa.shape, fixed),
            pl.BlockSpec((1, MLA_Q_RANK), fixed),
            pl.BlockSpec((1, MLA_KV_RANK), fixed),
            _resident(wqt.shape, fixed),
            _resident(wkn.shape, fixed),
            _resident(wvt.shape, fixed),
            pl.BlockSpec((tm, LANES), lambda i: (i % pos_blocks, 0)),
            pl.BlockSpec((tm, LANES), lambda i: (i % pos_blocks, 0)),
            pl.BlockSpec((LANES, tm), lambda i: (0, i % pos_blocks)),
            pl.BlockSpec((LANES, tm), lambda i: (0, i % pos_blocks)),
        ],
        out_specs=[
            pl.BlockSpec((MLA_HEADS * MLA_D_PAD, tm), col),
            pl.BlockSpec((tm, MLA_HEADS * MLA_D_PAD), row),
            pl.BlockSpec((MLA_HEADS * MLA_V_ROWS, tm), col),
        ],
        out_shape=[
            jax.ShapeDtypeStruct((MLA_HEADS * MLA_D_PAD, T), BF16),
            jax.ShapeDtypeStruct((T, MLA_HEADS * MLA_D_PAD), BF16),
            jax.ShapeDtypeStruct((MLA_HEADS * MLA_V_ROWS, T), BF16),
        ],
        compiler_params=_params(("parallel",)),
        name="mla_proj",
    )(x, nw, wa, qn, kvn, wqt, wkn, wvt, cos, sin, cos.T, sin.T)


ATTN_SOFTMAX_ROWS = 64
ATTN_SCORE_SLOTS = 4


def _attn_kernel(tk, tq, qt_ref, k_ref, vt_ref, ot_ref, s_ref, p_ref, acc_ref):
    n = k_ref.shape[0] // tk
    blocks = qt_ref.shape[1] // tq
    R = ATTN_SOFTMAX_ROWS

    def keys(g):
        return slice((g % n) * tk, (g % n + 1) * tk)

    def queries(g):
        return slice((g // n) * tq, (g // n + 1) * tq)

    def scores(g):
        s_ref[g % ATTN_SCORE_SLOTS] = _dot(k_ref[keys(g), :], qt_ref[:, queries(g)])

    def weighted(g, a):
        acc = a * acc_ref[g // n] + _dot(vt_ref[:, keys(g)], p_ref[g % 2])
        if g % n == n - 1:
            ot_ref[:, queries(g)] = (acc[:MLA_D_V] / acc[MLA_D_V:MLA_D_V + 1]).astype(BF16)
        else:
            acc_ref[g // n] = acc

    def softmax(g, m):
        slot = g % ATTN_SCORE_SLOTS
        if g % n == 0:
            m = jnp.full((1, tq), -jnp.inf, F32)
        mx = s_ref[slot, 0:R, :]
        for r in range(R, tk, R):
            mx = jnp.maximum(mx, s_ref[slot, r:r + R, :])
        m_new = jnp.maximum(m, jnp.max(mx, axis=0, keepdims=True))
        for r in range(0, tk, R):
            p_ref[g % 2, r:r + R, :] = jnp.exp2(s_ref[slot, r:r + R, :] - m_new).astype(BF16)
        return m_new, jnp.exp2(m - m_new)

    total = blocks * n
    acc_ref[...] = jnp.zeros_like(acc_ref)
    scores(0)
    scores(1)
    m, a = softmax(0, None)
    scores(2)
    for g in range(1, total):
        if g + 2 < total:
            scores(g + 2)
        weighted(g - 1, a)
        m, a = softmax(g, m)
    weighted(total - 1, a)


def _attention(qt, k, vt, batch, tq, tk, blocks):
    T = k.shape[0]
    L = T // batch
    nq = L // (tq * blocks)
    assert blocks * (L // tk) >= 3
    return pl.pallas_call(
        functools.partial(_attn_kernel, tk, tq),
        grid=(batch, MLA_HEADS, nq),
        in_specs=[
            pl.BlockSpec((MLA_D_PAD, blocks * tq), lambda b, h, i: (h, b * nq + i)),
            pl.BlockSpec((L, MLA_D_PAD), lambda b, h, i: (b, h)),
            pl.BlockSpec((MLA_V_ROWS, L), lambda b, h, i: (h, b)),
        ],
        out_specs=pl.BlockSpec((MLA_D_V, blocks * tq), lambda b, h, i: (h, b * nq + i)),
        out_shape=jax.ShapeDtypeStruct((MLA_HEADS * MLA_D_V, T), BF16),
        scratch_shapes=[pltpu.VMEM((ATTN_SCORE_SLOTS, tk, tq), F32), pltpu.VMEM((2, tk, tq), BF16),
                        pltpu.VMEM((blocks, MLA_V_ROWS, tq), F32)],
        compiler_params=_params(("parallel", "parallel", "arbitrary")),
        name="attention",
    )(qt, k, vt)


def _gate_lane_perm():
    lanes = jnp.arange(GDN_GATE_LANES)
    q, e = lanes // 8, lanes % 8
    return GDN_V_HEADS * (e // 2) + 2 * q + e % 2


def _gate_lane_params(a_log, dt_bias):
    lanes = jnp.arange(GDN_GATE_LANES)
    q, e = lanes // 8, lanes % 8
    head = 2 * q + e % 2
    direction = jnp.maximum(e // 2 - 2, 0)
    is_decay = e >= 4
    neg_a = jnp.where(is_decay, -jnp.exp(a_log.astype(F32))[direction, head], 0.0)
    dt = jnp.where(is_decay, dt_bias.astype(F32)[direction, head], 0.0)
    return neg_a, dt


def _pad_rope_cols(w):
    half = MLA_D_ROPE // 2
    z = jnp.zeros(w.shape[:-1] + (half,), w.dtype)
    return jnp.concatenate([w[..., :half], z, w[..., half:], z], axis=-1)


def _rope_tables(L):
    half = MLA_D_ROPE // 2
    inv_freq = ROPE_THETA ** (-jnp.arange(half, dtype=F32) / half)
    ang = jnp.arange(L, dtype=F32)[:, None] * inv_freq[None, :]
    c, s = jnp.cos(ang), jnp.sin(ang)
    z = jnp.zeros_like(c)
    return jnp.concatenate([c, z, c, z], axis=-1), jnp.concatenate([-s, z, s, z], axis=-1)


def _prepare(p):
    w = {}
    g_in = p['gdn_w_in'][0]
    w['gdn_main'] = g_in[:, :GDN_MAIN_DIM].astype(BF16)
    wg = g_in[:, GDN_MAIN_DIM:][:, _gate_lane_perm()].astype(BF16)
    w['gdn_gate'] = wg
    w['gdn_gate_t'] = wg.T
    w['gdn_neg_a'], w['gdn_dt'] = _gate_lane_params(p['gdn_a_log'][0], p['gdn_dt_bias'][0])
    w['gdn_conv'] = p['gdn_conv_w'][0].astype(F32)
    w['gdn_norm'] = p['gdn_norm_w'][0].reshape(1, GDN_DK).astype(F32)
    w['gdn_out'] = p['gdn_w_out'][0].astype(BF16)

    wa = p['mla_w_a'][0]
    rank = MLA_Q_RANK + MLA_KV_RANK
    w['mla_a'] = jnp.concatenate([wa[:, :rank], _pad_rope_cols(wa[:, rank:])], axis=-1).astype(BF16)
    wq = p['mla_w_q_b'][0].reshape(MLA_Q_RANK, MLA_HEADS, MLA_D_QK)
    wq = jnp.concatenate([wq[..., :MLA_D_NOPE], _pad_rope_cols(wq[..., MLA_D_NOPE:])], axis=-1)
    w['mla_q_t'] = wq.reshape(MLA_Q_RANK, MLA_HEADS * MLA_D_PAD).astype(BF16).T
    wkv = p['mla_w_kv_b'][0].reshape(MLA_KV_RANK, MLA_HEADS, MLA_D_NOPE + MLA_D_V)
    w['mla_kn'] = wkv[..., :MLA_D_NOPE].reshape(MLA_KV_RANK, MLA_HEADS * MLA_D_NOPE).astype(BF16)
    w['mla_v_t'] = wkv[..., MLA_D_NOPE:].reshape(MLA_KV_RANK, MLA_HEADS * MLA_D_V).astype(BF16).T
    w['mla_qn'] = p['mla_q_a_norm'][0].reshape(1, MLA_Q_RANK).astype(F32)
    w['mla_kvn'] = p['mla_kv_a_norm'][0].reshape(1, MLA_KV_RANK).astype(F32)
    w['mla_o'] = p['mla_w_o'][0].astype(BF16)

    w['ffn_in'] = p['ffn_w_in'].astype(BF16)
    w['ffn_out'] = p['ffn_w_out'].astype(BF16)
    for name in ('norm_mix_pre', 'norm_mix_post', 'norm_ffn_pre', 'norm_ffn_post'):
        w[name] = p[name].astype(F32)[:, None, :]
    return w


TILES = dict(
    proj_tm=1024, proj_tn=2048,
    gate_tr=512,
    conv_tr=512, conv_tc=1024,
    solve_chunks=16,
    sweep_chunks=4, sweep_heads=16,
    out_tm=512,
    ffn_tm=512, ffn_tf=1024,
    mla_tm=512,
    attn_tq=256, attn_tk=512, attn_items=64,
)


def _trunk(x3, w, tiles):
    B, L, D = x3.shape
    x = x3.reshape(B * L, D)

    pre, gates, gates_t = _gdn_proj(x, w['norm_mix_pre'][0], w['gdn_main'], w['gdn_gate'], w['gdn_gate_t'],
                                    tiles['proj_tm'], tiles['proj_tn'])
    gsum, gsum_t = _gdn_gates(gates, gates_t, w['gdn_neg_a'], w['gdn_dt'], tiles['gate_tr'])
    qkv = _gdn_conv(pre, w['gdn_conv'], L, tiles['conv_tr'], tiles['conv_tc'])
    solved = _gdn_delta_solve(qkv, gsum, gsum_t, tiles['solve_chunks'])
    sweep = functools.partial(_gdn_delta_sweep, gsum_t=gsum_t, batch=B, chunks=tiles['sweep_chunks'],
                              heads=tiles['sweep_heads'])
    o_fwd = sweep(False, *solved[:5])
    o = sweep(True, *solved[5:], extra=(o_fwd, pre, w['gdn_norm']))
    x = _out_proj(o, w['gdn_out'], w['norm_mix_post'][0], x, tiles['out_tm'])
    x = _ffn_streamed(x, w['norm_ffn_pre'][0], w['ffn_in'], w['ffn_out'], w['norm_ffn_post'][0], 0,
             tiles['ffn_tm'], tiles['ffn_tf'])

    cos, sin = _rope_tables(L)
    qt, k, vt = _mla_proj(x, w['norm_mix_pre'][1], w['mla_a'], w['mla_qn'], w['mla_kvn'], w['mla_q_t'],
                          w['mla_kn'], w['mla_v_t'], cos, sin, L, tiles['mla_tm'])
    blocks = max(1, tiles['attn_items'] * tiles['attn_tk'] // L)
    ot = _attention(qt, k, vt, B, tiles['attn_tq'], tiles['attn_tk'], blocks)
    x = _out_proj(ot, w['mla_o'], w['norm_mix_post'][1], x, tiles['out_tm'], transposed=True)
    x = _ffn_streamed(x, w['norm_ffn_pre'][1], w['ffn_in'], w['ffn_out'], w['norm_ffn_post'][1], 1,
             tiles['ffn_tm'], tiles['ffn_tf'])
    return x.reshape(B, L, D)


def kernel(x_prompt, x_sample, norm_mix_pre, norm_mix_post, norm_ffn_pre, norm_ffn_post, gdn_w_in, gdn_conv_w, gdn_a_log, gdn_dt_bias, gdn_norm_w, gdn_w_out, mla_w_a, mla_q_a_norm, mla_w_q_b, mla_kv_a_norm, mla_w_kv_b, mla_w_o, ffn_w_in, ffn_w_out):
    w = _prepare(dict(
        norm_mix_pre=norm_mix_pre, norm_mix_post=norm_mix_post, norm_ffn_pre=norm_ffn_pre,
        norm_ffn_post=norm_ffn_post, gdn_w_in=gdn_w_in, gdn_conv_w=gdn_conv_w, gdn_a_log=gdn_a_log,
        gdn_dt_bias=gdn_dt_bias, gdn_norm_w=gdn_norm_w, gdn_w_out=gdn_w_out, mla_w_a=mla_w_a,
        mla_q_a_norm=mla_q_a_norm, mla_w_q_b=mla_w_q_b, mla_kv_a_norm=mla_kv_a_norm,
        mla_w_kv_b=mla_w_kv_b, mla_w_o=mla_w_o, ffn_w_in=ffn_w_in, ffn_w_out=ffn_w_out))
    return _trunk(x_prompt, w, TILES), _trunk(x_sample, w, TILES)
```

```python
import functools
import math

import jax
import jax.numpy as jnp
from jax import lax
from jax.experimental import pallas as pl
from jax.experimental.pallas import tpu as pltpu

F32 = jnp.float32
BF16 = jnp.bfloat16

RMS_EPS = 1e-6
L2_EPS = 1e-6
LANES = 128
BF16_ROWS = 16

GDN_QK_HEADS = 16
GDN_V_HEADS = 32
GDN_DK = 128
GDN_CONV = 5
GDN_CHUNK = 64
GDN_Q_DIM = GDN_QK_HEADS * GDN_DK
GDN_V_DIM = GDN_V_HEADS * GDN_DK
GDN_CONV_DIM = 2 * GDN_Q_DIM + GDN_V_DIM
GDN_MAIN_DIM = GDN_CONV_DIM + GDN_V_DIM
GDN_GATE_LANES = 4 * GDN_V_HEADS

MLA_HEADS = 16
MLA_Q_RANK = 768
MLA_KV_RANK = 512
MLA_D_NOPE = 128
MLA_D_ROPE = 64
MLA_D_V = 128
MLA_D_QK = MLA_D_NOPE + MLA_D_ROPE
MLA_D_PAD = 2 * LANES
MLA_V_ROWS = MLA_D_V + BF16_ROWS
ROPE_THETA = 10000.0

VMEM_LIMIT = 56 * 1024 * 1024


def _params(sem):
    return pltpu.CompilerParams(dimension_semantics=sem, vmem_limit_bytes=VMEM_LIMIT)


def _resident(shape, index_map):
    return pl.BlockSpec(shape, index_map, pipeline_mode=pl.Buffered(1))


def _rms(x, w):
    return x * lax.rsqrt(jnp.mean(x * x, axis=-1, keepdims=True) + RMS_EPS) * w


def _dot(a, b):
    return jnp.dot(a, b, preferred_element_type=F32)


def _dot_nt(a, b):
    return lax.dot_general(a, b, (((1,), (1,)), ((), ())), preferred_element_type=F32)


def _dot_tn(a, b):
    return lax.dot_general(a, b, (((0,), (0,)), ((), ())), preferred_element_type=F32)


def _split3(x):
    hi = x.astype(BF16)
    r1 = x - hi.astype(F32)
    mid = r1.astype(BF16)
    lo = (r1 - mid.astype(F32)).astype(BF16)
    return hi, mid, lo


def _silu(x):
    return x * (1.0 / (1.0 + jnp.exp(-x)))


def _gdn_proj_kernel(x_ref, nw_ref, w_ref, wg_ref, wgt_ref, o_ref, g_ref, gt_ref, xn_ref):
    @pl.when(pl.program_id(1) == 0)
    def _():
        xn = _rms(x_ref[...], nw_ref[...]).astype(BF16)
        xn_ref[...] = xn
        g_ref[...] = _dot(xn, wg_ref[...])
        gt_ref[...] = _dot_nt(wgt_ref[...], xn)

    o_ref[...] = _dot(xn_ref[...], w_ref[...]).astype(BF16)


def _gdn_proj(x, nw, w, wg, wgt, tm, tn):
    T, D = x.shape
    N = w.shape[1]
    return pl.pallas_call(
        _gdn_proj_kernel,
        grid=(T // tm, N // tn),
        in_specs=[
            pl.BlockSpec((tm, D), lambda i, j: (i, 0)),
            pl.BlockSpec((1, D), lambda i, j: (0, 0)),
            pl.BlockSpec((D, tn), lambda i, j: (0, j)),
            pl.BlockSpec((D, LANES), lambda i, j: (0, 0)),
            pl.BlockSpec((LANES, D), lambda i, j: (0, 0)),
        ],
        out_specs=[
            pl.BlockSpec((tm, tn), lambda i, j: (i, j)),
            pl.BlockSpec((tm, LANES), lambda i, j: (i, 0)),
            pl.BlockSpec((LANES, tm), lambda i, j: (0, i)),
        ],
        out_shape=[
            jax.ShapeDtypeStruct((T, N), BF16),
            jax.ShapeDtypeStruct((T, LANES), F32),
            jax.ShapeDtypeStruct((LANES, T), F32),
        ],
        scratch_shapes=[pltpu.VMEM((tm, D), BF16)],
        compiler_params=_params(("parallel", "arbitrary")),
        name="gdn_proj",
    )(x, nw, w, wg, wgt)


def _softplus(y):
    return jnp.maximum(y, 0.0) + jnp.log1p(jnp.exp(-jnp.abs(y)))


def _gate_values(x, neg_a, dt):
    beta = 1.0 / (1.0 + jnp.exp(-x))
    g = neg_a * _softplus(x + dt)
    return beta, g


def _gdn_gates_kernel(x_ref, xt_ref, na_ref, dt_ref, nat_ref, dtt_ref, o_ref, ot_ref):
    R = x_ref.shape[0]
    ii = lax.broadcasted_iota(jnp.int32, (R, R), 0)
    jj = lax.broadcasted_iota(jnp.int32, (R, R), 1)
    same = (ii // GDN_CHUNK) == (jj // GDN_CHUNK)
    lower = jnp.where(same & (jj <= ii), 1.0, 0.0).astype(BF16)
    upper = jnp.where(same & (jj >= ii), 1.0, 0.0).astype(BF16)

    beta, g = _gate_values(x_ref[...], na_ref[...], dt_ref[...])
    parts = _split3(g)
    cf = sum(_dot(lower, p) for p in parts)
    cb = sum(_dot(upper, p) for p in parts)
    e = lax.broadcasted_iota(jnp.int32, (R, LANES), 1) % 8
    o_ref[...] = jnp.where(e < 4, beta, jnp.where(e < 6, cf, cb))

    beta_t, g_t = _gate_values(xt_ref[...], nat_ref[...], dtt_ref[...])
    parts_t = _split3(g_t)
    cf_t = sum(_dot(p, upper) for p in parts_t)
    cb_t = sum(_dot(p, lower) for p in parts_t)
    e_t = lax.broadcasted_iota(jnp.int32, (LANES, R), 0) % 8
    ot_ref[...] = jnp.where(e_t < 4, beta_t, jnp.where(e_t < 6, cf_t, cb_t))


def _gdn_gates(gates, gates_t, neg_a, dt, tr):
    T = gates.shape[0]
    row = lambda i: (i, 0)
    col = lambda i: (0, i)
    fixed = lambda i: (0, 0)
    return pl.pallas_call(
        _gdn_gates_kernel,
        grid=(T // tr,),
        in_specs=[
            pl.BlockSpec((tr, LANES), row),
            pl.BlockSpec((LANES, tr), col),
            pl.BlockSpec((1, LANES), fixed),
            pl.BlockSpec((1, LANES), fixed),
            pl.BlockSpec((LANES, 1), fixed),
            pl.BlockSpec((LANES, 1), fixed),
        ],
        out_specs=[pl.BlockSpec((tr, LANES), row), pl.BlockSpec((LANES, tr), col)],
        out_shape=[jax.ShapeDtypeStruct((T, LANES), F32), jax.ShapeDtypeStruct((LANES, T), F32)],
        compiler_params=_params(("parallel",)),
        name="gdn_gates",
    )(gates, gates_t, neg_a.reshape(1, LANES), dt.reshape(1, LANES),
      neg_a.reshape(LANES, 1), dt.reshape(LANES, 1))


CONV_HALO = BF16_ROWS
CONV_PAD = 8


def _gdn_conv_kernel(blocks_per_seq, n_q_blocks, n_qk_blocks,
                     prev_ref, main_ref, next_ref, w_ref, o_ref):
    i = pl.program_id(0)
    j = pl.program_id(1)
    tr, tc = main_ref.shape
    r = GDN_CONV // 2
    pos = i % blocks_per_seq
    keep_prev = jnp.where(pos == 0, 0.0, 1.0)
    keep_next = jnp.where(pos == blocks_per_seq - 1, 0.0, 1.0)
    rows = tr + 2 * CONV_PAD

    def conv_silu(cols):
        xs = jnp.concatenate([
            prev_ref[CONV_HALO - CONV_PAD:, cols].astype(F32) * keep_prev,
            main_ref[:, cols].astype(F32),
            next_ref[:CONV_PAD, cols].astype(F32) * keep_next], axis=0)
        w = w_ref[:, cols]
        acc = xs[CONV_PAD:CONV_PAD + tr] * w[r:r + 1, :]
        for t in range(GDN_CONV):
            if t != r:
                shifted = pltpu.roll(xs, (r - t) % rows, axis=0)
                acc = acc + shifted[CONV_PAD:CONV_PAD + tr] * w[t:t + 1, :]
        return _silu(acc)

    @pl.when(j >= n_qk_blocks)
    def _():
        for h in range(tc // LANES):
            cols = slice(h * LANES, (h + 1) * LANES)
            o_ref[:, cols] = conv_silu(cols).astype(BF16)

    @pl.when(j < n_qk_blocks)
    def _():
        q_scale = jnp.where(j < n_q_blocks, GDN_DK ** -0.5, 1.0)
        for h in range(tc // LANES):
            cols = slice(h * LANES, (h + 1) * LANES)
            blk = conv_silu(cols)
            inv = lax.rsqrt(jnp.sum(blk * blk, axis=-1, keepdims=True) + L2_EPS) * q_scale
            o_ref[:, cols] = (blk * inv).astype(BF16)


def _gdn_conv(pre, conv_w, seq_len, tr, tc):
    T = pre.shape[0]
    C = conv_w.shape[1]
    hb = tr // CONV_HALO
    n_halo = T // CONV_HALO
    kern = functools.partial(_gdn_conv_kernel, seq_len // tr, GDN_Q_DIM // tc, 2 * GDN_Q_DIM // tc)
    return pl.pallas_call(
        kern,
        grid=(T // tr, C // tc),
        in_specs=[
            pl.BlockSpec((CONV_HALO, tc), lambda i, j: (jnp.maximum(i * hb - 1, 0), j)),
            pl.BlockSpec((tr, tc), lambda i, j: (i, j)),
            pl.BlockSpec((CONV_HALO, tc), lambda i, j: (jnp.minimum((i + 1) * hb, n_halo - 1), j)),
            pl.BlockSpec((GDN_CONV, tc), lambda i, j: (0, j)),
        ],
        out_specs=pl.BlockSpec((tr, tc), lambda i, j: (i, j)),
        out_shape=jax.ShapeDtypeStruct((T, C), BF16),
        compiler_params=_params(("parallel", "parallel")),
        name="gdn_conv",
    )(pre, pre, pre, conv_w)


NEUMANN_DOUBLINGS = 5


def _gate_lane(direction, v_head, decay):
    return (4 if decay else 0) + 2 * direction + v_head


def _delta_solve_kernel(chunks, q_ref, k_ref, v_ref, g_ref, gt_ref,
                        uf_ref, wf_ref, pf_ref, kf_ref, ub_ref, wb_ref, pb_ref, kb_ref):
    C = GDN_CHUNK
    W = 4 * C
    qh = pl.program_id(1)
    gsel = pltpu.roll(g_ref[...], (LANES - 8 * qh) % LANES, axis=1)
    ii = lax.broadcasted_iota(jnp.int32, (C, W), 0)
    ll = lax.broadcasted_iota(jnp.int32, (C, W), 1)
    jj = ll % C
    group = ll // C
    ahead = jnp.where(group >= 2, jj - ii, ii - jj)
    incl = ahead >= 0
    strict = ahead > 0
    eye = jnp.where(ii == jj, 1.0, 0.0)
    own = [group == p for p in range(4)]
    low_half = lax.broadcasted_iota(jnp.int32, (C, LANES), 1) < C
    outs = ((uf_ref, wf_ref, pf_ref, kf_ref), (ub_ref, wb_ref, pb_ref, kb_ref))
    eye_k = jnp.where(lax.broadcasted_iota(jnp.int32, (GDN_DK, GDN_DK), 0)
                      == lax.broadcasted_iota(jnp.int32, (GDN_DK, GDN_DK), 1), 1.0, 0.0).astype(BF16)
    zeros = jnp.zeros((C, 2 * LANES), BF16)

    def side_by_side(cols):
        return jnp.concatenate([jnp.where(low_half, cols[0], cols[1]),
                                jnp.where(low_half, cols[2], cols[3])], axis=1)

    def block_diag(x):
        return jnp.concatenate([jnp.where(own[p], x, 0.0).astype(BF16) for p in range(4)], axis=0)

    ts, xs, rhs = [], [], []
    for ci in range(chunks):
        rows = slice(ci * C, (ci + 1) * C)
        kc = k_ref[rows, :]
        qc = q_ref[rows, :]
        kq = _dot_nt(jnp.concatenate([kc, qc, eye_k], axis=0), jnp.concatenate([kc] * 4, axis=0))
        betas = [jnp.broadcast_to(gsel[rows, p:p + 1], (C, LANES)) for p in range(4)]
        gccs = [jnp.broadcast_to(gsel[rows, 4 + p:5 + p], (C, LANES)) for p in range(4)]
        g_pair = gt_ref[:, (ci // 2) * 2 * C:(ci // 2 + 1) * 2 * C]
        g_swap = pltpu.roll(g_pair, C, axis=1)
        on_low, on_high = (g_pair, g_swap) if ci % 2 == 0 else (g_swap, g_pair)
        gcrs = [on_low[4 + p:5 + p, :C] for p in range(4)]
        gcr_rows = [jnp.where(low_half[0:1], on_low[4 + p:5 + p], on_high[5 + p:6 + p]) for p in (0, 2)]
        decay = jnp.concatenate([jnp.where(low_half, gccs[p], gccs[p + 1]) - gcr_rows[p // 2]
                                 for p in (0, 2)], axis=1)
        decay = jnp.where(incl, jnp.exp(jnp.where(incl, decay, 0.0)), 0.0)
        a = jnp.where(strict, side_by_side(betas) * kq[:C] * decay, 0.0)
        ts.append(eye - a)
        xs.append(a)
        pm = (kq[C:2 * C] * decay).astype(BF16)
        pf_ref[rows, :] = pm[:, :2 * C]
        pb_ref[rows, :] = pm[:, 2 * C:]
        kcf = kc.astype(F32)
        for p in range(4):
            d, hv = divmod(p, 2)
            cols = slice(hv * LANES, (hv + 1) * LANES)
            gl = gcrs[p][:, 0:1] if d else gcrs[p][:, C - 1:C]
            eg = jnp.exp(gccs[p])
            outs[d][3][ci // 2, cols, (ci % 2) * C:(ci % 2 + 1) * C] = (
                kq[2 * C:, :C] * jnp.exp(gl - gcrs[p])).astype(BF16)
            vb = v_ref[rows, cols].astype(F32) * betas[p]
            kb = kcf * (betas[p] * eg)
            rhs.append(jnp.concatenate([vb, kb], axis=1).astype(BF16))

    xs = [_dot(x.astype(BF16), block_diag(x)) for x in xs]
    for step in range(NEUMANN_DOUBLINGS):
        if step + 1 < NEUMANN_DOUBLINGS:
            both = [_dot(jnp.concatenate([t, x], axis=0).astype(BF16), block_diag(x)) for t, x in zip(ts, xs)]
            ts = [t + b[:C] for t, b in zip(ts, both)]
            xs = [b[C:] for b in both]
        else:
            ts = [t + _dot(t.astype(BF16), block_diag(x)) for t, x in zip(ts, xs)]

    for ci in range(chunks):
        rows = slice(ci * C, (ci + 1) * C)
        for p in range(4):
            d, hv = divmod(p, 2)
            cols = slice(hv * LANES, (hv + 1) * LANES)
            lhs = jnp.where(own[p], ts[ci], 0.0).astype(BF16)
            padded = jnp.concatenate([zeros] * p + [rhs[4 * ci + p]] + [zeros] * (3 - p), axis=0)
            uw = _dot(lhs, padded)
            outs[d][0][rows, cols] = uw[:, :LANES].astype(BF16)
            outs[d][1][rows, cols] = uw[:, LANES:].astype(BF16)


def _gdn_delta_solve(qkv, gsum, gsum_t, chunks):
    T = qkv.shape[0]
    R = chunks * GDN_CHUNK
    k_block0 = GDN_Q_DIM // LANES
    v_block0 = 2 * GDN_Q_DIM // (2 * LANES)
    wide = pl.BlockSpec((R, 2 * LANES), lambda i, h: (i, h))
    narrow = pl.BlockSpec((R, LANES), lambda i, h: (i, h))
    tall = pl.BlockSpec((chunks // 2, 2 * GDN_DK, 2 * GDN_CHUNK), lambda i, h: (i, h, 0))
    uw_shape = jax.ShapeDtypeStruct((T, GDN_V_DIM), BF16)
    p_shape = jax.ShapeDtypeStruct((T, GDN_V_HEADS * GDN_CHUNK), BF16)
    kt_shape = jax.ShapeDtypeStruct((T // (2 * GDN_CHUNK), GDN_V_HEADS * GDN_DK, 2 * GDN_CHUNK), BF16)
    return pl.pallas_call(
        functools.partial(_delta_solve_kernel, chunks),
        grid=(T // R, GDN_QK_HEADS),
        in_specs=[
            narrow,
            pl.BlockSpec((R, LANES), lambda i, h: (i, k_block0 + h)),
            pl.BlockSpec((R, 2 * LANES), lambda i, h: (i, v_block0 + h)),
            pl.BlockSpec((R, LANES), lambda i, h: (i, 0)),
            pl.BlockSpec((8, R), lambda i, h: (h, i)),
        ],
        out_specs=[wide, wide, narrow, tall] * 2,
        out_shape=[uw_shape, uw_shape, p_shape, kt_shape] * 2,
        compiler_params=_params(("parallel", "parallel")),
        name="gdn_delta_solve",
    )(qkv, qkv, qkv, gsum, gsum_t)


def _delta_sweep_kernel(rev, chunks, heads, *refs):
    if rev:
        u_ref, w_ref, p_ref, kdt_ref, gt_ref, q_ref, g_ref, of_ref, z_ref, nw_ref, o_ref, s_ref = refs
    else:
        u_ref, w_ref, p_ref, kdt_ref, gt_ref, q_ref, g_ref, o_ref, s_ref = refs
    C = GDN_CHUNK
    d = 1 if rev else 0

    @pl.when(pl.program_id(2) == 0)
    def _():
        s_ref[...] = jnp.zeros_like(s_ref)

    states = range(2 * heads)
    gsel = pltpu.roll(g_ref[...], (LANES - 8 * heads * pl.program_id(1)) % LANES, axis=1)
    zc = jnp.zeros((C, GDN_DK), BF16)
    zs = jnp.zeros((GDN_DK, GDN_DK), BF16)
    diag = lambda a, b, z: jnp.concatenate([jnp.concatenate([a, z], axis=1),
                                            jnp.concatenate([z, b], axis=1)], axis=0)
    order = range(chunks - 1, -1, -1) if rev else range(chunks)
    for ci in order:
        rows = slice(ci * C, (ci + 1) * C)
        last = ci * C + (0 if rev else C - 1)
        cols = [slice(h * LANES, (h + 1) * LANES) for h in states]
        both = [slice(2 * j * LANES, (2 * j + 2) * LANES) for j in range(heads)]
        s_old = [s_ref[h] for h in states]
        qd = [jnp.concatenate([
            (q_ref[rows, j * LANES:(j + 1) * LANES].astype(F32) * jnp.exp(jnp.broadcast_to(
                gsel[rows, 8 * j + _gate_lane(d, hv, True):8 * j + _gate_lane(d, hv, True) + 1], (C, LANES)))).astype(BF16)
            for hv in range(2)], axis=1) for j in range(heads)]
        wqs = [_dot(jnp.concatenate([w_ref[rows, both[j]], qd[j]], axis=0),
                    diag(s_old[2 * j].astype(BF16), s_old[2 * j + 1].astype(BF16), zs)) for j in range(heads)]
        vns = [(u_ref[rows, both[j]].astype(F32) - wqs[j][:C]).astype(BF16) for j in range(heads)]
        vn = [vns[h // 2][:, (h % 2) * LANES:(h % 2 + 1) * LANES] for h in states]
        pvs = [_dot(p_ref[rows, j * LANES:(j + 1) * LANES], diag(vn[2 * j], vn[2 * j + 1], zc))
               for j in range(heads)]
        kvs = [_dot(kdt_ref[ci // 2, cols[h], :], jnp.concatenate([zc, vn[h]] if ci % 2 else [vn[h], zc], axis=0))
               for h in states]
        for h in states:
            lg = 8 * (h // 2) + _gate_lane(d, h % 2, True)
            s_ref[h] = s_old[h] * jnp.exp(gt_ref[lg:lg + 1, last:last + 1]) + kvs[h]
            half = slice((h % 2) * LANES, (h % 2 + 1) * LANES)
            o = wqs[h // 2][C:, half] + pvs[h // 2][:, half]
            if rev:
                tot = of_ref[rows, cols[h]].astype(F32) + o
                gate = _silu(z_ref[rows, cols[h]].astype(F32))
                o_ref[rows, cols[h]] = (_rms(tot, nw_ref[...]) * gate).astype(BF16)
            else:
                o_ref[rows, cols[h]] = o.astype(BF16)


def _gdn_delta_sweep(rev, u, w, p, kdt, qkv, gsum, gsum_t, batch, chunks, heads, extra=None):
    T = u.shape[0]
    R = chunks * GDN_CHUNK
    nb = T // batch // R
    assert chunks % 2 == 0
    rows = (lambda b, n: b * nb + (nb - 1 - n)) if rev else (lambda b, n: b * nb + n)
    qk_w = heads * LANES
    v_w = 2 * heads * LANES
    wide = pl.BlockSpec((R, v_w), lambda b, h, n: (rows(b, n), h))
    narrow = pl.BlockSpec((R, qk_w), lambda b, h, n: (rows(b, n), h))
    in_specs = [
        wide, wide, narrow,
        pl.BlockSpec((chunks // 2, v_w, 2 * GDN_CHUNK), lambda b, h, n: (rows(b, n), h, 0)),
        pl.BlockSpec((8 * heads, R), lambda b, h, n: (h, rows(b, n))),
        narrow,
        pl.BlockSpec((R, LANES), lambda b, h, n: (rows(b, n), 0)),
    ]
    args = [u, w, p, kdt, gsum_t, qkv, gsum]
    if rev:
        o_fwd, pre, norm_w = extra
        z_block0 = GDN_CONV_DIM // v_w
        in_specs += [
            wide,
            pl.BlockSpec((R, v_w), lambda b, h, n: (rows(b, n), z_block0 + h)),
            pl.BlockSpec((1, LANES), lambda b, h, n: (0, 0)),
        ]
        args += [o_fwd, pre, norm_w]
    return pl.pallas_call(
        functools.partial(_delta_sweep_kernel, rev, chunks, heads),
        grid=(batch, GDN_QK_HEADS // heads, nb),
        in_specs=in_specs,
        out_specs=wide,
        out_shape=jax.ShapeDtypeStruct((T, GDN_V_DIM), BF16),
        scratch_shapes=[pltpu.VMEM((2 * heads, GDN_DK, GDN_DK), F32)],
        compiler_params=_params(("parallel", "parallel", "arbitrary")),
        name="gdn_sweep_bwd" if rev else "gdn_sweep_fwd",
    )(*args)


def _out_proj_kernel(transposed, a_ref, w_ref, nw_ref, x_ref, o_ref):
    m = _dot_tn(a_ref[...], w_ref[...]) if transposed else _dot(a_ref[...], w_ref[...])
    o_ref[...] = x_ref[...] + _rms(m, nw_ref[...])


def _out_proj(a, w, nw, x, tm, transposed=False):
    K, D = w.shape
    T = x.shape[0]
    a_spec = pl.BlockSpec((K, tm), lambda i: (0, i)) if transposed else pl.BlockSpec((tm, K), lambda i: (i, 0))
    return pl.pallas_call(
        functools.partial(_out_proj_kernel, transposed),
        grid=(T // tm,),
        in_specs=[
            a_spec,
            _resident((K, D), lambda i: (0, 0)),
            pl.BlockSpec((1, D), lambda i: (0, 0)),
            pl.BlockSpec((tm, D), lambda i: (i, 0)),
        ],
        out_specs=pl.BlockSpec((tm, D), lambda i: (i, 0)),
        out_shape=jax.ShapeDtypeStruct((T, D), F32),
        compiler_params=_params(("parallel",)),
        name="out_proj_t" if transposed else "out_proj",
    )(a, w, nw, x)


def _ffn_streamed_kernel(layer, tf, x_ref, nw1_ref, w1_hbm, w2_hbm, nw2_ref, o_ref,
                         w1_buf, w2_buf, sem, xn_ref, acc_ref):
    i = pl.program_id(0)
    nj = w1_hbm.shape[2] // tf

    def copies(j, slot):
        return (pltpu.make_async_copy(w1_hbm.at[layer, :, pl.ds(j * tf, tf)], w1_buf.at[slot], sem.at[0, slot]),
                pltpu.make_async_copy(w2_hbm.at[layer, pl.ds(j * tf, tf), :], w2_buf.at[slot], sem.at[1, slot]))

    def start(j, slot):
        for c in copies(j, slot):
            c.start()

    @pl.when(i == 0)
    def _():
        start(0, 0)

    xn_ref[...] = _rms(x_ref[...], nw1_ref[...]).astype(BF16)
    for j in range(nj):
        slot = j % 2
        for c in copies(j, slot):
            c.wait()
        if j + 1 < nj:
            start(j + 1, 1 - slot)
        else:
            @pl.when(i + 1 < pl.num_programs(0))
            def _():
                start(0, 1 - slot)
        h = jnp.maximum(_dot(xn_ref[...], w1_buf[slot]), 0.0)
        part = _dot((h * h).astype(BF16), w2_buf[slot])
        if j == 0:
            acc_ref[...] = part
        else:
            acc_ref[...] += part
    o_ref[...] = x_ref[...] + _rms(acc_ref[...], nw2_ref[...])


def _ffn_streamed(x, nw1, w1, w2, nw2, layer, tm, tf):
    T, D = x.shape
    assert (w1.shape[2] // tf) % 2 == 0
    return pl.pallas_call(
        functools.partial(_ffn_streamed_kernel, layer, tf),
        grid=(T // tm,),
        in_specs=[
            pl.BlockSpec((tm, D), lambda i: (i, 0)),
            pl.BlockSpec((1, D), lambda i: (0, 0)),
            pl.BlockSpec(memory_space=pl.ANY),
            pl.BlockSpec(memory_space=pl.ANY),
            pl.BlockSpec((1, D), lambda i: (0, 0)),
        ],
        out_specs=pl.BlockSpec((tm, D), lambda i: (i, 0)),
        out_shape=jax.ShapeDtypeStruct((T, D), F32),
        scratch_shapes=[pltpu.VMEM((2, D, tf), BF16), pltpu.VMEM((2, tf, D), BF16),
                        pltpu.SemaphoreType.DMA((2, 2)),
                        pltpu.VMEM((tm, D), BF16), pltpu.VMEM((tm, D), F32)],
        compiler_params=_params(("arbitrary",)),
        name="ffn_streamed",
    )(x, nw1, w1, w2, nw2)


def _mla_proj_kernel(x_ref, nw_ref, wa_ref, qn_ref, kvn_ref, wqt_ref, wkn_ref, wvt_ref,
                     cos_ref, sin_ref, cost_ref, sint_ref, qt_ref, k_ref, vt_ref):
    xn = _rms(x_ref[...], nw_ref[...]).astype(BF16)
    a = _dot(xn, wa_ref[...])
    cq = _rms(a[:, :MLA_Q_RANK], qn_ref[...]).astype(BF16)
    ckv = _rms(a[:, MLA_Q_RANK:MLA_Q_RANK + MLA_KV_RANK], kvn_ref[...]).astype(BF16)

    kr = a[:, MLA_Q_RANK + MLA_KV_RANK:]
    k_rope = (kr * cos_ref[...] + pltpu.roll(kr, LANES // 2, axis=1) * sin_ref[...]).astype(BF16)
    kn = _dot(ckv, wkn_ref[...])
    for h in range(MLA_HEADS):
        c0 = h * MLA_D_PAD
        k_ref[:, c0:c0 + LANES] = kn[:, h * LANES:(h + 1) * LANES].astype(BF16)
        k_ref[:, c0 + LANES:c0 + 2 * LANES] = k_rope

    scale = MLA_D_QK ** -0.5 * math.log2(math.e)
    qt = _dot_nt(wqt_ref[...], cq)
    cost = cost_ref[...]
    sint = sint_ref[...]
    half = LANES // 2
    for h in range(MLA_HEADS):
        r0 = h * MLA_D_PAD
        qt_ref[r0:r0 + LANES, :] = (qt[r0:r0 + LANES] * scale).astype(BF16)
        blk = qt[r0 + LANES:r0 + 2 * LANES]
        swapped = jnp.concatenate([blk[half:], blk[:half]], axis=0)
        qt_ref[r0 + LANES:r0 + 2 * LANES, :] = ((blk * cost + swapped * sint) * scale).astype(BF16)

    vt = _dot_nt(wvt_ref[...], ckv)
    ones = jnp.ones((MLA_V_ROWS - MLA_D_V, vt.shape[1]), BF16)
    for h in range(MLA_HEADS):
        r0 = h * MLA_V_ROWS
        vt_ref[r0:r0 + MLA_D_V, :] = vt[h * MLA_D_V:(h + 1) * MLA_D_V].astype(BF16)
        vt_ref[r0 + MLA_D_V:r0 + MLA_V_ROWS, :] = ones


def _mla_proj(x, nw, wa, qn, kvn, wqt, wkn, wvt, cos, sin, seq_len, tm):
    T, D = x.shape
    pos_blocks = seq_len // tm
    row = lambda i: (i, 0)
    col = lambda i: (0, i)
    fixed = lambda i: (0, 0)
    return pl.pallas_call(
        _mla_proj_kernel,
        grid=(T // tm,),
        in_specs=[
            pl.BlockSpec((tm, D), row),
            pl.BlockSpec((1, D), fixed),
            _resident(wa.shape, fixed),
            pl.BlockSpec((1, MLA_Q_RANK), fixed),
            pl.BlockSpec((1, MLA_KV_RANK), fixed),
            _resident(wqt.shape, fixed),
            _resident(wkn.shape, fixed),
            _resident(wvt.shape, fixed),
            pl.BlockSpec((tm, LANES), lambda i: (i % pos_blocks, 0)),
            pl.BlockSpec((tm, LANES), lambda i: (i % pos_blocks, 0)),
            pl.BlockSpec((LANES, tm), lambda i: (0, i % pos_blocks)),
            pl.BlockSpec((LANES, tm), lambda i: (0, i % pos_blocks)),
        ],
        out_specs=[
            pl.BlockSpec((MLA_HEADS * MLA_D_PAD, tm), col),
            pl.BlockSpec((tm, MLA_HEADS * MLA_D_PAD), row),
            pl.BlockSpec((MLA_HEADS * MLA_V_ROWS, tm), col),
        ],
        out_shape=[
            jax.ShapeDtypeStruct((MLA_HEADS * MLA_D_PAD, T), BF16),
            jax.ShapeDtypeStruct((T, MLA_HEADS * MLA_D_PAD), BF16),
            jax.ShapeDtypeStruct((MLA_HEADS * MLA_V_ROWS, T), BF16),
        ],
        compiler_params=_params(("parallel",)),
        name="mla_proj",
    )(x, nw, wa, qn, kvn, wqt, wkn, wvt, cos, sin, cos.T, sin.T)


ATTN_SOFTMAX_ROWS = 64
ATTN_SCORE_SLOTS = 4


def _attn_kernel(tk, tq, qt_ref, k_ref, vt_ref, ot_ref, s_ref, p_ref, acc_ref):
    n = k_ref.shape[0] // tk
    blocks = qt_ref.shape[1] // tq
    R = ATTN_SOFTMAX_ROWS

    def keys(g):
        return slice((g % n) * tk, (g % n + 1) * tk)

    def queries(g):
        return slice((g // n) * tq, (g // n + 1) * tq)

    def scores(g):
        s_ref[g % ATTN_SCORE_SLOTS] = _dot(k_ref[keys(g), :], qt_ref[:, queries(g)])

    def weighted(g, a):
        acc = a * acc_ref[g // n] + _dot(vt_ref[:, keys(g)], p_ref[g % 2])
        if g % n == n - 1:
            ot_ref[:, queries(g)] = (acc[:MLA_D_V] / acc[MLA_D_V:MLA_D_V + 1]).astype(BF16)
        else:
            acc_ref[g // n] = acc

    def softmax(g, m):
        slot = g % ATTN_SCORE_SLOTS
        if g % n == 0:
            m = jnp.full((1, tq), -jnp.inf, F32)
        mx = s_ref[slot, 0:R, :]
        for r in range(R, tk, R):
            mx = jnp.maximum(mx, s_ref[slot, r:r + R, :])
        m_new = jnp.maximum(m, jnp.max(mx, axis=0, keepdims=True))
        for r in range(0, tk, R):
            p_ref[g % 2, r:r + R, :] = jnp.exp2(s_ref[slot, r:r + R, :] - m_new).astype(BF16)
        return m_new, jnp.exp2(m - m_new)

    total = blocks * n
    acc_ref[...] = jnp.zeros_like(acc_ref)
    scores(0)
    scores(1)
    m, a = softmax(0, None)
    scores(2)
    for g in range(1, total):
        if g + 2 < total:
            scores(g + 2)
        weighted(g - 1, a)
        m, a = softmax(g, m)
    weighted(total - 1, a)


def _attention(qt, k, vt, batch, tq, tk, blocks):
    T = k.shape[0]
    L = T // batch
    nq = L // (tq * blocks)
    assert blocks * (L // tk) >= 3
    return pl.pallas_call(
        functools.partial(_attn_kernel, tk, tq),
        grid=(batch, MLA_HEADS, nq),
        in_specs=[
            pl.BlockSpec((MLA_D_PAD, blocks * tq), lambda b, h, i: (h, b * nq + i)),
            pl.BlockSpec((L, MLA_D_PAD), lambda b, h, i: (b, h)),
            pl.BlockSpec((MLA_V_ROWS, L), lambda b, h, i: (h, b)),
        ],
        out_specs=pl.BlockSpec((MLA_D_V, blocks * tq), lambda b, h, i: (h, b * nq + i)),
        out_shape=jax.ShapeDtypeStruct((MLA_HEADS * MLA_D_V, T), BF16),
        scratch_shapes=[pltpu.VMEM((ATTN_SCORE_SLOTS, tk, tq), F32), pltpu.VMEM((2, tk, tq), BF16),
                        pltpu.VMEM((blocks, MLA_V_ROWS, tq), F32)],
        compiler_params=_params(("parallel", "parallel", "arbitrary")),
        name="attention",
    )(qt, k, vt)


def _gate_lane_perm():
    lanes = jnp.arange(GDN_GATE_LANES)
    q, e = lanes // 8, lanes % 8
    return GDN_V_HEADS * (e // 2) + 2 * q + e % 2


def _gate_lane_params(a_log, dt_bias):
    lanes = jnp.arange(GDN_GATE_LANES)
    q, e = lanes // 8, lanes % 8
    head = 2 * q + e % 2
    direction = jnp.maximum(e // 2 - 2, 0)
    is_decay = e >= 4
    neg_a = jnp.where(is_decay, -jnp.exp(a_log.astype(F32))[direction, head], 0.0)
    dt = jnp.where(is_decay, dt_bias.astype(F32)[direction, head], 0.0)
    return neg_a, dt


def _pad_rope_cols(w):
    half = MLA_D_ROPE // 2
    z = jnp.zeros(w.shape[:-1] + (half,), w.dtype)
    return jnp.concatenate([w[..., :half], z, w[..., half:], z], axis=-1)


def _rope_tables(L):
    half = MLA_D_ROPE // 2
    inv_freq = ROPE_THETA ** (-jnp.arange(half, dtype=F32) / half)
    ang = jnp.arange(L, dtype=F32)[:, None] * inv_freq[None, :]
    c, s = jnp.cos(ang), jnp.sin(ang)
    z = jnp.zeros_like(c)
    return jnp.concatenate([c, z, c, z], axis=-1), jnp.concatenate([-s, z, s, z], axis=-1)


def _prepare(p):
    w = {}
    g_in = p['gdn_w_in'][0]
    w['gdn_main'] = g_in[:, :GDN_MAIN_DIM].astype(BF16)
    wg = g_in[:, GDN_MAIN_DIM:][:, _gate_lane_perm()].astype(BF16)
    w['gdn_gate'] = wg
    w['gdn_gate_t'] = wg.T
    w['gdn_neg_a'], w['gdn_dt'] = _gate_lane_params(p['gdn_a_log'][0], p['gdn_dt_bias'][0])
    w['gdn_conv'] = p['gdn_conv_w'][0].astype(F32)
    w['gdn_norm'] = p['gdn_norm_w'][0].reshape(1, GDN_DK).astype(F32)
    w['gdn_out'] = p['gdn_w_out'][0].astype(BF16)

    wa = p['mla_w_a'][0]
    rank = MLA_Q_RANK + MLA_KV_RANK
    w['mla_a'] = jnp.concatenate([wa[:, :rank], _pad_rope_cols(wa[:, rank:])], axis=-1).astype(BF16)
    wq = p['mla_w_q_b'][0].reshape(MLA_Q_RANK, MLA_HEADS, MLA_D_QK)
    wq = jnp.concatenate([wq[..., :MLA_D_NOPE], _pad_rope_cols(wq[..., MLA_D_NOPE:])], axis=-1)
    w['mla_q_t'] = wq.reshape(MLA_Q_RANK, MLA_HEADS * MLA_D_PAD).astype(BF16).T
    wkv = p['mla_w_kv_b'][0].reshape(MLA_KV_RANK, MLA_HEADS, MLA_D_NOPE + MLA_D_V)
    w['mla_kn'] = wkv[..., :MLA_D_NOPE].reshape(MLA_KV_RANK, MLA_HEADS * MLA_D_NOPE).astype(BF16)
    w['mla_v_t'] = wkv[..., MLA_D_NOPE:].reshape(MLA_KV_RANK, MLA_HEADS * MLA_D_V).astype(BF16).T
    w['mla_qn'] = p['mla_q_a_norm'][0].reshape(1, MLA_Q_RANK).astype(F32)
    w['mla_kvn'] = p['mla_kv_a_norm'][0].reshape(1, MLA_KV_RANK).astype(F32)
    w['mla_o'] = p['mla_w_o'][0].astype(BF16)

    w['ffn_in'] = p['ffn_w_in'].astype(BF16)
    w['ffn_out'] = p['ffn_w_out'].astype(BF16)
    for name in ('norm_mix_pre', 'norm_mix_post', 'norm_ffn_pre', 'norm_ffn_post'):
        w[name] = p[name].astype(F32)[:, None, :]
    return w


TILES = dict(
    proj_tm=1024, proj_tn=2048,
    gate_tr=512,
    conv_tr=512, conv_tc=1024,
    solve_chunks=16,
    sweep_chunks=4, sweep_heads=16,
    out_tm=512,
    ffn_tm=512, ffn_tf=1024,
    mla_tm=512,
    attn_tq=256, attn_tk=512, attn_items=64,
)


def _trunk(x3, w, tiles):
    B, L, D = x3.shape
    x = x3.reshape(B * L, D)

    pre, gates, gates_t = _gdn_proj(x, w['norm_mix_pre'][0], w['gdn_main'], w['gdn_gate'], w['gdn_gate_t'],
                                    tiles['proj_tm'], tiles['proj_tn'])
    gsum, gsum_t = _gdn_gates(gates, gates_t, w['gdn_neg_a'], w['gdn_dt'], tiles['gate_tr'])
    qkv = _gdn_conv(pre, w['gdn_conv'], L, tiles['conv_tr'], tiles['conv_tc'])
    solved = _gdn_delta_solve(qkv, gsum, gsum_t, tiles['solve_chunks'])
    sweep = functools.partial(_gdn_delta_sweep, qkv=qkv, gsum=gsum, gsum_t=gsum_t, batch=B, chunks=tiles['sweep_chunks'],
                              heads=tiles['sweep_heads'])
    o_fwd = sweep(False, *solved[:4])
    o = sweep(True, *solved[4:], extra=(o_fwd, pre, w['gdn_norm']))
    x = _out_proj(o, w['gdn_out'], w['norm_mix_post'][0], x, tiles['out_tm'])
    x = _ffn_streamed(x, w['norm_ffn_pre'][0], w['ffn_in'], w['ffn_out'], w['norm_ffn_post'][0], 0,
             tiles['ffn_tm'], tiles['ffn_tf'])

    cos, sin = _rope_tables(L)
    qt, k, vt = _mla_proj(x, w['norm_mix_pre'][1], w['mla_a'], w['mla_qn'], w['mla_kvn'], w['mla_q_t'],
                          w['mla_kn'], w['mla_v_t'], cos, sin, L, tiles['mla_tm'])
    blocks = max(1, tiles['attn_items'] * tiles['attn_tk'] // L)
    ot = _attention(qt, k, vt, B, tiles['attn_tq'], tiles['attn_tk'], blocks)
    x = _out_proj(ot, w['mla_o'], w['norm_mix_post'][1], x, tiles['out_tm'], transposed=True)
    x = _ffn_streamed(x, w['norm_ffn_pre'][1], w['ffn_in'], w['ffn_out'], w['norm_ffn_post'][1], 1,
             tiles['ffn_tm'], tiles['ffn_tf'])
    return x.reshape(B, L, D)


def kernel(x_prompt, x_sample, norm_mix_pre, norm_mix_post, norm_ffn_pre, norm_ffn_post, gdn_w_in, gdn_conv_w, gdn_a_log, gdn_dt_bias, gdn_norm_w, gdn_w_out, mla_w_a, mla_q_a_norm, mla_w_q_b, mla_kv_a_norm, mla_w_kv_b, mla_w_o, ffn_w_in, ffn_w_out):
    w = _prepare(dict(
        norm_mix_pre=norm_mix_pre, norm_mix_post=norm_mix_post, norm_ffn_pre=norm_ffn_pre,
        norm_ffn_post=norm_ffn_post, gdn_w_in=gdn_w_in, gdn_conv_w=gdn_conv_w, gdn_a_log=gdn_a_log,
        gdn_dt_bias=gdn_dt_bias, gdn_norm_w=gdn_norm_w, gdn_w_out=gdn_w_out, mla_w_a=mla_w_a,
        mla_q_a_norm=mla_q_a_norm, mla_w_q_b=mla_w_q_b, mla_kv_a_norm=mla_kv_a_norm,
        mla_w_kv_b=mla_w_kv_b, mla_w_o=mla_w_o, ffn_w_in=ffn_w_in, ffn_w_out=ffn_w_out))
    return _trunk(x_prompt, w, TILES), _trunk(x_sample, w, TILES)
```
